```python
import math
import jax, jax.numpy as jnp
from jax import lax
import numpy as np

D_MODEL = 1024
BATCH = 8
SEQ = 2048
DEPTH = 2
DEC_BATCH = 128
DEC_SEQ = 8
PAST_LEN = 16384
PAGE_SIZE = 128

N_AB = (DEPTH + 1) // 2
N_CD = DEPTH // 2
D_MIX = D_MODEL
EPS = 1e-6
GDN_HEADS = 4
GDN_DK = D_MODEL // 8
GDN_DV = D_MODEL // 8
GDN_CONV = 4
GDN_CHUNK = 64
D_A = GDN_HEADS * GDN_DV
D_QKV = GDN_HEADS * (2 * GDN_DK + GDN_DV)
SGU_CHUNK = 128
SGU_GROUPS = 4
D_B = D_MIX - D_A
SGU_GW = D_B // SGU_GROUPS
D_IN_AB = D_QKV + D_A + 2 * GDN_HEADS + 2 * D_B
AB_SPLITS = [D_QKV, D_QKV + D_A, D_QKV + D_A + GDN_HEADS, D_QKV + D_A + 2 * GDN_HEADS,
             D_QKV + D_A + 2 * GDN_HEADS + D_B]
POOL_WINDOWS = (2, 4, 8, 16)
POOL_GROUPS = len(POOL_WINDOWS)
D_C = D_MIX // 2
POOL_GW = D_C // POOL_GROUPS
POOL_BUF = max(POOL_WINDOWS) - 1
D_D = D_MIX - D_C
S5_GW = 16
S5_GROUPS = D_D // S5_GW
S5_STATE = 64
D_IN_CD = D_C + D_D
D_FF = 4 * D_MODEL
D_PLE = 256

kernel_name = 'hybrid_gdn_sgu_pool_s5_step'


def rmsnorm(x, g):
    xf = x.astype(jnp.float32)
    y = xf * lax.rsqrt(jnp.mean(xf * xf, axis=-1, keepdims=True) + EPS)
    return (y * g.astype(jnp.float32)).astype(x.dtype)


def layernorm(x, g, b):
    xf = x.astype(jnp.float32)
    xc = xf - jnp.mean(xf, axis=-1, keepdims=True)
    var = jnp.mean(xc * xc, axis=-1, keepdims=True)
    return (xc * lax.rsqrt(var + EPS) * g.astype(jnp.float32) + b.astype(jnp.float32)).astype(x.dtype)


def l2norm(x):
    return x * lax.rsqrt(jnp.sum(x * x, axis=-1, keepdims=True) + EPS)


def short_conv(x, buf, w):
    t = x.shape[1]
    xe = jnp.concatenate([buf.astype(x.dtype), x], axis=1)
    y = xe[:, 0:t] * w[0]
    for i in range(1, GDN_CONV):
        y = y + xe[:, i:i + t] * w[i]
    return y, xe[:, t:]


def gated_delta_rule(q, k, v, beta, g, s0):
    bsz, t, nh, dk = q.shape
    dv = v.shape[-1]
    c = min(GDN_CHUNK, t)
    n = -(-t // c)
    pad = n * c - t

    def prep(a):
        a = jnp.pad(a, [(0, 0), (0, pad)] + [(0, 0)] * (a.ndim - 2))
        a = a.reshape((bsz, n, c) + a.shape[2:])
        return jnp.moveaxis(a, 3, 1)

    q, k, v, beta, g = (prep(a) for a in (q, k, v, beta, g))
    gc = jnp.cumsum(g, axis=-1)
    tril = jnp.tril(jnp.ones((c, c), bool))
    strict = jnp.tril(jnp.ones((c, c), bool), -1)
    decay = jnp.exp(jnp.where(tril, gc[..., :, None] - gc[..., None, :], -jnp.inf))
    kk = jnp.einsum('bhnid,bhnjd->bhnij', k, k)
    lmat = jnp.where(strict, beta[..., :, None] * kk * decay, 0.0)
    eye = jnp.eye(c, dtype=jnp.float32)
    tinv = lax.linalg.triangular_solve(eye + lmat, jnp.broadcast_to(eye, lmat.shape),
                                       left_side=True, lower=True, unit_diagonal=True)
    u_base = jnp.einsum('bhnij,bhnje->bhnie', tinv, v * beta[..., None])
    w_dec = jnp.einsum('bhnij,bhnjd->bhnid', tinv, k * (beta * jnp.exp(gc))[..., None])
    qk = jnp.einsum('bhnid,bhnjd->bhnij', q, k) * decay
    g_last = gc[..., -1]
    k_tail = k * jnp.exp(g_last[..., None] - gc)[..., None]
    q_head = q * jnp.exp(gc)[..., None]

    def step(s, xs):
        q_h, qk_i, u_b, w_d, k_t, g_l = xs
        u = u_b - jnp.einsum('bhid,bhde->bhie', w_d, s)
        o = jnp.einsum('bhid,bhde->bhie', q_h, s) + jnp.einsum('bhij,bhje->bhie', qk_i, u)
        s = s * jnp.exp(g_l)[..., None, None] + jnp.einsum('bhid,bhie->bhde', k_t, u)
        return s, o

    xs = tuple(jnp.moveaxis(a, 2, 0) for a in (q_head, qk, u_base, w_dec, k_tail, g_last))
    s_fin, o = lax.scan(step, s0, xs)
    o = jnp.moveaxis(o, 0, 2).reshape(bsz, nh, n * c, dv)[:, :, :t]
    return jnp.transpose(o, (0, 2, 1, 3)), s_fin


def chunk_sgu(zu, zv, ln_g, ln_b, w_sp, b_sp):
    bsz, t, _ = zu.shape
    c = SGU_CHUNK
    u = jax.nn.gelu(zu)
    v = layernorm(jax.nn.gelu(zv), ln_g, ln_b)
    n = -(-t // c)
    pad = n * c - t
    vc = jnp.pad(v, ((0, 0), (0, pad), (0, 0))).reshape(bsz, n, c, SGU_GROUPS, SGU_GW)
    w = jnp.where(jnp.tril(jnp.ones((c, c), bool)), w_sp, 0.0)
    mixed = jnp.einsum('gts,bnsgc->bntgc', w, vc) + b_sp.T[None, None, :, :, None]
    mixed = mixed.reshape(bsz, n * c, D_B)[:, :t]
    start = ((t - 1) // c) * c
    return u * mixed.astype(u.dtype), v[:, start:]


def pool_mixer(x, buf, pos0, w_pool, scale):
    bsz, t, _ = x.shape
    xe = jnp.concatenate([buf.astype(x.dtype), x], axis=1)
    cs = jnp.pad(jnp.cumsum(xe.astype(jnp.float32), axis=1), ((0, 0), (1, 0), (0, 0)))
    pos = pos0 + jnp.arange(t)
    xf = x.astype(jnp.float32)
    e0 = POOL_BUF + 1
    outs = []
    for gi, win in enumerate(POOL_WINDOWS):
        lo, hi = gi * POOL_GW, (gi + 1) * POOL_GW
        s = cs[:, e0:e0 + t, lo:hi] - cs[:, e0 - win:e0 - win + t, lo:hi]
        cnt = jnp.minimum(win, pos + 1).astype(jnp.float32)[None, :, None]
        m = s / cnt - xf[:, :, lo:hi]
        outs.append(m @ w_pool[gi].astype(jnp.float32))
    y = jnp.concatenate(outs, axis=-1) * scale.astype(jnp.float32)
    return y.astype(x.dtype), xe[:, t:]


def s5_mixer(x, st_re, st_im, lam_re, lam_im, log_dt, b_re, b_im, c_re, c_im, d_skip, w_glu, b_glu):
    bsz, t, _ = x.shape
    f32 = jnp.float32
    u = x.astype(f32).reshape(bsz, t, S5_GROUPS, S5_GW)
    lam = lax.complex(lam_re.astype(f32), lam_im.astype(f32))
    dt = jnp.exp(log_dt.astype(f32))[:, None]
    lam_bar = jnp.exp(lam * dt)
    b_bar = ((lam_bar - 1.0) / lam)[..., None] * lax.complex(b_re.astype(f32), b_im.astype(f32))
    bu = jnp.einsum('gnc,btgc->btgn', b_bar, u.astype(jnp.complex64))
    s0 = lax.complex(st_re.astype(f32), st_im.astype(f32))
    bu = bu.at[:, 0].add(lam_bar * s0)
    a = jnp.broadcast_to(lam_bar, bu.shape)

    def comb(e1, e2):
        a1, b1 = e1
        a2, b2 = e2
        return a1 * a2, a2 * b1 + b2

    _, s = lax.associative_scan(comb, (a, bu), axis=1)
    cm = lax.complex(c_re.astype(f32), c_im.astype(f32))
    y = jnp.real(jnp.einsum('gcn,btgn->btgc', cm, s)) + d_skip.astype(f32).reshape(S5_GROUPS, S5_GW) * u
    y = jax.nn.gelu(y.reshape(bsz, t, D_D))
    y = y * jax.nn.sigmoid(y @ w_glu.astype(f32) + b_glu.astype(f32))
    s_last = s[:, -1]
    return y.astype(x.dtype), jnp.real(s_last), jnp.imag(s_last)


def layer_ab(h, conv_buf, s0, norm_mix, w_in, conv_w, a_log, dt_bias, norm_o,
             ln_g, ln_b, w_sp, b_sp, w_out):
    f32 = jnp.float32
    bsz, t, _ = h.shape
    z = rmsnorm(h, norm_mix) @ w_in
    qkv, zg, zb, za, zu, zv = jnp.split(z, AB_SPLITS, axis=-1)
    qkv, new_buf = short_conv(qkv, conv_buf, conv_w)
    qkv = jax.nn.silu(qkv).astype(f32)
    q, k, v = jnp.split(qkv, [GDN_HEADS * GDN_DK, 2 * GDN_HEADS * GDN_DK], axis=-1)
    q = l2norm(q.reshape(bsz, t, GDN_HEADS, GDN_DK)) * (GDN_DK ** -0.5)
    k = l2norm(k.reshape(bsz, t, GDN_HEADS, GDN_DK))
    v = v.reshape(bsz, t, GDN_HEADS, GDN_DV)
    beta = jax.nn.sigmoid(zb.astype(f32))
    g = -jnp.exp(a_log.astype(f32)) * jax.nn.softplus(za.astype(f32) + dt_bias.astype(f32))
    o, s_new = gated_delta_rule(q, k, v, beta, g, s0.astype(f32))
    o = rmsnorm(o, norm_o) * jax.nn.silu(zg.astype(f32).reshape(bsz, t, GDN_HEADS, GDN_DV))
    o_a = o.reshape(bsz, t, D_A).astype(h.dtype)
    o_b, v_rows = chunk_sgu(zu, zv, ln_g, ln_b, w_sp, b_sp)
    y = jnp.concatenate([o_a, o_b.astype(h.dtype)], axis=-1) @ w_out
    return h + y, new_buf, s_new, v_rows


def layer_cd(h, pool_buf, st_re, st_im, pos0, norm_mix, w_in, w_pool, pool_scale,
             lam_re, lam_im, log_dt, b_re, b_im, c_re, c_im, d_skip, w_glu, b_glu, w_out):
    z = rmsnorm(h, norm_mix) @ w_in
    xc, xd = jnp.split(z, [D_C], axis=-1)
    o_c, new_pool = pool_mixer(xc, pool_buf, pos0, w_pool, pool_scale)
    o_d, s_re, s_im = s5_mixer(xd, st_re, st_im, lam_re, lam_im, log_dt, b_re, b_im,
                               c_re, c_im, d_skip, w_glu, b_glu)
    y = jnp.concatenate([o_c, o_d], axis=-1) @ w_out
    return h + y, new_pool, s_re, s_im


def channel_mixer(h, norm_g, w_up, w_down):
    a = jax.nn.relu(rmsnorm(h, norm_g) @ w_up)
    return h + (a * a) @ w_down


def per_layer_embed(h, p, norm_g, w_proj, w_gate):
    gate = jax.nn.sigmoid(rmsnorm(h, norm_g) @ w_gate)
    return h + (p.astype(h.dtype) @ w_proj) * gate


def trunk(h, p, conv0, delta0, pool0, s5re0, s5im0, pos0, wt):
    convs, deltas, vrows, pools, s5res, s5ims = [], [], [], [], [], []
    for i in range(DEPTH):
        j = i // 2
        if i % 2 == 0:
            h, cb, sd, vr = layer_ab(h, conv0[j], delta0[j], wt['norm_mix'][i], wt['w_in_ab'][j],
                                     wt['conv_qkv'][j], wt['a_log'][j], wt['dt_bias'][j], wt['norm_o'][j],
                                     wt['ln_v_gain'][j], wt['ln_v_bias'][j], wt['w_spatial'][j],
                                     wt['b_spatial'][j], wt['w_out_ab'][j])
            convs.append(cb)
            deltas.append(sd)
            vrows.append(vr)
        else:
            h, pb, sre, sim = layer_cd(h, pool0[j], s5re0[j], s5im0[j], pos0, wt['norm_mix'][i],
                                       wt['w_in_cd'][j], wt['w_pool'][j], wt['pool_scale'][j],
                                       wt['lam_re'][j], wt['lam_im'][j], wt['log_dt'][j],
                                       wt['b_re'][j], wt['b_im'][j], wt['c_re'][j], wt['c_im'][j],
                                       wt['d_skip'][j], wt['w_glu'][j], wt['b_glu'][j], wt['w_out_cd'][j])
            pools.append(pb)
            s5res.append(sre)
            s5ims.append(sim)
        h = channel_mixer(h, wt['norm_ffn'][i], wt['w_ffn_up'][i], wt['w_ffn_down'][i])
        h = per_layer_embed(h, p[i], wt['norm_pe'][i], wt['w_pe_proj'][i], wt['w_pe_gate'][i])
    y = rmsnorm(h, wt['norm_final'])
    return (y, jnp.stack(convs), jnp.stack(deltas), jnp.stack(vrows),
            jnp.stack(pools), jnp.stack(s5res), jnp.stack(s5ims))


def setup_inputs(seed: int = 0) -> dict:
    key = jax.random.key(seed)
    ks = iter(jax.random.split(key, 64))
    f32 = jnp.float32

    def nrm(shape, s=1.0):
        return jax.random.normal(next(ks), shape, f32) * s

    def gain(shape):
        return 1.0 + nrm(shape, 0.02)

    def unif(shape, lo, hi):
        return jax.random.uniform(next(ks), shape, f32, minval=lo, maxval=hi)

    dt_a = unif((N_AB, GDN_HEADS), 1e-3, 1e-1)
    inp = {
        'x_prompt': nrm((BATCH, SEQ, D_MODEL)),
        'x_sample': nrm((DEC_BATCH, DEC_SEQ, D_MODEL)),
        'state_conv': nrm((N_AB, DEC_BATCH, GDN_CONV - 1, D_QKV)),
        'state_delta': nrm((N_AB, DEC_BATCH, GDN_HEADS, GDN_DK, GDN_DV), GDN_DK ** -0.5),
        'state_pool': nrm((N_CD, DEC_BATCH, POOL_BUF, D_C)),
        'state_s5_re': nrm((N_CD, DEC_BATCH, S5_GROUPS, S5_STATE), 0.1),
        'state_s5_im': nrm((N_CD, DEC_BATCH, S5_GROUPS, S5_STATE), 0.1),
        'p_prompt': nrm((DEPTH, BATCH, SEQ, D_PLE)),
        'p_sample': nrm((DEPTH, DEC_BATCH, DEC_SEQ, D_PLE)),
        'norm_mix': gain((DEPTH, D_MODEL)),
        'norm_ffn': gain((DEPTH, D_MODEL)),
        'norm_pe': gain((DEPTH, D_MODEL)),
        'norm_final': gain((D_MODEL,)),
        'w_in_ab': nrm((N_AB, D_MODEL, D_IN_AB), D_MODEL ** -0.5),
        'conv_qkv': nrm((N_AB, GDN_CONV, D_QKV), GDN_CONV ** -0.5),
        'a_log': jnp.log(unif((N_AB, GDN_HEADS), 1.0, 16.0)),
        'dt_bias': jnp.log(jnp.expm1(dt_a)),
        'norm_o': gain((N_AB, GDN_DV)),
        'ln_v_gain': gain((N_AB, D_B)),
        'ln_v_bias': nrm((N_AB, D_B), 0.02),
        'w_spatial': nrm((N_AB, SGU_GROUPS, SGU_CHUNK, SGU_CHUNK), SGU_CHUNK ** -0.5),
        'b_spatial': 1.0 + nrm((N_AB, SGU_GROUPS, SGU_CHUNK), 0.1),
        'w_out_ab': nrm((N_AB, D_A + D_B, D_MODEL), (D_A + D_B) ** -0.5),
        'w_in_cd': nrm((N_CD, D_MODEL, D_IN_CD), D_MODEL ** -0.5),
        'w_pool': nrm((N_CD, POOL_GROUPS, POOL_GW, POOL_GW), POOL_GW ** -0.5),
        'pool_scale': gain((N_CD, D_C)),
        'lam_re': -0.5 + nrm((N_CD, S5_GROUPS, S5_STATE), 0.01),
        'lam_im': math.pi * jnp.arange(S5_STATE, dtype=f32) + nrm((N_CD, S5_GROUPS, S5_STATE), 0.01),
        'log_dt': unif((N_CD, S5_GROUPS), math.log(1e-3), math.log(1e-1)),
        'b_re': nrm((N_CD, S5_GROUPS, S5_STATE, S5_GW), S5_GW ** -0.5),
        'b_im': nrm((N_CD, S5_GROUPS, S5_STATE, S5_GW), S5_GW ** -0.5),
        'c_re': nrm((N_CD, S5_GROUPS, S5_GW, S5_STATE), S5_STATE ** -0.5),
        'c_im': nrm((N_CD, S5_GROUPS, S5_GW, S5_STATE), S5_STATE ** -0.5),
        'd_skip': nrm((N_CD, D_D)),
        'w_glu': nrm((N_CD, D_D, D_D), D_D ** -0.5),
        'b_glu': nrm((N_CD, D_D), 0.01),
        'w_out_cd': nrm((N_CD, D_C + D_D, D_MODEL), (D_C + D_D) ** -0.5),
        'w_ffn_up': nrm((DEPTH, D_MODEL, D_FF), D_MODEL ** -0.5),
        'w_ffn_down': nrm((DEPTH, D_FF, D_MODEL), D_FF ** -0.5),
        'w_pe_proj': nrm((DEPTH, D_PLE, D_MODEL), D_PLE ** -0.5),
        'w_pe_gate': nrm((DEPTH, D_MODEL, D_MODEL), D_MODEL ** -0.5),
    }
    return inp


def reference(x_prompt, x_sample, state_conv, state_delta, state_pool, state_s5_re, state_s5_im,
              p_prompt, p_sample, norm_mix, norm_ffn, norm_pe, norm_final, w_in_ab, conv_qkv,
              a_log, dt_bias, norm_o, ln_v_gain, ln_v_bias, w_spatial, b_spatial, w_out_ab,
              w_in_cd, w_pool, pool_scale, lam_re, lam_im, log_dt, b_re, b_im, c_re, c_im,
              d_skip, w_glu, b_glu, w_out_cd, w_ffn_up, w_ffn_down, w_pe_proj, w_pe_gate):
    wt = {'norm_mix': norm_mix, 'norm_ffn': norm_ffn, 'norm_pe': norm_pe, 'norm_final': norm_final,
          'w_in_ab': w_in_ab, 'conv_qkv': conv_qkv, 'a_log': a_log, 'dt_bias': dt_bias,
          'norm_o': norm_o, 'ln_v_gain': ln_v_gain, 'ln_v_bias': ln_v_bias, 'w_spatial': w_spatial,
          'b_spatial': b_spatial, 'w_out_ab': w_out_ab, 'w_in_cd': w_in_cd, 'w_pool': w_pool,
          'pool_scale': pool_scale, 'lam_re': lam_re, 'lam_im': lam_im, 'log_dt': log_dt,
          'b_re': b_re, 'b_im': b_im, 'c_re': c_re, 'c_im': c_im, 'd_skip': d_skip,
          'w_glu': w_glu, 'b_glu': b_glu, 'w_out_cd': w_out_cd, 'w_ffn_up': w_ffn_up,
          'w_ffn_down': w_ffn_down, 'w_pe_proj': w_pe_proj, 'w_pe_gate': w_pe_gate}
    bp = x_prompt.shape[0]
    z_conv = jnp.zeros((N_AB, bp) + state_conv.shape[2:], x_prompt.dtype)
    z_delta = jnp.zeros((N_AB, bp) + state_delta.shape[2:], jnp.float32)
    z_pool = jnp.zeros((N_CD, bp) + state_pool.shape[2:], x_prompt.dtype)
    z_s5 = jnp.zeros((N_CD, bp) + state_s5_re.shape[2:], jnp.float32)
    y_prompt, conv_p, delta_p, sgu_v_p, pool_p, s5re_p, s5im_p = trunk(
        x_prompt, p_prompt, z_conv, z_delta, z_pool, z_s5, z_s5, 0, wt)
    y_sample, conv_s, delta_s, sgu_v_s, pool_s, s5re_s, s5im_s = trunk(
        x_sample, p_sample, state_conv, state_delta, state_pool, state_s5_re, state_s5_im, PAST_LEN, wt)
    return (y_prompt, y_sample, conv_p, delta_p, sgu_v_p, pool_p, s5re_p, s5im_p,
            conv_s, delta_s, sgu_v_s, pool_s, s5re_s, s5im_s)
```

```python
import functools
import math

import jax
import jax.numpy as jnp
from jax import lax
from jax.experimental import pallas as pl
from jax.experimental.pallas import tpu as pltpu

F32 = jnp.float32
BF16 = jnp.bfloat16
HIGHEST = lax.Precision.HIGHEST

EPS = 1e-6
D_MODEL = 1024
PAST_LEN = 16384
GDN_HEADS = 4
GDN_DK = 128
D_A = 512
D_QKV = 1536
D_B = 512
SGU_GROUPS = 4
SGU_CHUNK = 128
POOL_WINDOWS = (2, 4, 8, 16)
POOL_BUF = 15
D_C = 512
D_D = 512
S5_GROUPS = 32
S5_STATE = 64
S5_GW = 16
D_FF = 4096
D_PLE = 256
S5_MODES = S5_GROUPS * S5_STATE
S5_HALF = S5_MODES // 2
LANES = 128
SUBLANES = 8
VMEM_LIMIT_BYTES = 56 * 1024 * 1024


def _dot(a, b):
    return jnp.dot(a, b, preferred_element_type=F32)


def _dot_nt(a, b):
    return lax.dot_general(a, b, (((1,), (1,)), ((), ())), preferred_element_type=F32)


def _dot_exact(a, b):
    return jnp.dot(a, b, precision=HIGHEST, preferred_element_type=F32)


def _rms(x, g):
    return x * lax.rsqrt(jnp.mean(x * x, axis=-1, keepdims=True) + EPS) * g


def _sigmoid(x):
    return 1.0 / (1.0 + jnp.exp(-x))


def _silu(x):
    return x * _sigmoid(x)


def _gelu_tanh(x):
    return 0.5 * x * (1.0 + jnp.tanh(math.sqrt(2.0 / math.pi) * (x + 0.044715 * (x * x * x))))


def _softplus(x):
    return jnp.maximum(x, 0.0) + jnp.log(1.0 + jnp.exp(-jnp.abs(x)))


def _log2(n):
    l = n.bit_length() - 1
    assert (1 << l) == n, n
    return l


def _ab_body(h_ref, conv0_ref, delta0_ref, gmix_ref, win_ref, wba_ref, convw_ref, gpar_ref,
             normo_ref, lng_ref, lnb_ref, wsp_ref, bsp_ref, wout_ref,
             hout_ref, conv_ref, delta_ref, vrows_ref, *, nseq, tq, c, cs):
    R = nseq * tq
    t_idx = pl.program_id(1)

    @pl.when(t_idx == 0)
    def _init():
        conv_ref[...] = conv0_ref[...]
        delta_ref[...] = delta0_ref[...]

    x = h_ref[...].reshape(R, D_MODEL)
    xn = _rms(x, gmix_ref[...]).astype(BF16)
    z = _dot(xn, win_ref[...])
    zba = _dot(xn, wba_ref[...])

    pre = z[:, :D_QKV].reshape(nseq, tq, D_QKV)
    xext = jnp.concatenate([conv_ref[...], pre], axis=1)
    cw = convw_ref[...]
    acc = pre * cw[3:4, :]
    for k in range(1, 4):
        acc = acc + pltpu.roll(xext, k, axis=1)[:, SUBLANES:, :] * cw[3 - k:4 - k, :]
    conv_ref[...] = xext[:, tq:, :]
    qkv = _silu(acc).reshape(R, D_QKV)

    gpar = gpar_ref[...]
    beta_all = _sigmoid(zba)
    g_all = -jnp.exp(gpar[0:1, :]) * _softplus(zba + gpar[1:2, :])

    log2c = _log2(c)
    row = lax.broadcasted_iota(jnp.int32, (R, R), 0)
    col = lax.broadcasted_iota(jnp.int32, (R, R), 1)
    rblk = lax.shift_right_logical(row, log2c)
    cblk = lax.shift_right_logical(col, log2c)
    same = rblk == cblk
    tril = same & (row >= col)
    strict = same & (row > col)
    eye = (row == col).astype(F32)
    gc_all = _dot_exact(tril.astype(F32), g_all)
    grem_all = _dot_exact((same & (col > row)).astype(F32), g_all)
    gc_t = gc_all.T
    cblk_row = cblk[0:1, :]

    nblk = R // c
    blk_per_seq = tq // c
    o_heads = []
    for h in range(GDN_HEADS):
        lo = h * GDN_DK
        q = qkv[:, lo:lo + GDN_DK]
        k = qkv[:, D_A + lo:D_A + lo + GDN_DK]
        v = qkv[:, 2 * D_A + lo:2 * D_A + lo + GDN_DK]
        q = q * lax.rsqrt(jnp.sum(q * q, axis=-1, keepdims=True) + EPS) * (GDN_DK ** -0.5)
        k = k * lax.rsqrt(jnp.sum(k * k, axis=-1, keepdims=True) + EPS)
        beta = beta_all[:, h:h + 1]
        gc = gc_all[:, 4 + h:5 + h]
        gcr = gc_t[4 + h:5 + h, :]
        decay = jnp.where(tril, jnp.exp(jnp.where(tril, gc - gcr, 0.0)), 0.0)
        kb = k.astype(BF16)
        kk = _dot_nt(kb, kb)
        lmat = jnp.where(strict, beta * kk * decay, 0.0)
        tinv = eye - lmat
        lpow = lmat
        for _ in range(log2c - 1):
            lb = lpow.astype(BF16)
            lpow = _dot(lb, lb)
            tinv = tinv + _dot(tinv.astype(BF16), lpow.astype(BF16))
        eg = jnp.exp(gc)
        rhs = jnp.concatenate([(v * beta).astype(BF16), (k * (beta * eg)).astype(BF16)], axis=1)
        uw = _dot(tinv.astype(BF16), rhs)
        u_base = uw[:, :GDN_DK]
        w_dec = uw[:, GDN_DK:]
        qk = (_dot_nt(q.astype(BF16), kb) * decay).astype(BF16)
        q_head = q * eg
        k_tail_t = (k * jnp.exp(grem_all[:, 4 + h:5 + h])).T
        u_list, qs_list = [], []
        for j in range(nblk):
            sl = slice(j * c, (j + 1) * c)
            seq = j // blk_per_seq
            s_old = delta_ref[seq, h]
            both = jnp.concatenate([w_dec[sl], q_head[sl]], axis=0).astype(BF16)
            tmp = _dot(both, s_old.astype(BF16))
            u_j = u_base[sl] - tmp[:c]
            u_list.append(u_j)
            qs_list.append(tmp[c:])
            pieces = u_list + ([u_base[(j + 1) * c:]] if j + 1 < nblk else [])
            u_full = pieces[0] if len(pieces) == 1 else jnp.concatenate(pieces, axis=0)
            kt_j = jnp.where(cblk_row == j, k_tail_t, 0.0).astype(BF16)
            g_last = gc_all[(j + 1) * c - 1:(j + 1) * c, 4 + h:5 + h]
            delta_ref[seq, h] = s_old * jnp.exp(g_last) + _dot(kt_j, u_full.astype(BF16))
        u_all = u_list[0] if nblk == 1 else jnp.concatenate(u_list, axis=0)
        qs_all = qs_list[0] if nblk == 1 else jnp.concatenate(qs_list, axis=0)
        o = qs_all + _dot(qk, u_all.astype(BF16))
        o = _rms(o, normo_ref[...]) * _silu(z[:, D_QKV + lo:D_QKV + lo + GDN_DK])
        o_heads.append(o)

    zu = z[:, D_QKV + D_A:D_QKV + D_A + D_B]
    zv = z[:, D_QKV + D_A + D_B:]
    u_act = _gelu_tanh(zu)
    gv = _gelu_tanh(zv)
    vc = gv - jnp.mean(gv, axis=-1, keepdims=True)
    var = jnp.mean(vc * vc, axis=-1, keepdims=True)
    v_ln = vc * lax.rsqrt(var + EPS) * lng_ref[...] + lnb_ref[...]
    rv = vrows_ref.shape[1]
    vrows_ref[...] = v_ln.reshape(nseq, tq, D_B)[:, tq - rv:, :]
    log2cs = _log2(cs)
    srow = lax.broadcasted_iota(jnp.int32, (SGU_CHUNK, SGU_CHUNK), 0)
    scol = lax.broadcasted_iota(jnp.int32, (SGU_CHUNK, SGU_CHUNK), 1)
    smask = (lax.shift_right_logical(srow, log2cs) == lax.shift_right_logical(scol, log2cs)) & (srow >= scol)
    v_bf = v_ln.astype(BF16)
    mixed_groups = []
    for g in range(SGU_GROUPS):
        wg = jnp.where(smask, wsp_ref[g], 0.0).astype(BF16)
        bg = bsp_ref[g]
        parts = []
        for r0 in range(0, R, SGU_CHUNK):
            parts.append(_dot(wg, v_bf[r0:r0 + SGU_CHUNK, g * LANES:(g + 1) * LANES]) + bg)
        mixed_groups.append(parts[0] if len(parts) == 1 else jnp.concatenate(parts, axis=0))
    o_b = u_act * jnp.concatenate(mixed_groups, axis=1)

    y_in = jnp.concatenate(o_heads + [o_b], axis=1).astype(BF16)
    out = x + _dot(y_in, wout_ref[...])
    hout_ref[...] = out.reshape(nseq, tq, D_MODEL)


def _ab_layer(h, conv0, delta0, w, *, nseq, tq, c, cs):
    nb, t, _ = h.shape
    grid = (nb // nseq, t // tq)
    rv = min(SGU_CHUNK, tq)
    seq_blk = lambda shape: pl.BlockSpec(shape, lambda b, s: (b,) + (0,) * (len(shape) - 1))
    full = lambda a: pl.BlockSpec(a.shape, lambda b, s: (0,) * a.ndim)
    params = [w['gmix0'], w['win_ab'], w['wba'], w['convw'], w['gpar'], w['normo'], w['lng'], w['lnb'],
              w['wsp'], w['bsp'], w['wout_ab']]
    out_shapes = (
        jax.ShapeDtypeStruct(h.shape, F32),
        jax.ShapeDtypeStruct((nb, SUBLANES, D_QKV), F32),
        jax.ShapeDtypeStruct((nb, GDN_HEADS, GDN_DK, GDN_DK), F32),
        jax.ShapeDtypeStruct((nb, rv, D_B), F32),
    )
    return pl.pallas_call(
        functools.partial(_ab_body, nseq=nseq, tq=tq, c=c, cs=cs),
        grid=grid,
        in_specs=[pl.BlockSpec((nseq, tq, D_MODEL), lambda b, s: (b, s, 0)),
                  seq_blk((nseq, SUBLANES, D_QKV)),
                  seq_blk((nseq, GDN_HEADS, GDN_DK, GDN_DK))] + [full(a) for a in params],
        out_specs=(pl.BlockSpec((nseq, tq, D_MODEL), lambda b, s: (b, s, 0)),
                   seq_blk((nseq, SUBLANES, D_QKV)),
                   seq_blk((nseq, GDN_HEADS, GDN_DK, GDN_DK)),
                   seq_blk((nseq, rv, D_B))),
        out_shape=out_shapes,
        compiler_params=pltpu.CompilerParams(
            dimension_semantics=("arbitrary", "arbitrary"), vmem_limit_bytes=VMEM_LIMIT_BYTES),
        name="ab_mixer",
    )(h, conv0, delta0, *params)


def _cd_body(h_ref, pool0_ref, s50_ref, gmix_ref, win_ref, wpool_ref, pscale_ref, wb_ref, pw_re_ref,
             pw_im_ref, wcre_ref, wcim_ref, dskip_ref, wglu_ref, bglu_ref, wout_ref,
             hout_ref, pool_ref, s5_ref, bu_ref, *, nseq, tq, pos0, chain):
    R = nseq * tq
    t_idx = pl.program_id(1)

    @pl.when(t_idx == 0)
    def _init():
        pool_ref[...] = pool0_ref[...]
        s5_ref[...] = s50_ref[...]

    x = h_ref[...].reshape(R, D_MODEL)
    xn = _rms(x, gmix_ref[...]).astype(BF16)
    z = _dot(xn, win_ref[...])
    xc = z[:, :D_C].reshape(nseq, tq, D_C)
    xd = z[:, D_C:]

    pbuf = 2 * SUBLANES
    xext = jnp.concatenate([pool_ref[...], xc], axis=1)
    pool_ref[...] = xext[:, tq:, :]
    pos = pos0 + t_idx * tq + lax.broadcasted_iota(jnp.int32, (1, tq, 1), 1)
    sums = xext
    outs = []
    for gi, win in enumerate(POOL_WINDOWS):
        sums = sums + pltpu.roll(sums, win // 2, axis=1)
        cnt = jnp.minimum(win, pos + 1).astype(F32)
        sl = slice(gi * LANES, (gi + 1) * LANES)
        m = sums[:, pbuf:, sl] / cnt - xc[:, :, sl]
        outs.append(_dot(m.reshape(R, LANES).astype(BF16), wpool_ref[gi]))
    y_c = jnp.concatenate(outs, axis=1) * pscale_ref[...]

    xd_bf = xd.astype(BF16)
    for half in range(2):
        bu_ref[:, half * 2 * S5_HALF:(half + 1) * 2 * S5_HALF] = _dot(
            xd_bf[:, half * 256:(half + 1) * 256], wb_ref[half])
    rowid = lax.broadcasted_iota(jnp.int32, (SUBLANES, S5_MODES), 0)
    pw_re = pw_re_ref[...]
    pw_im = pw_im_ref[...]
    step_mul = []
    for k in (1, 2, 4):
        step_mul.append((jnp.where(rowid >= k, pw_re[k - 1:k, :], 0.0),
                         jnp.where(rowid >= k, pw_im[k - 1:k, :], 0.0)))
    car_re = pw_re[0:SUBLANES, :]
    car_im = pw_im[0:SUBLANES, :]

    def group(i, carry):
        r0 = pl.multiple_of(i * SUBLANES, SUBLANES)
        blk = bu_ref[pl.ds(r0, SUBLANES), :]
        if chain:
            cin = carry
        else:
            cin = s5_ref[i]
        new_rows = []
        for half in range(2):
            o = half * 2 * S5_HALF
            ms = slice(half * S5_HALF, (half + 1) * S5_HALF)
            re = blk[:, o:o + S5_HALF]
            im = blk[:, o + S5_HALF:o + 2 * S5_HALF]
            for (mr, mi), k in zip(step_mul, (1, 2, 4)):
                sre = pltpu.roll(re, k, axis=0)
                sim = pltpu.roll(im, k, axis=0)
                re, im = (re + mr[:, ms] * sre - mi[:, ms] * sim,
                          im + mr[:, ms] * sim + mi[:, ms] * sre)
            cre = cin[:, o:o + S5_HALF]
            cim = cin[:, o + S5_HALF:o + 2 * S5_HALF]
            re, im = (re + car_re[:, ms] * cre - car_im[:, ms] * cim,
                      im + car_re[:, ms] * cim + car_im[:, ms] * cre)
            bu_ref[pl.ds(r0, SUBLANES), o:o + S5_HALF] = re
            bu_ref[pl.ds(r0, SUBLANES), o + S5_HALF:o + 2 * S5_HALF] = im
            new_rows += [re[SUBLANES - 1:, :], im[SUBLANES - 1:, :]]
        last = jnp.concatenate(new_rows, axis=1)
        if chain:
            return last
        s5_ref[i] = last
        return carry

    ngroups = R // SUBLANES
    if chain:
        s5_ref[0] = lax.fori_loop(0, ngroups, group, s5_ref[0])
    else:
        lax.fori_loop(0, ngroups, group, jnp.zeros((1, 2 * S5_MODES), F32))

    ys = []
    for half in range(2):
        o = half * 2 * S5_HALF
        s_re = bu_ref[:, o:o + S5_HALF].astype(BF16)
        s_im = bu_ref[:, o + S5_HALF:o + 2 * S5_HALF].astype(BF16)
        ys.append(_dot(s_re, wcre_ref[half]) - _dot(s_im, wcim_ref[half]))
    y = jnp.concatenate(ys, axis=1) + dskip_ref[...] * xd
    y = _gelu_tanh(y)
    y_d = y * _sigmoid(_dot(y.astype(BF16), wglu_ref[...]) + bglu_ref[...])

    y_in = jnp.concatenate([y_c, y_d], axis=1).astype(BF16)
    out = x + _dot(y_in, wout_ref[...])
    hout_ref[...] = out.reshape(nseq, tq, D_MODEL)


def _cd_layer(h, pool0, s50, w, *, nseq, tq, pos0, chain):
    nb, t, _ = h.shape
    grid = (nb // nseq, t // tq)
    seq_blk = lambda shape: pl.BlockSpec(shape, lambda b, s: (b,) + (0,) * (len(shape) - 1))
    full = lambda a: pl.BlockSpec(a.shape, lambda b, s: (0,) * a.ndim)
    params = [w['gmix1'], w['win_cd'], w['wpool'], w['pscale'], w['wb'], w['pw_re'], w['pw_im'],
              w['wcre'], w['wcim'], w['dskip'], w['wglu'], w['bglu'], w['wout_cd']]
    out_shapes = (
        jax.ShapeDtypeStruct(h.shape, F32),
        jax.ShapeDtypeStruct((nb, 2 * SUBLANES, D_C), F32),
        jax.ShapeDtypeStruct((nb, 1, 2 * S5_MODES), F32),
    )
    return pl.pallas_call(
        functools.partial(_cd_body, nseq=nseq, tq=tq, pos0=pos0, chain=chain),
        grid=grid,
        in_specs=[pl.BlockSpec((nseq, tq, D_MODEL), lambda b, s: (b, s, 0)),
                  seq_blk((nseq, 2 * SUBLANES, D_C)),
                  seq_blk((nseq, 1, 2 * S5_MODES))] + [full(a) for a in params],
        out_specs=(pl.BlockSpec((nseq, tq, D_MODEL), lambda b, s: (b, s, 0)),
                   seq_blk((nseq, 2 * SUBLANES, D_C)),
                   seq_blk((nseq, 1, 2 * S5_MODES))),
        out_shape=out_shapes,
        scratch_shapes=[pltpu.VMEM((nseq * tq, 2 * S5_MODES), F32)],
        compiler_params=pltpu.CompilerParams(
            dimension_semantics=("arbitrary", "arbitrary"), vmem_limit_bytes=VMEM_LIMIT_BYTES),
        name="cd_mixer",
    )(h, pool0, s50, *params)


def _ffn_pe_body(h_ref, p_ref, gffn_ref, wup_ref, wdn_ref, gpe_ref, wgate_ref, wproj_ref, gfin_ref,
                 out_ref, *, final):
    x = h_ref[...]
    xn = _rms(x, gffn_ref[...]).astype(BF16)
    h = x
    ffc = D_FF // 4
    for ci in range(4):
        a = jnp.maximum(_dot(xn, wup_ref[:, ci * ffc:(ci + 1) * ffc]), 0.0)
        h = h + _dot((a * a).astype(BF16), wdn_ref[ci * ffc:(ci + 1) * ffc, :])
    hn = _rms(h, gpe_ref[...]).astype(BF16)
    gate = _sigmoid(_dot(hn, wgate_ref[...]))
    h = h + _dot(p_ref[...].astype(BF16), wproj_ref[...]) * gate
    if final:
        h = _rms(h, gfin_ref[...])
    out_ref[...] = h


def _ffn_pe(h2, p2, w, layer, *, tm, final):
    n = h2.shape[0]
    full = lambda a: pl.BlockSpec(a.shape, lambda i: (0,) * a.ndim)
    params = [w['gffn'][layer], w['wup'][layer], w['wdn'][layer], w['gpe'][layer], w['wgate'][layer],
              w['wproj'][layer], w['gfin']]
    return pl.pallas_call(
        functools.partial(_ffn_pe_body, final=final),
        grid=(n // tm,),
        in_specs=[pl.BlockSpec((tm, D_MODEL), lambda i: (i, 0)),
                  pl.BlockSpec((tm, D_PLE), lambda i: (i, 0))] + [full(a) for a in params],
        out_specs=pl.BlockSpec((tm, D_MODEL), lambda i: (i, 0)),
        out_shape=jax.ShapeDtypeStruct(h2.shape, F32),
        compiler_params=pltpu.CompilerParams(
            dimension_semantics=("arbitrary",), vmem_limit_bytes=VMEM_LIMIT_BYTES),
        name="ffn_pe",
    )(h2, p2, *params)


def _s5_prep_body(lre_ref, lim_ref, ldt_ref, bre_ref, bim_ref, pwre_ref, pwim_ref, bbre_ref, bbim_ref):
    lr = lre_ref[...]
    li = lim_ref[...]
    dt = jnp.exp(ldt_ref[...])
    k = (lax.broadcasted_iota(jnp.int32, (S5_MODES, 2 * SUBLANES), 1) + 1).astype(F32)
    mag = jnp.exp(k * (lr * dt))
    ang = k * (li * dt)
    pw_re = mag * jnp.cos(ang)
    pw_im = mag * jnp.sin(ang)
    pwre_ref[...] = pw_re
    pwim_ref[...] = pw_im
    a = pw_re[:, 0:1] - 1.0
    b = pw_im[:, 0:1]
    den = lr * lr + li * li
    cr = (a * lr + b * li) / den
    ci = (b * lr - a * li) / den
    bre = bre_ref[...]
    bim = bim_ref[...]
    bbre_ref[...] = cr * bre - ci * bim
    bbim_ref[...] = cr * bim + ci * bre


def _s5_prep(lam_re, lam_im, log_dt, b_re, b_im):
    col = lambda a: a.reshape(S5_MODES, 1)
    ldt = jnp.broadcast_to(log_dt[:, None], (S5_GROUPS, S5_STATE))
    args = [col(lam_re), col(lam_im), col(ldt), b_re.reshape(S5_MODES, S5_GW), b_im.reshape(S5_MODES, S5_GW)]
    shp = jax.ShapeDtypeStruct((S5_MODES, S5_GW), F32)
    return pl.pallas_call(_s5_prep_body, out_shape=(shp, shp, shp, shp), name="s5_prep")(*args)


def _block_diag_halves(a):
    p, q = a.shape[1], a.shape[2]
    eye = jnp.eye(S5_GROUPS // 2, dtype=a.dtype)
    a = a.reshape(2, S5_GROUPS // 2, p, q)
    return jnp.einsum('hgpq,gk->hgpkq', a, eye).reshape(2, S5_GROUPS // 2 * p, S5_GROUPS // 2 * q)


def _mode_lanes(a):
    return a.reshape(a.shape[:-2] + (2, S5_HALF))


def _prepare_weights(norm_mix, norm_ffn, norm_pe, norm_final, w_in_ab, conv_qkv, a_log, dt_bias, norm_o,
                     ln_v_gain, ln_v_bias, w_spatial, b_spatial, w_out_ab, w_in_cd, w_pool, pool_scale,
                     lam_re, lam_im, log_dt, b_re, b_im, c_re, c_im, d_skip, w_glu, b_glu, w_out_cd,
                     w_ffn_up, w_ffn_down, w_pe_proj, w_pe_gate):
    row = lambda a: a.reshape(1, -1)
    w = {}
    w['gmix0'] = row(norm_mix[0])
    w['gmix1'] = row(norm_mix[1])
    wi = w_in_ab[0]
    cut0 = D_QKV + D_A
    cut1 = cut0 + 2 * GDN_HEADS
    w['win_ab'] = jnp.concatenate([wi[:, :cut0], wi[:, cut1:]], axis=1).astype(BF16)
    w['wba'] = jnp.pad(wi[:, cut0:cut1], ((0, 0), (0, LANES - 2 * GDN_HEADS))).astype(BF16)
    w['convw'] = jnp.pad(conv_qkv[0], ((0, SUBLANES - 4), (0, 0)))
    gp = jnp.zeros((SUBLANES, LANES), F32)
    gp = gp.at[0, GDN_HEADS:2 * GDN_HEADS].set(a_log[0]).at[1, GDN_HEADS:2 * GDN_HEADS].set(dt_bias[0])
    w['gpar'] = gp
    w['normo'] = row(norm_o[0])
    w['lng'] = row(ln_v_gain[0])
    w['lnb'] = row(ln_v_bias[0])
    w['wout_ab'] = w_out_ab[0].astype(BF16)
    w['w_spatial'] = w_spatial[0]
    w['b_spatial'] = b_spatial[0]
    w['win_cd'] = w_in_cd[0].astype(BF16)
    w['wpool'] = w_pool[0].astype(BF16)
    w['pscale'] = row(pool_scale[0])
    pw_re, pw_im, bb_re, bb_im = _s5_prep(lam_re[0], lam_im[0], log_dt[0], b_re[0], b_im[0])
    w['pw_re'] = pw_re.T
    w['pw_im'] = pw_im.T
    to_gcn = lambda a: jnp.swapaxes(a.reshape(S5_GROUPS, S5_STATE, S5_GW), 1, 2)
    w['wb'] = jnp.concatenate([_block_diag_halves(to_gcn(bb_re)), _block_diag_halves(to_gcn(bb_im))],
                              axis=2).astype(BF16)
    w['wcre'] = _block_diag_halves(jnp.swapaxes(c_re[0], 1, 2)).astype(BF16)
    w['wcim'] = _block_diag_halves(jnp.swapaxes(c_im[0], 1, 2)).astype(BF16)
    w['dskip'] = row(d_skip[0])
    w['wglu'] = w_glu[0].astype(BF16)
    w['bglu'] = row(b_glu[0])
    w['wout_cd'] = w_out_cd[0].astype(BF16)
    w['gffn'] = norm_ffn.reshape(2, 1, D_MODEL)
    w['gpe'] = norm_pe.reshape(2, 1, D_MODEL)
    w['gfin'] = row(norm_final)
    w['wup'] = w_ffn_up.astype(BF16)
    w['wdn'] = w_ffn_down.astype(BF16)
    w['wgate'] = w_pe_gate.astype(BF16)
    w['wproj'] = w_pe_proj.astype(BF16)
    return w


def _trunk(x, p, conv0, delta0, pool0, s5re0, s5im0, pos0, w, *, nseq, tq, tm):
    nb, t, _ = x.shape
    c = min(64, tq)
    cs = min(SGU_CHUNK, tq)
    reps = SGU_CHUNK // cs
    wl = dict(w)
    wl['wsp'] = jnp.tile(w['w_spatial'][:, :cs, :cs], (1, reps, reps))
    wl['bsp'] = jnp.tile(w['b_spatial'][:, :cs], (1, reps))[:, :, None]
    conv_in = jnp.pad(conv0, ((0, 0), (SUBLANES - 3, 0), (0, 0)))
    h, conv_o, delta_o, vrows = _ab_layer(x, conv_in, delta0, wl, nseq=nseq, tq=tq, c=c, cs=cs)
    h = _ffn_pe(h.reshape(nb * t, D_MODEL), p[0].reshape(nb * t, D_PLE), w, 0, tm=tm, final=False)
    pool_in = jnp.pad(pool0, ((0, 0), (2 * SUBLANES - POOL_BUF, 0), (0, 0)))
    s5_in = jnp.concatenate([_mode_lanes(s5re0), _mode_lanes(s5im0)], axis=-1).reshape(nb, 1, 2 * S5_MODES)
    h, pool_o, s5_o = _cd_layer(h.reshape(nb, t, D_MODEL), pool_in, s5_in, w,
                                nseq=nseq, tq=tq, pos0=pos0, chain=(nseq == 1))
    y = _ffn_pe(h.reshape(nb * t, D_MODEL), p[1].reshape(nb * t, D_PLE), w, 1, tm=tm, final=True)
    s5_o = s5_o.reshape(nb, 2, 2, S5_HALF)
    s5re = s5_o[:, :, 0, :].reshape(nb, S5_GROUPS, S5_STATE)
    s5im = s5_o[:, :, 1, :].reshape(nb, S5_GROUPS, S5_STATE)
    return (y.reshape(nb, t, D_MODEL), conv_o[None, :, SUBLANES - 3:, :], delta_o[None], vrows[None],
            pool_o[None, :, 2 * SUBLANES - POOL_BUF:, :], s5re[None], s5im[None])


def kernel(x_prompt, x_sample, state_conv, state_delta, state_pool, state_s5_re, state_s5_im, p_prompt, p_sample, norm_mix, norm_ffn, norm_pe, norm_final, w_in_ab, conv_qkv, a_log, dt_bias, norm_o, ln_v_gain, ln_v_bias, w_spatial, b_spatial, w_out_ab, w_in_cd, w_pool, pool_scale, lam_re, lam_im, log_dt, b_re, b_im, c_re, c_im, d_skip, w_glu, b_glu, w_out_cd, w_ffn_up, w_ffn_down, w_pe_proj, w_pe_gate):
    w = _prepare_weights(norm_mix, norm_ffn, norm_pe, norm_final, w_in_ab, conv_qkv, a_log, dt_bias, norm_o,
                         ln_v_gain, ln_v_bias, w_spatial, b_spatial, w_out_ab, w_in_cd, w_pool, pool_scale,
                         lam_re, lam_im, log_dt, b_re, b_im, c_re, c_im, d_skip, w_glu, b_glu, w_out_cd,
                         w_ffn_up, w_ffn_down, w_pe_proj, w_pe_gate)
    bp = x_prompt.shape[0]
    zeros = lambda a: jnp.zeros((bp,) + a.shape[2:], F32)
    outs_p = _trunk(x_prompt, p_prompt, zeros(state_conv), zeros(state_delta), zeros(state_pool),
                    zeros(state_s5_re), zeros(state_s5_im), 0, w, nseq=1, tq=256, tm=512)
    outs_s = _trunk(x_sample, p_sample, state_conv[0], state_delta[0], state_pool[0],
                    state_s5_re[0], state_s5_im[0], PAST_LEN, w, nseq=16, tq=8, tm=512)
    return (outs_p[0], outs_s[0]) + outs_p[1:] + outs_s[1:]
```

```python
import functools
import math

import jax
import jax.numpy as jnp
from jax import lax
from jax.experimental import pallas as pl
from jax.experimental.pallas import tpu as pltpu

F32 = jnp.float32
BF16 = jnp.bfloat16
HIGHEST = lax.Precision.HIGHEST

EPS = 1e-6
D_MODEL = 1024
PAST_LEN = 16384
GDN_HEADS = 4
GDN_DK = 128
D_A = 512
D_QKV = 1536
D_B = 512
SGU_GROUPS = 4
SGU_CHUNK = 128
POOL_WINDOWS = (2, 4, 8, 16)
POOL_BUF = 15
D_C = 512
D_D = 512
S5_GROUPS = 32
S5_STATE = 64
S5_GW = 16
D_FF = 4096
D_PLE = 256
S5_MODES = S5_GROUPS * S5_STATE
S5_HALF = S5_MODES // 2
LANES = 128
SUBLANES = 8
GDN_SUB = 128
S5_SCAN_VREGS = 16
VMEM_LIMIT_BYTES = 56 * 1024 * 1024


def _dot(a, b):
    return jnp.dot(a, b, preferred_element_type=F32)


def _dot_nt(a, b):
    return lax.dot_general(a, b, (((1,), (1,)), ((), ())), preferred_element_type=F32)


def _dot_exact(a, b):
    return jnp.dot(a, b, precision=HIGHEST, preferred_element_type=F32)


def _rms(x, g):
    return x * lax.rsqrt(jnp.mean(x * x, axis=-1, keepdims=True) + EPS) * g


def _sigmoid(x):
    return 1.0 / (1.0 + jnp.exp(-x))


def _silu(x):
    return x * _sigmoid(x)


def _gelu_tanh(x):
    return 0.5 * x * (1.0 + jnp.tanh(math.sqrt(2.0 / math.pi) * (x + 0.044715 * (x * x * x))))


def _softplus(x):
    return jnp.maximum(x, 0.0) + jnp.log(1.0 + jnp.exp(-jnp.abs(x)))


def _log2(n):
    l = n.bit_length() - 1
    assert (1 << l) == n, n
    return l


def _ab_body(h_ref, conv0_ref, delta0_ref, gmix_ref, win_ref, wba_ref, convw_ref, gpar_ref,
             normo_ref, lng_ref, lnb_ref, wsp_ref, bsp_ref, wout_ref,
             hout_ref, conv_ref, delta_ref, vrows_ref, *, nseq, tq, c, cs):
    R = nseq * tq
    t_idx = pl.program_id(1)

    @pl.when(t_idx == 0)
    def _init():
        conv_ref[...] = conv0_ref[...]
        delta_ref[...] = delta0_ref[...]

    x = h_ref[...].reshape(R, D_MODEL)
    xn = _rms(x, gmix_ref[...]).astype(BF16)
    z = _dot(xn, win_ref[...])
    zba = _dot(xn, wba_ref[...])

    pre = z[:, :D_QKV].reshape(nseq, tq, D_QKV)
    xext = jnp.concatenate([conv_ref[...], pre], axis=1)
    cw = convw_ref[...]
    acc = pre * cw[3:4, :]
    for k in range(1, 4):
        acc = acc + pltpu.roll(xext, k, axis=1)[:, SUBLANES:, :] * cw[3 - k:4 - k, :]
    conv_ref[...] = xext[:, tq:, :]
    qkv = _silu(acc).reshape(R, D_QKV)

    gpar = gpar_ref[...]
    beta_all = _sigmoid(zba)
    g_all = -jnp.exp(gpar[0:1, :]) * _softplus(zba + gpar[1:2, :])

    sub = GDN_SUB
    log2c = _log2(c)
    row = lax.broadcasted_iota(jnp.int32, (sub, sub), 0)
    col = lax.broadcasted_iota(jnp.int32, (sub, sub), 1)
    cblk = lax.shift_right_logical(col, log2c)
    same = lax.shift_right_logical(row, log2c) == cblk
    tril = same & (row >= col)
    strict = same & (row > col)
    tril_f = tril.astype(F32)
    eye = (row == col).astype(F32)
    cblk_row = cblk[0:1, :]
    nblk = sub // c
    blk_per_seq = tq // c
    nsub = R // sub
    cat = lambda parts, axis: parts[0] if len(parts) == 1 else jnp.concatenate(parts, axis=axis)
    pairs = [(st, h) for st in range(nsub) for h in range(GDN_HEADS)]
    gc_alls, glasts, grems, gc_ts = [], [], [], []
    for st in range(nsub):
        gc_all = _dot_exact(tril_f, g_all[st * sub:(st + 1) * sub])
        glast = [gc_all[(j + 1) * c - 1:(j + 1) * c, :] for j in range(nblk)]
        gc_alls.append(gc_all)
        glasts.append(glast)
        grems.append(cat([jnp.broadcast_to(gl, (c, LANES)) for gl in glast], 0) - gc_all)
        gc_ts.append(gc_all.T)
    qs, ks, vs, betas, egs, decays, qkks = {}, {}, {}, {}, {}, {}, {}
    for p in pairs:
        st, h = p
        r0, lo = st * sub, h * GDN_DK
        q = qkv[r0:r0 + sub, lo:lo + GDN_DK]
        k = qkv[r0:r0 + sub, D_A + lo:D_A + lo + GDN_DK]
        qs[p] = q * lax.rsqrt(jnp.sum(q * q, axis=-1, keepdims=True) + EPS) * (GDN_DK ** -0.5)
        ks[p] = k * lax.rsqrt(jnp.sum(k * k, axis=-1, keepdims=True) + EPS)
        vs[p] = qkv[r0:r0 + sub, 2 * D_A + lo:2 * D_A + lo + GDN_DK]
        betas[p] = beta_all[r0:r0 + sub, h:h + 1]
        gc = gc_alls[st][:, 4 + h:5 + h]
        egs[p] = jnp.exp(gc)
        decays[p] = jnp.where(tril, jnp.exp(jnp.where(tril, gc - gc_ts[st][4 + h:5 + h, :], 0.0)), 0.0)
    for p in pairs:
        kb = ks[p].astype(BF16)
        qkks[p] = _dot_nt(jnp.concatenate([qs[p].astype(BF16), kb], axis=0), kb)
    xks, tinvs, qkds = {}, {}, {}
    for p in pairs:
        qkds[p] = (qkks[p][:sub] * decays[p]).astype(BF16)
        xks[p] = jnp.where(strict, -(betas[p] * qkks[p][sub:] * decays[p]), 0.0)
        tinvs[p] = eye + xks[p]
    if log2c > 1:
        for p in pairs:
            xb = xks[p].astype(BF16)
            xks[p] = _dot(xb, xb)
    for lev in range(1, log2c):
        res = {}
        for p in pairs:
            xb = xks[p].astype(BF16)
            if lev < log2c - 1:
                res[p] = _dot(jnp.concatenate([xb, tinvs[p].astype(BF16)], axis=0), xb)
            else:
                res[p] = _dot(tinvs[p].astype(BF16), xb)
        for p in pairs:
            if lev < log2c - 1:
                xks[p] = res[p][:sub]
                tinvs[p] = tinvs[p] + res[p][sub:]
            else:
                tinvs[p] = tinvs[p] + res[p]
    uws, q_heads, kts = {}, {}, {}
    for p in pairs:
        rhs = jnp.concatenate([(vs[p] * betas[p]).astype(BF16),
                               (ks[p] * (betas[p] * egs[p])).astype(BF16)], axis=1)
        uws[p] = _dot(tinvs[p].astype(BF16), rhs)
        q_heads[p] = qs[p] * egs[p]
        st, h = p
        kts[p] = (ks[p] * jnp.exp(grems[st][:, 4 + h:5 + h])).T
    u_lists = {p: [] for p in pairs}
    qs_lists = {p: [] for p in pairs}
    for st in range(nsub):
        for j in range(nblk):
            sl = slice(j * c, (j + 1) * c)
            seq = (st * nblk + j) // blk_per_seq
            s_olds, tmps = {}, {}
            for h in range(GDN_HEADS):
                p = (st, h)
                s_olds[h] = delta_ref[seq, h]
                both = jnp.concatenate([uws[p][sl, GDN_DK:], q_heads[p][sl]], axis=0).astype(BF16)
                tmps[h] = _dot(both, s_olds[h].astype(BF16))
            for h in range(GDN_HEADS):
                p = (st, h)
                u_base = uws[p][:, :GDN_DK]
                u_lists[p].append(u_base[sl] - tmps[h][:c])
                qs_lists[p].append(tmps[h][c:])
                u_full = cat(u_lists[p] + ([u_base[(j + 1) * c:]] if j + 1 < nblk else []), 0)
                kt_j = jnp.where(cblk_row == j, kts[p], 0.0).astype(BF16)
                delta_ref[seq, h] = (s_olds[h] * jnp.exp(glasts[st][j][:, 4 + h:5 + h])
                                     + _dot(kt_j, u_full.astype(BF16)))
    o_raw = {}
    for p in pairs:
        o_raw[p] = cat(qs_lists[p], 0) + _dot(qkds[p], cat(u_lists[p], 0).astype(BF16))
    o_tiles = []
    for st in range(nsub):
        o_heads = []
        for h in range(GDN_HEADS):
            zg = z[st * sub:(st + 1) * sub, D_QKV + h * GDN_DK:D_QKV + (h + 1) * GDN_DK]
            o_heads.append(_rms(o_raw[(st, h)], normo_ref[...]) * _silu(zg))
        o_tiles.append(jnp.concatenate(o_heads, axis=1))
    o_a = cat(o_tiles, 0)

    zu = z[:, D_QKV + D_A:D_QKV + D_A + D_B]
    zv = z[:, D_QKV + D_A + D_B:]
    u_act = _gelu_tanh(zu)
    gv = _gelu_tanh(zv)
    vc = gv - jnp.mean(gv, axis=-1, keepdims=True)
    var = jnp.mean(vc * vc, axis=-1, keepdims=True)
    v_ln = vc * lax.rsqrt(var + EPS) * lng_ref[...] + lnb_ref[...]
    rv = vrows_ref.shape[1]
    vrows_ref[...] = v_ln.reshape(nseq, tq, D_B)[:, tq - rv:, :]
    log2cs = _log2(cs)
    srow = lax.broadcasted_iota(jnp.int32, (SGU_CHUNK, SGU_CHUNK), 0)
    scol = lax.broadcasted_iota(jnp.int32, (SGU_CHUNK, SGU_CHUNK), 1)
    smask = (lax.shift_right_logical(srow, log2cs) == lax.shift_right_logical(scol, log2cs)) & (srow >= scol)
    v_bf = v_ln.astype(BF16)
    mixed_groups = []
    for g in range(SGU_GROUPS):
        wg = jnp.where(smask, wsp_ref[g], 0.0).astype(BF16)
        bg = bsp_ref[g]
        parts = []
        for r0 in range(0, R, SGU_CHUNK):
            parts.append(_dot(wg, v_bf[r0:r0 + SGU_CHUNK, g * LANES:(g + 1) * LANES]) + bg)
        mixed_groups.append(parts[0] if len(parts) == 1 else jnp.concatenate(parts, axis=0))
    o_b = u_act * jnp.concatenate(mixed_groups, axis=1)

    y_in = jnp.concatenate([o_a, o_b], axis=1).astype(BF16)
    out = x + _dot(y_in, wout_ref[...])
    hout_ref[...] = out.reshape(nseq, tq, D_MODEL)


def _ab_layer(h, conv0, delta0, w, *, nseq, tq, c, cs):
    nb, t, _ = h.shape
    grid = (nb // nseq, t // tq)
    rv = min(SGU_CHUNK, tq)
    seq_blk = lambda shape: pl.BlockSpec(shape, lambda b, s: (b,) + (0,) * (len(shape) - 1))
    full = lambda a: pl.BlockSpec(a.shape, lambda b, s: (0,) * a.ndim)
    params = [w['gmix0'], w['win_ab'], w['wba'], w['convw'], w['gpar'], w['normo'], w['lng'], w['lnb'],
              w['wsp'], w['bsp'], w['wout_ab']]
    out_shapes = (
        jax.ShapeDtypeStruct(h.shape, F32),
        jax.ShapeDtypeStruct((nb, SUBLANES, D_QKV), F32),
        jax.ShapeDtypeStruct((nb, GDN_HEADS, GDN_DK, GDN_DK), F32),
        jax.ShapeDtypeStruct((nb, rv, D_B), F32),
    )
    return pl.pallas_call(
        functools.partial(_ab_body, nseq=nseq, tq=tq, c=c, cs=cs),
        grid=grid,
        in_specs=[pl.BlockSpec((nseq, tq, D_MODEL), lambda b, s: (b, s, 0)),
                  seq_blk((nseq, SUBLANES, D_QKV)),
                  seq_blk((nseq, GDN_HEADS, GDN_DK, GDN_DK))] + [full(a) for a in params],
        out_specs=(pl.BlockSpec((nseq, tq, D_MODEL), lambda b, s: (b, s, 0)),
                   seq_blk((nseq, SUBLANES, D_QKV)),
                   seq_blk((nseq, GDN_HEADS, GDN_DK, GDN_DK)),
                   seq_blk((nseq, rv, D_B))),
        out_shape=out_shapes,
        compiler_params=pltpu.CompilerParams(
            dimension_semantics=("arbitrary", "arbitrary"), vmem_limit_bytes=VMEM_LIMIT_BYTES),
        name="ab_mixer",
    )(h, conv0, delta0, *params)


def _cd_body(h_ref, pool0_ref, s50_ref, gmix_ref, win_ref, wpool_ref, pscale_ref, wb_ref, lam_re_ref,
             lam_im_ref, wcre_ref, wcim_ref, dskip_ref, wglu_ref, bglu_ref, wout_ref,
             hout_ref, pool_ref, s5_ref, bu_ref, *, nseq, tq, pos0):
    R = nseq * tq
    t_idx = pl.program_id(1)

    @pl.when(t_idx == 0)
    def _init():
        pool_ref[...] = pool0_ref[...]
        s5_ref[...] = s50_ref[...]

    x = h_ref[...].reshape(R, D_MODEL)
    xn = _rms(x, gmix_ref[...]).astype(BF16)
    log2n = _log2(nseq)
    ri = lax.broadcasted_iota(jnp.int32, (R, R), 0)
    ci = lax.broadcasted_iota(jnp.int32, (R, R), 1)
    to_tm = (ci == (ri & (nseq - 1)) * tq + lax.shift_right_logical(ri, log2n)).astype(BF16)
    to_sm = (ri == (ci & (nseq - 1)) * tq + lax.shift_right_logical(ci, log2n)).astype(BF16)
    xt = _dot(to_tm, xn).astype(BF16)
    z = _dot(xt, win_ref[...])
    xc = z[:, :D_C].reshape(tq, nseq, D_C)
    xd = z[:, D_C:]

    xext = jnp.concatenate([pool_ref[...], xc], axis=0)
    pool_ref[...] = xext[tq:]
    pos = pos0 + t_idx * tq + lax.broadcasted_iota(jnp.int32, (tq, 1, 1), 0)
    sums = xext
    outs = []
    for gi, win in enumerate(POOL_WINDOWS):
        sums = sums[win // 2:] + sums[:-(win // 2)]
        cnt = jnp.minimum(win, pos + 1).astype(F32)
        sl = slice(gi * LANES, (gi + 1) * LANES)
        m = sums[sums.shape[0] - tq:, :, sl] / cnt - xc[:, :, sl]
        outs.append(_dot(m.reshape(R, LANES).astype(BF16), wpool_ref[gi]))
    y_c = jnp.concatenate(outs, axis=1) * pscale_ref[...]

    xd_bf = xd.astype(BF16)
    for half in range(2):
        bu_ref[:, half * 2 * S5_HALF:(half + 1) * 2 * S5_HALF] = _dot(
            xd_bf[:, half * 256:(half + 1) * 256], wb_ref[half])
    piece = S5_SCAN_VREGS * SUBLANES * LANES // (2 * nseq)
    for half in range(2):
        for p0 in range(0, S5_HALF, piece):
            o_re = half * 2 * S5_HALF + p0
            o_im = o_re + S5_HALF
            ms = slice(half * S5_HALF + p0, half * S5_HALF + p0 + piece)
            lr = jnp.broadcast_to(lam_re_ref[:, ms], (nseq, piece))
            li = jnp.broadcast_to(lam_im_ref[:, ms], (nseq, piece))

            def step(t, carry, o_re=o_re, o_im=o_im, lr=lr, li=li):
                re, im = carry
                r0 = pl.multiple_of(t * nseq, nseq)
                nre = lr * re - li * im + bu_ref[pl.ds(r0, nseq), o_re:o_re + piece]
                nim = lr * im + li * re + bu_ref[pl.ds(r0, nseq), o_im:o_im + piece]
                bu_ref[pl.ds(r0, nseq), o_re:o_re + piece] = nre
                bu_ref[pl.ds(r0, nseq), o_im:o_im + piece] = nim
                return nre, nim

            re, im = lax.fori_loop(0, tq, step, (s5_ref[:, o_re:o_re + piece], s5_ref[:, o_im:o_im + piece]))
            s5_ref[:, o_re:o_re + piece] = re
            s5_ref[:, o_im:o_im + piece] = im

    ys = []
    ngrp = S5_GROUPS // wcre_ref.shape[0]
    for n in range(wcre_ref.shape[0]):
        half, off = divmod(n * ngrp * S5_STATE, S5_HALF)
        o_re = half * 2 * S5_HALF + off
        s_re = bu_ref[:, o_re:o_re + ngrp * S5_STATE].astype(BF16)
        s_im = bu_ref[:, o_re + S5_HALF:o_re + S5_HALF + ngrp * S5_STATE].astype(BF16)
        ys.append(_dot(s_re, wcre_ref[n]) - _dot(s_im, wcim_ref[n]))
    y = jnp.concatenate(ys, axis=1) + dskip_ref[...] * xd
    y = _gelu_tanh(y)
    y_d = y * _sigmoid(_dot(y.astype(BF16), wglu_ref[...]) + bglu_ref[...])

    y_in = jnp.concatenate([y_c, y_d], axis=1).astype(BF16)
    out = x + _dot(_dot(to_sm, y_in).astype(BF16), wout_ref[...])
    hout_ref[...] = out.reshape(nseq, tq, D_MODEL)


def _cd_layer(h, pool0, s50, w, *, nseq, tq, pos0):
    nb, t, _ = h.shape
    grid = (nb // nseq, t // tq)
    full = lambda a: pl.BlockSpec(a.shape, lambda b, s: (0,) * a.ndim)
    params = [w['gmix1'], w['win_cd'], w['wpool'], w['pscale'], w['wb'], w['lam_re'], w['lam_im'],
              w['wcre'], w['wcim'], w['dskip'], w['wglu'], w['bglu'], w['wout_cd']]
    pool_spec = pl.BlockSpec((2 * SUBLANES, nseq, D_C), lambda b, s: (0, b, 0))
    s5_spec = pl.BlockSpec((nseq, 2 * S5_MODES), lambda b, s: (b, 0))
    out_shapes = (
        jax.ShapeDtypeStruct(h.shape, F32),
        jax.ShapeDtypeStruct((2 * SUBLANES, nb, D_C), F32),
        jax.ShapeDtypeStruct((nb, 2 * S5_MODES), F32),
    )
    return pl.pallas_call(
        functools.partial(_cd_body, nseq=nseq, tq=tq, pos0=pos0),
        grid=grid,
        in_specs=[pl.BlockSpec((nseq, tq, D_MODEL), lambda b, s: (b, s, 0)), pool_spec, s5_spec]
        + [full(a) for a in params],
        out_specs=(pl.BlockSpec((nseq, tq, D_MODEL), lambda b, s: (b, s, 0)), pool_spec, s5_spec),
        out_shape=out_shapes,
        scratch_shapes=[pltpu.VMEM((nseq * tq, 2 * S5_MODES), F32)],
        compiler_params=pltpu.CompilerParams(
            dimension_semantics=("arbitrary", "arbitrary"), vmem_limit_bytes=VMEM_LIMIT_BYTES),
        name="cd_mixer",
    )(h, pool0, s50, *params)


def _ffn_pe_body(h_ref, p_ref, gffn_ref, wup_ref, wdn_ref, gpe_ref, wgate_ref, wproj_ref, gfin_ref,
                 out_ref, *, final):
    x = h_ref[...]
    xn = _rms(x, gffn_ref[...]).astype(BF16)
    h = x
    ffc = D_FF // 4
    for ci in range(4):
        a = jnp.maximum(_dot(xn, wup_ref[:, ci * ffc:(ci + 1) * ffc]), 0.0)
        h = h + _dot((a * a).astype(BF16), wdn_ref[ci * ffc:(ci + 1) * ffc, :])
    hn = _rms(h, gpe_ref[...]).astype(BF16)
    gate = _sigmoid(_dot(hn, wgate_ref[...]))
    h = h + _dot(p_ref[...].astype(BF16), wproj_ref[...]) * gate
    if final:
        h = _rms(h, gfin_ref[...])
    out_ref[...] = h


def _ffn_pe(h2, p2, w, layer, *, tm, final):
    n = h2.shape[0]
    full = lambda a: pl.BlockSpec(a.shape, lambda i: (0,) * a.ndim)
    params = [w['gffn'][layer], w['wup'][layer], w['wdn'][layer], w['gpe'][layer], w['wgate'][layer],
              w['wproj'][layer], w['gfin']]
    return pl.pallas_call(
        functools.partial(_ffn_pe_body, final=final),
        grid=(n // tm,),
        in_specs=[pl.BlockSpec((tm, D_MODEL), lambda i: (i, 0)),
                  pl.BlockSpec((tm, D_PLE), lambda i: (i, 0))] + [full(a) for a in params],
        out_specs=pl.BlockSpec((tm, D_MODEL), lambda i: (i, 0)),
        out_shape=jax.ShapeDtypeStruct(h2.shape, F32),
        compiler_params=pltpu.CompilerParams(
            dimension_semantics=("arbitrary",), vmem_limit_bytes=VMEM_LIMIT_BYTES),
        name="ffn_pe",
    )(h2, p2, *params)


def _s5_prep_body(lre_ref, lim_ref, ldt_ref, bre_ref, bim_ref, lbre_ref, lbim_ref, bbre_ref, bbim_ref):
    lr = lre_ref[...]
    li = lim_ref[...]
    dt = jnp.exp(ldt_ref[...])
    mag = jnp.exp(lr * dt)
    ang = li * dt
    lb_re = mag * jnp.cos(ang)
    lb_im = mag * jnp.sin(ang)
    lbre_ref[...] = lb_re
    lbim_ref[...] = lb_im
    a = lb_re - 1.0
    b = lb_im
    den = lr * lr + li * li
    cr = (a * lr + b * li) / den
    ci = (b * lr - a * li) / den
    bre = bre_ref[...]
    bim = bim_ref[...]
    bbre_ref[...] = cr * bre - ci * bim
    bbim_ref[...] = cr * bim + ci * bre


def _s5_prep(lam_re, lam_im, log_dt, b_re, b_im):
    col = lambda a: a.reshape(S5_MODES, 1)
    ldt = jnp.broadcast_to(log_dt[:, None], (S5_GROUPS, S5_STATE))
    args = [col(lam_re), col(lam_im), col(ldt), b_re.reshape(S5_MODES, S5_GW), b_im.reshape(S5_MODES, S5_GW)]
    shp = jax.ShapeDtypeStruct((S5_MODES, S5_GW), F32)
    col_shp = jax.ShapeDtypeStruct((S5_MODES, 1), F32)
    return pl.pallas_call(_s5_prep_body, out_shape=(col_shp, col_shp, shp, shp), name="s5_prep")(*args)


def _block_diag(a, ngrp):
    p, q = a.shape[1], a.shape[2]
    eye = jnp.eye(ngrp, dtype=a.dtype)
    a = a.reshape(S5_GROUPS // ngrp, ngrp, p, q)
    return jnp.einsum('hgpq,gk->hgpkq', a, eye).reshape(S5_GROUPS // ngrp, ngrp * p, ngrp * q)


def _mode_lanes(a):
    return a.reshape(a.shape[:-2] + (2, S5_HALF))


def _prepare_weights(norm_mix, norm_ffn, norm_pe, norm_final, w_in_ab, conv_qkv, a_log, dt_bias, norm_o,
                     ln_v_gain, ln_v_bias, w_spatial, b_spatial, w_out_ab, w_in_cd, w_pool, pool_scale,
                     lam_re, lam_im, log_dt, b_re, b_im, c_re, c_im, d_skip, w_glu, b_glu, w_out_cd,
                     w_ffn_up, w_ffn_down, w_pe_proj, w_pe_gate):
    row = lambda a: a.reshape(1, -1)
    w = {}
    w['gmix0'] = row(norm_mix[0])
    w['gmix1'] = row(norm_mix[1])
    wi = w_in_ab[0]
    cut0 = D_QKV + D_A
    cut1 = cut0 + 2 * GDN_HEADS
    w['win_ab'] = jnp.concatenate([wi[:, :cut0], wi[:, cut1:]], axis=1).astype(BF16)
    w['wba'] = jnp.pad(wi[:, cut0:cut1], ((0, 0), (0, LANES - 2 * GDN_HEADS))).astype(BF16)
    w['convw'] = jnp.pad(conv_qkv[0], ((0, SUBLANES - 4), (0, 0)))
    gp = jnp.zeros((SUBLANES, LANES), F32)
    gp = gp.at[0, GDN_HEADS:2 * GDN_HEADS].set(a_log[0]).at[1, GDN_HEADS:2 * GDN_HEADS].set(dt_bias[0])
    w['gpar'] = gp
    w['normo'] = row(norm_o[0])
    w['lng'] = row(ln_v_gain[0])
    w['lnb'] = row(ln_v_bias[0])
    w['wout_ab'] = w_out_ab[0].astype(BF16)
    w['w_spatial'] = w_spatial[0]
    w['b_spatial'] = b_spatial[0]
    w['win_cd'] = w_in_cd[0].astype(BF16)
    w['wpool'] = w_pool[0].astype(BF16)
    w['pscale'] = row(pool_scale[0])
    lb_re, lb_im, bb_re, bb_im = _s5_prep(lam_re[0], lam_im[0], log_dt[0], b_re[0], b_im[0])
    w['lam_re'] = lb_re.reshape(1, S5_MODES)
    w['lam_im'] = lb_im.reshape(1, S5_MODES)
    to_gcn = lambda a: jnp.swapaxes(a.reshape(S5_GROUPS, S5_STATE, S5_GW), 1, 2)
    half = S5_GROUPS // 2
    w['wb'] = jnp.concatenate([_block_diag(to_gcn(bb_re), half), _block_diag(to_gcn(bb_im), half)],
                              axis=2).astype(BF16)
    per_tile = LANES // S5_GW
    w['wcre'] = _block_diag(jnp.swapaxes(c_re[0], 1, 2), per_tile).astype(BF16)
    w['wcim'] = _block_diag(jnp.swapaxes(c_im[0], 1, 2), per_tile).astype(BF16)
    w['dskip'] = row(d_skip[0])
    w['wglu'] = w_glu[0].astype(BF16)
    w['bglu'] = row(b_glu[0])
    w['wout_cd'] = w_out_cd[0].astype(BF16)
    w['gffn'] = norm_ffn.reshape(2, 1, D_MODEL)
    w['gpe'] = norm_pe.reshape(2, 1, D_MODEL)
    w['gfin'] = row(norm_final)
    w['wup'] = w_ffn_up.astype(BF16)
    w['wdn'] = w_ffn_down.astype(BF16)
    w['wgate'] = w_pe_gate.astype(BF16)
    w['wproj'] = w_pe_proj.astype(BF16)
    return w


def _trunk(x, p, conv0, delta0, pool0, s5re0, s5im0, pos0, w, *, ab_tile, cd_tile, tm):
    nb, t, _ = x.shape
    nseq, tq = ab_tile
    c = min(64, tq)
    cs = min(SGU_CHUNK, tq)
    reps = SGU_CHUNK // cs
    wl = dict(w)
    wl['wsp'] = jnp.tile(w['w_spatial'][:, :cs, :cs], (1, reps, reps))
    wl['bsp'] = jnp.tile(w['b_spatial'][:, :cs], (1, reps))[:, :, None]
    conv_in = jnp.pad(conv0, ((0, 0), (SUBLANES - 3, 0), (0, 0)))
    h, conv_o, delta_o, vrows = _ab_layer(x, conv_in, delta0, wl, nseq=nseq, tq=tq, c=c, cs=cs)
    h = _ffn_pe(h.reshape(nb * t, D_MODEL), p[0].reshape(nb * t, D_PLE), w, 0, tm=tm, final=False)
    pool_in = jnp.swapaxes(jnp.pad(pool0, ((0, 0), (2 * SUBLANES - POOL_BUF, 0), (0, 0))), 0, 1)
    s5_in = jnp.concatenate([_mode_lanes(s5re0), _mode_lanes(s5im0)], axis=-1).reshape(nb, 2 * S5_MODES)
    h, pool_o, s5_o = _cd_layer(h.reshape(nb, t, D_MODEL), pool_in, s5_in, w,
                                nseq=cd_tile[0], tq=cd_tile[1], pos0=pos0)
    y = _ffn_pe(h.reshape(nb * t, D_MODEL), p[1].reshape(nb * t, D_PLE), w, 1, tm=tm, final=True)
    s5_o = s5_o.reshape(nb, 2, 2, S5_HALF)
    s5re = s5_o[:, :, 0, :].reshape(nb, S5_GROUPS, S5_STATE)
    s5im = s5_o[:, :, 1, :].reshape(nb, S5_GROUPS, S5_STATE)
    pool_o = jnp.swapaxes(pool_o, 0, 1)[:, 2 * SUBLANES - POOL_BUF:, :]
    return (y.reshape(nb, t, D_MODEL), conv_o[None, :, SUBLANES - 3:, :], delta_o[None], vrows[None],
            pool_o[None], s5re[None], s5im[None])


def kernel(x_prompt, x_sample, state_conv, state_delta, state_pool, state_s5_re, state_s5_im, p_prompt, p_sample, norm_mix, norm_ffn, norm_pe, norm_final, w_in_ab, conv_qkv, a_log, dt_bias, norm_o, ln_v_gain, ln_v_bias, w_spatial, b_spatial, w_out_ab, w_in_cd, w_pool, pool_scale, lam_re, lam_im, log_dt, b_re, b_im, c_re, c_im, d_skip, w_glu, b_glu, w_out_cd, w_ffn_up, w_ffn_down, w_pe_proj, w_pe_gate):
    w = _prepare_weights(norm_mix, norm_ffn, norm_pe, norm_final, w_in_ab, conv_qkv, a_log, dt_bias, norm_o,
                         ln_v_gain, ln_v_bias, w_spatial, b_spatial, w_out_ab, w_in_cd, w_pool, pool_scale,
                         lam_re, lam_im, log_dt, b_re, b_im, c_re, c_im, d_skip, w_glu, b_glu, w_out_cd,
                         w_ffn_up, w_ffn_down, w_pe_proj, w_pe_gate)
    bp = x_prompt.shape[0]
    zeros = lambda a: jnp.zeros((bp,) + a.shape[2:], F32)
    outs_p = _trunk(x_prompt, p_prompt, zeros(state_conv), zeros(state_delta), zeros(state_pool),
                    zeros(state_s5_re), zeros(state_s5_im), 0, w,
                    ab_tile=(1, 256), cd_tile=(8, 32), tm=512)
    outs_s = _trunk(x_sample, p_sample, state_conv[0], state_delta[0], state_pool[0],
                    state_s5_re[0], state_s5_im[0], PAST_LEN, w,
                    ab_tile=(16, 8), cd_tile=(32, 8), tm=512)
    return (outs_p[0], outs_s[0]) + outs_p[1:] + outs_s[1:]
```

```python
import functools
import math

import jax
import jax.numpy as jnp
from jax import lax
from jax.experimental import pallas as pl
from jax.experimental.pallas import tpu as pltpu

F32 = jnp.float32
BF16 = jnp.bfloat16
HIGHEST = lax.Precision.HIGHEST

EPS = 1e-6
D_MODEL = 1024
PAST_LEN = 16384
GDN_HEADS = 4
GDN_DK = 128
D_A = 512
D_QKV = 1536
D_B = 512
SGU_GROUPS = 4
SGU_CHUNK = 128
POOL_WINDOWS = (2, 4, 8, 16)
POOL_BUF = 15
D_C = 512
D_D = 512
S5_GROUPS = 32
S5_STATE = 64
S5_GW = 16
D_FF = 4096
D_PLE = 256
S5_MODES = S5_GROUPS * S5_STATE
S5_HALF = S5_MODES // 2
LANES = 128
SUBLANES = 8
GDN_SUB = 128
S5_SCAN_VREGS = 16
VMEM_LIMIT_BYTES = 56 * 1024 * 1024


def _dot(a, b):
    return jnp.dot(a, b, preferred_element_type=F32)


def _dot_nt(a, b):
    return lax.dot_general(a, b, (((1,), (1,)), ((), ())), preferred_element_type=F32)


def _dot_exact(a, b):
    return jnp.dot(a, b, precision=HIGHEST, preferred_element_type=F32)


def _rms(x, g):
    return x * lax.rsqrt(jnp.mean(x * x, axis=-1, keepdims=True) + EPS) * g


def _sigmoid(x):
    return 1.0 / (1.0 + jnp.exp(-x))


def _silu(x):
    return x * _sigmoid(x)


def _gelu_tanh(x):
    return 0.5 * x * (1.0 + jnp.tanh(math.sqrt(2.0 / math.pi) * (x + 0.044715 * (x * x * x))))


def _softplus(x):
    return jnp.maximum(x, 0.0) + jnp.log(1.0 + jnp.exp(-jnp.abs(x)))


def _log2(n):
    l = n.bit_length() - 1
    assert (1 << l) == n, n
    return l


def _ab_body(h_ref, conv0_ref, delta0_ref, gmix_ref, win_ref, wba_ref, convw_ref, gpar_ref,
             normo_ref, lng_ref, lnb_ref, wsp_ref, bsp_ref, wout_ref,
             hout_ref, conv_ref, delta_ref, vrows_ref, *, nseq, tq, c, cs):
    R = nseq * tq
    t_idx = pl.program_id(1)

    @pl.when(t_idx == 0)
    def _init():
        conv_ref[...] = conv0_ref[...]
        delta_ref[...] = delta0_ref[...]

    x = h_ref[...].reshape(R, D_MODEL)
    xn = _rms(x, gmix_ref[...]).astype(BF16)
    z = _dot(xn, win_ref[...])
    zba = _dot(xn, wba_ref[...])

    pre = z[:, :D_QKV].reshape(nseq, tq, D_QKV)
    xext = jnp.concatenate([conv_ref[...], pre], axis=1)
    cw = convw_ref[...]
    acc = pre * cw[3:4, :]
    for k in range(1, 4):
        acc = acc + pltpu.roll(xext, k, axis=1)[:, SUBLANES:, :] * cw[3 - k:4 - k, :]
    conv_ref[...] = xext[:, tq:, :]
    qkv = _silu(acc).reshape(R, D_QKV)

    gpar = gpar_ref[...]
    beta_all = _sigmoid(zba)
    g_all = -jnp.exp(gpar[0:1, :]) * _softplus(zba + gpar[1:2, :])

    sub = GDN_SUB
    log2c = _log2(c)
    row = lax.broadcasted_iota(jnp.int32, (sub, sub), 0)
    col = lax.broadcasted_iota(jnp.int32, (sub, sub), 1)
    cblk = lax.shift_right_logical(col, log2c)
    same = lax.shift_right_logical(row, log2c) == cblk
    tril = same & (row >= col)
    strict = same & (row > col)
    tril_f = tril.astype(F32)
    eye = (row == col).astype(F32)
    cblk_row = cblk[0:1, :]
    nblk = sub // c
    blk_per_seq = tq // c
    nsub = R // sub
    cat = lambda parts, axis: parts[0] if len(parts) == 1 else jnp.concatenate(parts, axis=axis)
    pairs = [(st, h) for st in range(nsub) for h in range(GDN_HEADS)]
    gc_alls, glasts, grems, gc_ts = [], [], [], []
    for st in range(nsub):
        gc_all = _dot_exact(tril_f, g_all[st * sub:(st + 1) * sub])
        glast = [gc_all[(j + 1) * c - 1:(j + 1) * c, :] for j in range(nblk)]
        gc_alls.append(gc_all)
        glasts.append(glast)
        grems.append(cat([jnp.broadcast_to(gl, (c, LANES)) for gl in glast], 0) - gc_all)
        gc_ts.append(gc_all.T)
    qs, ks, vs, betas, egs, decays, qkks = {}, {}, {}, {}, {}, {}, {}
    for p in pairs:
        st, h = p
        r0, lo = st * sub, h * GDN_DK
        q = qkv[r0:r0 + sub, lo:lo + GDN_DK]
        k = qkv[r0:r0 + sub, D_A + lo:D_A + lo + GDN_DK]
        qs[p] = q * lax.rsqrt(jnp.sum(q * q, axis=-1, keepdims=True) + EPS) * (GDN_DK ** -0.5)
        ks[p] = k * lax.rsqrt(jnp.sum(k * k, axis=-1, keepdims=True) + EPS)
        vs[p] = qkv[r0:r0 + sub, 2 * D_A + lo:2 * D_A + lo + GDN_DK]
        betas[p] = beta_all[r0:r0 + sub, h:h + 1]
        gc = gc_alls[st][:, 4 + h:5 + h]
        egs[p] = jnp.exp(gc)
        decays[p] = jnp.where(tril, jnp.exp(jnp.where(tril, gc - gc_ts[st][4 + h:5 + h, :], 0.0)), 0.0)
    for p in pairs:
        kb = ks[p].astype(BF16)
        qkks[p] = _dot_nt(jnp.concatenate([qs[p].astype(BF16), kb], axis=0), kb)
    xks, tinvs, qkds = {}, {}, {}
    for p in pairs:
        qkds[p] = (qkks[p][:sub] * decays[p]).astype(BF16)
        xks[p] = jnp.where(strict, -(betas[p] * qkks[p][sub:] * decays[p]), 0.0)
        tinvs[p] = eye + xks[p]
    if log2c > 1:
        for p in pairs:
            xb = xks[p].astype(BF16)
            xks[p] = _dot(xb, xb)
    for lev in range(1, log2c):
        res = {}
        for p in pairs:
            xb = xks[p].astype(BF16)
            if lev < log2c - 1:
                res[p] = _dot(jnp.concatenate([xb, tinvs[p].astype(BF16)], axis=0), xb)
            else:
                res[p] = _dot(tinvs[p].astype(BF16), xb)
        for p in pairs:
            if lev < log2c - 1:
                xks[p] = res[p][:sub]
                tinvs[p] = tinvs[p] + res[p][sub:]
            else:
                tinvs[p] = tinvs[p] + res[p]
    uws, q_heads, kts = {}, {}, {}
    for p in pairs:
        rhs = jnp.concatenate([(vs[p] * betas[p]).astype(BF16),
                               (ks[p] * (betas[p] * egs[p])).astype(BF16)], axis=1)
        uws[p] = _dot(tinvs[p].astype(BF16), rhs)
        q_heads[p] = qs[p] * egs[p]
        st, h = p
        kts[p] = (ks[p] * jnp.exp(grems[st][:, 4 + h:5 + h])).T
    u_lists = {p: [] for p in pairs}
    qs_lists = {p: [] for p in pairs}
    for kstep in range(blk_per_seq):
        items = []
        for seq in range(nseq):
            st, j = divmod((seq * tq + kstep * c) // c, nblk)
            items += [(seq, st, j, h) for h in range(GDN_HEADS)]
        s_olds, tmps = {}, {}
        for it in items:
            seq, st, j, h = it
            p = (st, h)
            sl = slice(j * c, (j + 1) * c)
            s_olds[it] = delta_ref[seq, h]
            both = jnp.concatenate([uws[p][sl, GDN_DK:], q_heads[p][sl]], axis=0).astype(BF16)
            tmps[it] = _dot(both, s_olds[it].astype(BF16))
        for it in items:
            seq, st, j, h = it
            p = (st, h)
            sl = slice(j * c, (j + 1) * c)
            u_base = uws[p][:, :GDN_DK]
            u_lists[p].append(u_base[sl] - tmps[it][:c])
            qs_lists[p].append(tmps[it][c:])
            u_full = cat(u_lists[p] + ([u_base[(j + 1) * c:]] if j + 1 < nblk else []), 0)
            kt_j = jnp.where(cblk_row == j, kts[p], 0.0).astype(BF16)
            delta_ref[seq, h] = (s_olds[it] * jnp.exp(glasts[st][j][:, 4 + h:5 + h])
                                 + _dot(kt_j, u_full.astype(BF16)))
    o_raw = {}
    for p in pairs:
        o_raw[p] = cat(qs_lists[p], 0) + _dot(qkds[p], cat(u_lists[p], 0).astype(BF16))
    o_tiles = []
    for st in range(nsub):
        o_heads = []
        for h in range(GDN_HEADS):
            zg = z[st * sub:(st + 1) * sub, D_QKV + h * GDN_DK:D_QKV + (h + 1) * GDN_DK]
            o_heads.append(_rms(o_raw[(st, h)], normo_ref[...]) * _silu(zg))
        o_tiles.append(jnp.concatenate(o_heads, axis=1))
    o_a = cat(o_tiles, 0)

    zu = z[:, D_QKV + D_A:D_QKV + D_A + D_B]
    zv = z[:, D_QKV + D_A + D_B:]
    u_act = _gelu_tanh(zu)
    gv = _gelu_tanh(zv)
    vc = gv - jnp.mean(gv, axis=-1, keepdims=True)
    var = jnp.mean(vc * vc, axis=-1, keepdims=True)
    v_ln = vc * lax.rsqrt(var + EPS) * lng_ref[...] + lnb_ref[...]
    rv = vrows_ref.shape[1]
    vrows_ref[...] = v_ln.reshape(nseq, tq, D_B)[:, tq - rv:, :]
    log2cs = _log2(cs)
    srow = lax.broadcasted_iota(jnp.int32, (SGU_CHUNK, SGU_CHUNK), 0)
    scol = lax.broadcasted_iota(jnp.int32, (SGU_CHUNK, SGU_CHUNK), 1)
    smask = (lax.shift_right_logical(srow, log2cs) == lax.shift_right_logical(scol, log2cs)) & (srow >= scol)
    v_bf = v_ln.astype(BF16)
    mixed_groups = []
    for g in range(SGU_GROUPS):
        wg = jnp.where(smask, wsp_ref[g], 0.0).astype(BF16)
        bg = bsp_ref[g]
        parts = []
        for r0 in range(0, R, SGU_CHUNK):
            parts.append(_dot(wg, v_bf[r0:r0 + SGU_CHUNK, g * LANES:(g + 1) * LANES]) + bg)
        mixed_groups.append(parts[0] if len(parts) == 1 else jnp.concatenate(parts, axis=0))
    o_b = u_act * jnp.concatenate(mixed_groups, axis=1)

    y_in = jnp.concatenate([o_a, o_b], axis=1).astype(BF16)
    out = x + _dot(y_in, wout_ref[...])
    hout_ref[...] = out.reshape(nseq, tq, D_MODEL)


def _ab_layer(h, conv0, delta0, w, *, nseq, tq, c, cs):
    nb, t, _ = h.shape
    grid = (nb // nseq, t // tq)
    rv = min(SGU_CHUNK, tq)
    seq_blk = lambda shape: pl.BlockSpec(shape, lambda b, s: (b,) + (0,) * (len(shape) - 1))
    full = lambda a: pl.BlockSpec(a.shape, lambda b, s: (0,) * a.ndim)
    params = [w['gmix0'], w['win_ab'], w['wba'], w['convw'], w['gpar'], w['normo'], w['lng'], w['lnb'],
              w['wsp'], w['bsp'], w['wout_ab']]
    out_shapes = (
        jax.ShapeDtypeStruct(h.shape, F32),
        jax.ShapeDtypeStruct((nb, SUBLANES, D_QKV), F32),
        jax.ShapeDtypeStruct((nb, GDN_HEADS, GDN_DK, GDN_DK), F32),
        jax.ShapeDtypeStruct((nb, rv, D_B), F32),
    )
    return pl.pallas_call(
        functools.partial(_ab_body, nseq=nseq, tq=tq, c=c, cs=cs),
        grid=grid,
        in_specs=[pl.BlockSpec((nseq, tq, D_MODEL), lambda b, s: (b, s, 0)),
                  seq_blk((nseq, SUBLANES, D_QKV)),
                  seq_blk((nseq, GDN_HEADS, GDN_DK, GDN_DK))] + [full(a) for a in params],
        out_specs=(pl.BlockSpec((nseq, tq, D_MODEL), lambda b, s: (b, s, 0)),
                   seq_blk((nseq, SUBLANES, D_QKV)),
                   seq_blk((nseq, GDN_HEADS, GDN_DK, GDN_DK)),
                   seq_blk((nseq, rv, D_B))),
        out_shape=out_shapes,
        compiler_params=pltpu.CompilerParams(
            dimension_semantics=("arbitrary", "arbitrary"), vmem_limit_bytes=VMEM_LIMIT_BYTES),
        name="ab_mixer",
    )(h, conv0, delta0, *params)


def _cd_body(h_ref, pool0_ref, s50_ref, gmix_ref, win_ref, wpool_ref, pscale_ref, wb_ref, lam_re_ref,
             lam_im_ref, wcre_ref, wcim_ref, dskip_ref, wglu_ref, bglu_ref, wout_ref,
             hout_ref, pool_ref, s5_ref, bu_ref, *, nseq, tq, pos0):
    R = nseq * tq
    t_idx = pl.program_id(1)

    @pl.when(t_idx == 0)
    def _init():
        pool_ref[...] = pool0_ref[...]
        s5_ref[...] = s50_ref[...]

    x = h_ref[...].reshape(R, D_MODEL)
    xn = _rms(x, gmix_ref[...]).astype(BF16)
    log2n = _log2(nseq)
    ri = lax.broadcasted_iota(jnp.int32, (R, R), 0)
    ci = lax.broadcasted_iota(jnp.int32, (R, R), 1)
    to_tm = (ci == (ri & (nseq - 1)) * tq + lax.shift_right_logical(ri, log2n)).astype(BF16)
    to_sm = (ri == (ci & (nseq - 1)) * tq + lax.shift_right_logical(ci, log2n)).astype(BF16)
    xt = _dot(to_tm, xn).astype(BF16)
    z = _dot(xt, win_ref[...])
    xc = z[:, :D_C].reshape(tq, nseq, D_C)
    xd = z[:, D_C:]

    xext = jnp.concatenate([pool_ref[...], xc], axis=0)
    pool_ref[...] = xext[tq:]
    pos = pos0 + t_idx * tq + lax.broadcasted_iota(jnp.int32, (tq, 1, 1), 0)
    sums = xext
    outs = []
    for gi, win in enumerate(POOL_WINDOWS):
        sums = sums[win // 2:] + sums[:-(win // 2)]
        cnt = jnp.minimum(win, pos + 1).astype(F32)
        sl = slice(gi * LANES, (gi + 1) * LANES)
        m = sums[sums.shape[0] - tq:, :, sl] / cnt - xc[:, :, sl]
        outs.append(_dot(m.reshape(R, LANES).astype(BF16), wpool_ref[gi]))
    y_c = jnp.concatenate(outs, axis=1) * pscale_ref[...]

    xd_bf = xd.astype(BF16)
    for half in range(2):
        bu_ref[:, half * 2 * S5_HALF:(half + 1) * 2 * S5_HALF] = _dot(
            xd_bf[:, half * 256:(half + 1) * 256], wb_ref[half])
    piece = S5_SCAN_VREGS * SUBLANES * LANES // (2 * nseq)
    for half in range(2):
        for p0 in range(0, S5_HALF, piece):
            o_re = half * 2 * S5_HALF + p0
            o_im = o_re + S5_HALF
            ms = slice(half * S5_HALF + p0, half * S5_HALF + p0 + piece)
            lr = jnp.broadcast_to(lam_re_ref[:, ms], (nseq, piece))
            li = jnp.broadcast_to(lam_im_ref[:, ms], (nseq, piece))

            def step(t, carry, o_re=o_re, o_im=o_im, lr=lr, li=li):
                re, im = carry
                r0 = pl.multiple_of(t * nseq, nseq)
                nre = lr * re - li * im + bu_ref[pl.ds(r0, nseq), o_re:o_re + piece]
                nim = lr * im + li * re + bu_ref[pl.ds(r0, nseq), o_im:o_im + piece]
                bu_ref[pl.ds(r0, nseq), o_re:o_re + piece] = nre
                bu_ref[pl.ds(r0, nseq), o_im:o_im + piece] = nim
                return nre, nim

            re, im = lax.fori_loop(0, tq, step, (s5_ref[:, o_re:o_re + piece], s5_ref[:, o_im:o_im + piece]))
            s5_ref[:, o_re:o_re + piece] = re
            s5_ref[:, o_im:o_im + piece] = im

    ys = []
    ngrp = S5_GROUPS // wcre_ref.shape[0]
    for n in range(wcre_ref.shape[0]):
        half, off = divmod(n * ngrp * S5_STATE, S5_HALF)
        o_re = half * 2 * S5_HALF + off
        s_re = bu_ref[:, o_re:o_re + ngrp * S5_STATE].astype(BF16)
        s_im = bu_ref[:, o_re + S5_HALF:o_re + S5_HALF + ngrp * S5_STATE].astype(BF16)
        ys.append(_dot(s_re, wcre_ref[n]) - _dot(s_im, wcim_ref[n]))
    y = jnp.concatenate(ys, axis=1) + dskip_ref[...] * xd
    y = _gelu_tanh(y)
    y_d = y * _sigmoid(_dot(y.astype(BF16), wglu_ref[...]) + bglu_ref[...])

    y_in = jnp.concatenate([y_c, y_d], axis=1).astype(BF16)
    out = x + _dot(_dot(to_sm, y_in).astype(BF16), wout_ref[...])
    hout_ref[...] = out.reshape(nseq, tq, D_MODEL)


def _cd_layer(h, pool0, s50, w, *, nseq, tq, pos0):
    nb, t, _ = h.shape
    grid = (nb // nseq, t // tq)
    full = lambda a: pl.BlockSpec(a.shape, lambda b, s: (0,) * a.ndim)
    params = [w['gmix1'], w['win_cd'], w['wpool'], w['pscale'], w['wb'], w['lam_re'], w['lam_im'],
              w['wcre'], w['wcim'], w['dskip'], w['wglu'], w['bglu'], w['wout_cd']]
    pool_spec = pl.BlockSpec((2 * SUBLANES, nseq, D_C), lambda b, s: (0, b, 0))
    s5_spec = pl.BlockSpec((nseq, 2 * S5_MODES), lambda b, s: (b, 0))
    out_shapes = (
        jax.ShapeDtypeStruct(h.shape, F32),
        jax.ShapeDtypeStruct((2 * SUBLANES, nb, D_C), F32),
        jax.ShapeDtypeStruct((nb, 2 * S5_MODES), F32),
    )
    return pl.pallas_call(
        functools.partial(_cd_body, nseq=nseq, tq=tq, pos0=pos0),
        grid=grid,
        in_specs=[pl.BlockSpec((nseq, tq, D_MODEL), lambda b, s: (b, s, 0)), pool_spec, s5_spec]
        + [full(a) for a in params],
        out_specs=(pl.BlockSpec((nseq, tq, D_MODEL), lambda b, s: (b, s, 0)), pool_spec, s5_spec),
        out_shape=out_shapes,
        scratch_shapes=[pltpu.VMEM((nseq * tq, 2 * S5_MODES), F32)],
        compiler_params=pltpu.CompilerParams(
            dimension_semantics=("arbitrary", "arbitrary"), vmem_limit_bytes=VMEM_LIMIT_BYTES),
        name="cd_mixer",
    )(h, pool0, s50, *params)


def _ffn_pe_body(h_ref, p_ref, gffn_ref, wup_ref, wdn_ref, gpe_ref, wgate_ref, wproj_ref, gfin_ref,
                 out_ref, *, final):
    x = h_ref[...]
    xn = _rms(x, gffn_ref[...]).astype(BF16)
    h = x
    ffc = D_FF // 4
    for ci in range(4):
        a = jnp.maximum(_dot(xn, wup_ref[:, ci * ffc:(ci + 1) * ffc]), 0.0)
        h = h + _dot((a * a).astype(BF16), wdn_ref[ci * ffc:(ci + 1) * ffc, :])
    hn = _rms(h, gpe_ref[...]).astype(BF16)
    gate = _sigmoid(_dot(hn, wgate_ref[...]))
    h = h + _dot(p_ref[...].astype(BF16), wproj_ref[...]) * gate
    if final:
        h = _rms(h, gfin_ref[...])
    out_ref[...] = h


def _ffn_pe(h2, p3, w, layer, *, tm, final):
    n = h2.shape[0]
    of_layer = lambda a: pl.BlockSpec((None,) + a.shape[1:], lambda i: (layer,) + (0,) * (a.ndim - 1))
    params = [w['gffn'], w['wup'], w['wdn'], w['gpe'], w['wgate'], w['wproj']]
    return pl.pallas_call(
        functools.partial(_ffn_pe_body, final=final),
        grid=(n // tm,),
        in_specs=[pl.BlockSpec((tm, D_MODEL), lambda i: (i, 0)),
                  pl.BlockSpec((None, tm, D_PLE), lambda i: (layer, i, 0))]
        + [of_layer(a) for a in params] + [pl.BlockSpec(w['gfin'].shape, lambda i: (0, 0))],
        out_specs=pl.BlockSpec((tm, D_MODEL), lambda i: (i, 0)),
        out_shape=jax.ShapeDtypeStruct(h2.shape, F32),
        compiler_params=pltpu.CompilerParams(
            dimension_semantics=("arbitrary",), vmem_limit_bytes=VMEM_LIMIT_BYTES),
        name="ffn_pe",
    )(h2, p3, *params, w['gfin'])


def _lam_bar(lr, li, ldt):
    dt = jnp.exp(ldt)
    mag = jnp.exp(lr * dt)
    ang = li * dt
    return mag * jnp.cos(ang), mag * jnp.sin(ang)


def _s5_prep_body(lre_r_ref, lim_r_ref, ldt_r_ref, lre_ref, lim_ref, ldt_ref, bre_ref, bim_ref,
                  lbre_ref, lbim_ref, bbre_ref, bbim_ref):
    lbre_ref[...], lbim_ref[...] = _lam_bar(lre_r_ref[...], lim_r_ref[...], ldt_r_ref[...])
    lr = lre_ref[...]
    li = lim_ref[...]
    lb_re, lb_im = _lam_bar(lr, li, ldt_ref[...])
    a = lb_re - 1.0
    b = lb_im
    den = lr * lr + li * li
    cr = (a * lr + b * li) / den
    ci = (b * lr - a * li) / den
    bre = bre_ref[...]
    bim = bim_ref[...]
    bbre_ref[...] = cr * bre - ci * bim
    bbim_ref[...] = cr * bim + ci * bre


def _s5_prep(lam_re, lam_im, log_dt, b_re, b_im):
    ldt = jnp.broadcast_to(log_dt[:, None], (S5_GROUPS, S5_STATE))
    rows = [a.reshape(1, S5_MODES) for a in (lam_re, lam_im, ldt)]
    cols = [a.reshape(S5_MODES, 1) for a in (lam_re, lam_im, ldt)]
    args = rows + cols + [b_re.reshape(S5_MODES, S5_GW), b_im.reshape(S5_MODES, S5_GW)]
    shp = jax.ShapeDtypeStruct((S5_MODES, S5_GW), F32)
    row_shp = jax.ShapeDtypeStruct((1, S5_MODES), F32)
    return pl.pallas_call(_s5_prep_body, out_shape=(row_shp, row_shp, shp, shp), name="s5_prep")(*args)


def _block_diag(a, ngrp):
    p, q = a.shape[1], a.shape[2]
    eye = jnp.eye(ngrp, dtype=a.dtype)
    a = a.reshape(S5_GROUPS // ngrp, ngrp, p, q)
    return jnp.einsum('hgpq,gk->hgpkq', a, eye).reshape(S5_GROUPS // ngrp, ngrp * p, ngrp * q)


def _mode_lanes(a):
    return a.reshape(a.shape[:-2] + (2, S5_HALF))


def _prepare_weights(norm_mix, norm_ffn, norm_pe, norm_final, w_in_ab, conv_qkv, a_log, dt_bias, norm_o,
                     ln_v_gain, ln_v_bias, w_spatial, b_spatial, w_out_ab, w_in_cd, w_pool, pool_scale,
                     lam_re, lam_im, log_dt, b_re, b_im, c_re, c_im, d_skip, w_glu, b_glu, w_out_cd,
                     w_ffn_up, w_ffn_down, w_pe_proj, w_pe_gate):
    row = lambda a: a.reshape(1, -1)
    w = {}
    w['gmix0'] = row(norm_mix[0])
    w['gmix1'] = row(norm_mix[1])
    wi = w_in_ab[0]
    cut0 = D_QKV + D_A
    cut1 = cut0 + 2 * GDN_HEADS
    w['win_ab'] = jnp.concatenate([wi[:, :cut0], wi[:, cut1:]], axis=1).astype(BF16)
    w['wba'] = jnp.pad(wi[:, cut0:cut1], ((0, 0), (0, LANES - 2 * GDN_HEADS))).astype(BF16)
    w['convw'] = jnp.pad(conv_qkv[0], ((0, SUBLANES - 4), (0, 0)))
    gp = jnp.zeros((SUBLANES, LANES), F32)
    gp = gp.at[0, GDN_HEADS:2 * GDN_HEADS].set(a_log[0]).at[1, GDN_HEADS:2 * GDN_HEADS].set(dt_bias[0])
    w['gpar'] = gp
    w['normo'] = row(norm_o[0])
    w['lng'] = row(ln_v_gain[0])
    w['lnb'] = row(ln_v_bias[0])
    w['wout_ab'] = w_out_ab[0].astype(BF16)
    w['w_spatial'] = w_spatial[0]
    w['b_spatial'] = b_spatial[0]
    w['win_cd'] = w_in_cd[0].astype(BF16)
    w['wpool'] = w_pool[0].astype(BF16)
    w['pscale'] = row(pool_scale[0])
    lb_re, lb_im, bb_re, bb_im = _s5_prep(lam_re[0], lam_im[0], log_dt[0], b_re[0], b_im[0])
    w['lam_re'] = lb_re
    w['lam_im'] = lb_im
    to_gcn = lambda a: jnp.swapaxes(a.reshape(S5_GROUPS, S5_STATE, S5_GW), 1, 2)
    half = S5_GROUPS // 2
    w['wb'] = jnp.concatenate([_block_diag(to_gcn(bb_re), half), _block_diag(to_gcn(bb_im), half)],
                              axis=2).astype(BF16)
    per_tile = LANES // S5_GW
    w['wcre'] = _block_diag(jnp.swapaxes(c_re[0], 1, 2), per_tile).astype(BF16)
    w['wcim'] = _block_diag(jnp.swapaxes(c_im[0], 1, 2), per_tile).astype(BF16)
    w['dskip'] = row(d_skip[0])
    w['wglu'] = w_glu[0].astype(BF16)
    w['bglu'] = row(b_glu[0])
    w['wout_cd'] = w_out_cd[0].astype(BF16)
    w['gffn'] = norm_ffn.reshape(2, 1, D_MODEL)
    w['gpe'] = norm_pe.reshape(2, 1, D_MODEL)
    w['gfin'] = row(norm_final)
    w['wup'] = w_ffn_up.astype(BF16)
    w['wdn'] = w_ffn_down.astype(BF16)
    w['wgate'] = w_pe_gate.astype(BF16)
    w['wproj'] = w_pe_proj.astype(BF16)
    return w


def _trunk(x, p, conv0, delta0, pool0, s5re0, s5im0, pos0, w, *, ab_tile, cd_tile, tm):
    nb, t, _ = x.shape
    nseq, tq = ab_tile
    c = min(64, tq)
    cs = min(SGU_CHUNK, tq)
    reps = SGU_CHUNK // cs
    wl = dict(w)
    wl['wsp'] = jnp.tile(w['w_spatial'][:, :cs, :cs], (1, reps, reps))
    wl['bsp'] = jnp.tile(w['b_spatial'][:, :cs], (1, reps))[:, :, None]
    conv_in = jnp.pad(conv0, ((0, 0), (SUBLANES - 3, 0), (0, 0)))
    h, conv_o, delta_o, vrows = _ab_layer(x, conv_in, delta0, wl, nseq=nseq, tq=tq, c=c, cs=cs)
    p3 = p.reshape(p.shape[0], nb * t, D_PLE)
    h = _ffn_pe(h.reshape(nb * t, D_MODEL), p3, w, 0, tm=tm, final=False)
    pool_in = jnp.swapaxes(jnp.pad(pool0, ((0, 0), (2 * SUBLANES - POOL_BUF, 0), (0, 0))), 0, 1)
    s5_in = jnp.concatenate([_mode_lanes(s5re0), _mode_lanes(s5im0)], axis=-1).reshape(nb, 2 * S5_MODES)
    h, pool_o, s5_o = _cd_layer(h.reshape(nb, t, D_MODEL), pool_in, s5_in, w,
                                nseq=cd_tile[0], tq=cd_tile[1], pos0=pos0)
    y = _ffn_pe(h.reshape(nb * t, D_MODEL), p3, w, 1, tm=tm, final=True)
    s5_o = s5_o.reshape(nb, 2, 2, S5_HALF)
    s5re = s5_o[:, :, 0, :].reshape(nb, S5_GROUPS, S5_STATE)
    s5im = s5_o[:, :, 1, :].reshape(nb, S5_GROUPS, S5_STATE)
    pool_o = jnp.swapaxes(pool_o, 0, 1)[:, 2 * SUBLANES - POOL_BUF:, :]
    return (y.reshape(nb, t, D_MODEL), conv_o[None, :, SUBLANES - 3:, :], delta_o[None], vrows[None],
            pool_o[None], s5re[None], s5im[None])


def kernel(x_prompt, x_sample, state_conv, state_delta, state_pool, state_s5_re, state_s5_im, p_prompt, p_sample, norm_mix, norm_ffn, norm_pe, norm_final, w_in_ab, conv_qkv, a_log, dt_bias, norm_o, ln_v_gain, ln_v_bias, w_spatial, b_spatial, w_out_ab, w_in_cd, w_pool, pool_scale, lam_re, lam_im, log_dt, b_re, b_im, c_re, c_im, d_skip, w_glu, b_glu, w_out_cd, w_ffn_up, w_ffn_down, w_pe_proj, w_pe_gate):
    w = _prepare_weights(norm_mix, norm_ffn, norm_pe, norm_final, w_in_ab, conv_qkv, a_log, dt_bias, norm_o,
                         ln_v_gain, ln_v_bias, w_spatial, b_spatial, w_out_ab, w_in_cd, w_pool, pool_scale,
                         lam_re, lam_im, log_dt, b_re, b_im, c_re, c_im, d_skip, w_glu, b_glu, w_out_cd,
                         w_ffn_up, w_ffn_down, w_pe_proj, w_pe_gate)
    bp = x_prompt.shape[0]
    zeros = lambda a: jnp.zeros((bp,) + a.shape[2:], F32)
    outs_p = _trunk(x_prompt, p_prompt, zeros(state_conv), zeros(state_delta), zeros(state_pool),
                    zeros(state_s5_re), zeros(state_s5_im), 0, w,
                    ab_tile=(2, 256), cd_tile=(8, 32), tm=512)
    outs_s = _trunk(x_sample, p_sample, state_conv[0], state_delta[0], state_pool[0],
                    state_s5_re[0], state_s5_im[0], PAST_LEN, w,
                    ab_tile=(16, 8), cd_tile=(32, 8), tm=512)
    return (outs_p[0], outs_s[0]) + outs_p[1:] + outs_s[1:]
```

```python
import functools
import math

import jax
import jax.numpy as jnp
from jax import lax
from jax.experimental import pallas as pl
from jax.experimental.pallas import tpu as pltpu

F32 = jnp.float32
BF16 = jnp.bfloat16
HIGHEST = lax.Precision.HIGHEST

EPS = 1e-6
D_MODEL = 1024
PAST_LEN = 16384
GDN_HEADS = 4
GDN_DK = 128
D_A = 512
D_QKV = 1536
D_B = 512
SGU_GROUPS = 4
SGU_CHUNK = 128
POOL_WINDOWS = (2, 4, 8, 16)
POOL_BUF = 15
D_C = 512
D_D = 512
S5_GROUPS = 32
S5_STATE = 64
S5_GW = 16
D_FF = 4096
D_PLE = 256
S5_MODES = S5_GROUPS * S5_STATE
S5_HALF = S5_MODES // 2
LANES = 128
SUBLANES = 8
GDN_SUB = 128
S5_SCAN_VREGS = 16
CD_PERM_ROWS = 256
VMEM_LIMIT_BYTES = 56 * 1024 * 1024


def _dot(a, b):
    return jnp.dot(a, b, preferred_element_type=F32)


def _dot_nt(a, b):
    return lax.dot_general(a, b, (((1,), (1,)), ((), ())), preferred_element_type=F32)


def _dot_exact(a, b):
    return jnp.dot(a, b, precision=HIGHEST, preferred_element_type=F32)


def _rms(x, g):
    return x * lax.rsqrt(jnp.mean(x * x, axis=-1, keepdims=True) + EPS) * g


def _sigmoid(x):
    return 0.5 * jnp.tanh(0.5 * x) + 0.5


def _silu(x):
    return x * _sigmoid(x)


def _gelu_tanh(x):
    return 0.5 * x * (1.0 + jnp.tanh(math.sqrt(2.0 / math.pi) * (x + 0.044715 * (x * x * x))))


def _softplus(x):
    return jnp.maximum(x, 0.0) + jnp.log(1.0 + jnp.exp(-jnp.abs(x)))


def _log2(n):
    l = n.bit_length() - 1
    assert (1 << l) == n, n
    return l


def _ab_body(h_ref, conv0_ref, delta0_ref, gmix_ref, win_ref, wba_ref, convw_ref, gpar_ref,
             normo_ref, lng_ref, lnb_ref, wsp_ref, bsp_ref, wout_ref,
             hout_ref, conv_ref, delta_ref, vrows_ref, *, nseq, tq, c, cs):
    R = nseq * tq
    t_idx = pl.program_id(1)

    @pl.when(t_idx == 0)
    def _init():
        conv_ref[...] = conv0_ref[...]
        delta_ref[...] = delta0_ref[...]

    x = h_ref[...].reshape(R, D_MODEL)
    xn = _rms(x, gmix_ref[...]).astype(BF16)
    z = _dot(xn, win_ref[...])
    zba = _dot(xn, wba_ref[...])

    pre = z[:, :D_QKV].reshape(nseq, tq, D_QKV)
    xext = jnp.concatenate([conv_ref[...], pre], axis=1)
    cw = convw_ref[...]
    acc = pre * cw[3:4, :]
    for k in range(1, 4):
        acc = acc + pltpu.roll(xext, k, axis=1)[:, SUBLANES:, :] * cw[3 - k:4 - k, :]
    conv_ref[...] = xext[:, tq:, :]
    qkv = _silu(acc).reshape(R, D_QKV)

    gpar = gpar_ref[...]
    beta_all = _sigmoid(zba)
    g_all = -jnp.exp(gpar[0:1, :]) * _softplus(zba + gpar[1:2, :])

    sub = GDN_SUB
    log2c = _log2(c)
    row = lax.broadcasted_iota(jnp.int32, (sub, sub), 0)
    col = lax.broadcasted_iota(jnp.int32, (sub, sub), 1)
    cblk = lax.shift_right_logical(col, log2c)
    same = lax.shift_right_logical(row, log2c) == cblk
    tril = same & (row >= col)
    strict = same & (row > col)
    tril_f = tril.astype(F32)
    eye = (row == col).astype(F32)
    cblk_row = cblk[0:1, :]
    nblk = sub // c
    blk_per_seq = tq // c
    nsub = R // sub
    cat = lambda parts, axis: parts[0] if len(parts) == 1 else jnp.concatenate(parts, axis=axis)
    pairs = [(st, h) for st in range(nsub) for h in range(GDN_HEADS)]
    gc_alls, glasts, grems, gc_ts = [], [], [], []
    for st in range(nsub):
        gc_all = _dot_exact(tril_f, g_all[st * sub:(st + 1) * sub])
        glast = [gc_all[(j + 1) * c - 1:(j + 1) * c, :] for j in range(nblk)]
        gc_alls.append(gc_all)
        glasts.append(glast)
        grems.append(cat([jnp.broadcast_to(gl, (c, LANES)) for gl in glast], 0) - gc_all)
        gc_ts.append(gc_all.T)
    qs, ks, vs, betas, egs, decays, qkks = {}, {}, {}, {}, {}, {}, {}
    for p in pairs:
        st, h = p
        r0, lo = st * sub, h * GDN_DK
        q = qkv[r0:r0 + sub, lo:lo + GDN_DK]
        k = qkv[r0:r0 + sub, D_A + lo:D_A + lo + GDN_DK]
        qs[p] = q * lax.rsqrt(jnp.sum(q * q, axis=-1, keepdims=True) + EPS) * (GDN_DK ** -0.5)
        ks[p] = k * lax.rsqrt(jnp.sum(k * k, axis=-1, keepdims=True) + EPS)
        vs[p] = qkv[r0:r0 + sub, 2 * D_A + lo:2 * D_A + lo + GDN_DK]
        betas[p] = beta_all[r0:r0 + sub, h:h + 1]
        gc = gc_alls[st][:, 4 + h:5 + h]
        egs[p] = jnp.exp(gc)
        decays[p] = jnp.where(tril, jnp.exp(jnp.where(tril, gc - gc_ts[st][4 + h:5 + h, :], 0.0)), 0.0)
    for p in pairs:
        kb = ks[p].astype(BF16)
        qkks[p] = _dot_nt(jnp.concatenate([qs[p].astype(BF16), kb], axis=0), kb)
    xks, tinvs, qkds = {}, {}, {}
    for p in pairs:
        qkds[p] = (qkks[p][:sub] * decays[p]).astype(BF16)
        xks[p] = jnp.where(strict, -(betas[p] * qkks[p][sub:] * decays[p]), 0.0)
        tinvs[p] = eye + xks[p]
    if log2c > 1:
        for p in pairs:
            xb = xks[p].astype(BF16)
            xks[p] = _dot(xb, xb)
    for lev in range(1, log2c):
        res = {}
        for p in pairs:
            xb = xks[p].astype(BF16)
            if lev < log2c - 1:
                res[p] = _dot(jnp.concatenate([xb, tinvs[p].astype(BF16)], axis=0), xb)
            else:
                res[p] = _dot(tinvs[p].astype(BF16), xb)
        for p in pairs:
            if lev < log2c - 1:
                xks[p] = res[p][:sub]
                tinvs[p] = tinvs[p] + res[p][sub:]
            else:
                tinvs[p] = tinvs[p] + res[p]
    uws, q_heads, kts = {}, {}, {}
    for p in pairs:
        rhs = jnp.concatenate([(vs[p] * betas[p]).astype(BF16),
                               (ks[p] * (betas[p] * egs[p])).astype(BF16)], axis=1)
        uws[p] = _dot(tinvs[p].astype(BF16), rhs)
        q_heads[p] = qs[p] * egs[p]
        st, h = p
        kts[p] = (ks[p] * jnp.exp(grems[st][:, 4 + h:5 + h])).T
    u_lists = {p: [] for p in pairs}
    qs_lists = {p: [] for p in pairs}
    for kstep in range(blk_per_seq):
        items = []
        for seq in range(nseq):
            st, j = divmod((seq * tq + kstep * c) // c, nblk)
            items += [(seq, st, j, h) for h in range(GDN_HEADS)]
        s_olds, tmps = {}, {}
        for it in items:
            seq, st, j, h = it
            p = (st, h)
            sl = slice(j * c, (j + 1) * c)
            s_olds[it] = delta_ref[seq, h]
            both = jnp.concatenate([uws[p][sl, GDN_DK:], q_heads[p][sl]], axis=0).astype(BF16)
            tmps[it] = _dot(both, s_olds[it].astype(BF16))
        for it in items:
            seq, st, j, h = it
            p = (st, h)
            sl = slice(j * c, (j + 1) * c)
            u_base = uws[p][:, :GDN_DK]
            u_lists[p].append(u_base[sl] - tmps[it][:c])
            qs_lists[p].append(tmps[it][c:])
            u_full = cat(u_lists[p] + ([u_base[(j + 1) * c:]] if j + 1 < nblk else []), 0)
            kt_j = jnp.where(cblk_row == j, kts[p], 0.0).astype(BF16)
            delta_ref[seq, h] = (s_olds[it] * jnp.exp(glasts[st][j][:, 4 + h:5 + h])
                                 + _dot(kt_j, u_full.astype(BF16)))
    o_raw = {}
    for p in pairs:
        o_raw[p] = cat(qs_lists[p], 0) + _dot(qkds[p], cat(u_lists[p], 0).astype(BF16))
    o_tiles = []
    for st in range(nsub):
        o_heads = []
        for h in range(GDN_HEADS):
            zg = z[st * sub:(st + 1) * sub, D_QKV + h * GDN_DK:D_QKV + (h + 1) * GDN_DK]
            o_heads.append(_rms(o_raw[(st, h)], normo_ref[...]) * _silu(zg))
        o_tiles.append(jnp.concatenate(o_heads, axis=1))
    o_a = cat(o_tiles, 0)

    zu = z[:, D_QKV + D_A:D_QKV + D_A + D_B]
    zv = z[:, D_QKV + D_A + D_B:]
    u_act = _gelu_tanh(zu)
    gv = _gelu_tanh(zv)
    vc = gv - jnp.mean(gv, axis=-1, keepdims=True)
    var = jnp.mean(vc * vc, axis=-1, keepdims=True)
    v_ln = vc * lax.rsqrt(var + EPS) * lng_ref[...] + lnb_ref[...]
    rv = vrows_ref.shape[1]
    vrows_ref[...] = v_ln.reshape(nseq, tq, D_B)[:, tq - rv:, :]
    log2cs = _log2(cs)
    srow = lax.broadcasted_iota(jnp.int32, (SGU_CHUNK, SGU_CHUNK), 0)
    scol = lax.broadcasted_iota(jnp.int32, (SGU_CHUNK, SGU_CHUNK), 1)
    smask = (lax.shift_right_logical(srow, log2cs) == lax.shift_right_logical(scol, log2cs)) & (srow >= scol)
    v_bf = v_ln.astype(BF16)
    mixed_groups = []
    for g in range(SGU_GROUPS):
        wg = jnp.where(smask, wsp_ref[g], 0.0).astype(BF16)
        bg = bsp_ref[g]
        parts = []
        for r0 in range(0, R, SGU_CHUNK):
            parts.append(_dot(wg, v_bf[r0:r0 + SGU_CHUNK, g * LANES:(g + 1) * LANES]) + bg)
        mixed_groups.append(parts[0] if len(parts) == 1 else jnp.concatenate(parts, axis=0))
    o_b = u_act * jnp.concatenate(mixed_groups, axis=1)

    y_in = jnp.concatenate([o_a, o_b], axis=1).astype(BF16)
    out = x + _dot(y_in, wout_ref[...])
    hout_ref[...] = out.reshape(nseq, tq, D_MODEL)


def _ab_layer(h, conv0, delta0, w, *, nseq, tq, c, cs):
    nb, t, _ = h.shape
    grid = (nb // nseq, t // tq)
    rv = min(SGU_CHUNK, tq)
    seq_blk = lambda shape: pl.BlockSpec(shape, lambda b, s: (b,) + (0,) * (len(shape) - 1))
    full = lambda a: pl.BlockSpec(a.shape, lambda b, s: (0,) * a.ndim)
    params = [w['gmix0'], w['win_ab'], w['wba'], w['convw'], w['gpar'], w['normo'], w['lng'], w['lnb'],
              w['wsp'], w['bsp'], w['wout_ab']]
    out_shapes = (
        jax.ShapeDtypeStruct(h.shape, F32),
        jax.ShapeDtypeStruct((nb, SUBLANES, D_QKV), F32),
        jax.ShapeDtypeStruct((nb, GDN_HEADS, GDN_DK, GDN_DK), F32),
        jax.ShapeDtypeStruct((nb, rv, D_B), F32),
    )
    return pl.pallas_call(
        functools.partial(_ab_body, nseq=nseq, tq=tq, c=c, cs=cs),
        grid=grid,
        in_specs=[pl.BlockSpec((nseq, tq, D_MODEL), lambda b, s: (b, s, 0)),
                  seq_blk((nseq, SUBLANES, D_QKV)),
                  seq_blk((nseq, GDN_HEADS, GDN_DK, GDN_DK))] + [full(a) for a in params],
        out_specs=(pl.BlockSpec((nseq, tq, D_MODEL), lambda b, s: (b, s, 0)),
                   seq_blk((nseq, SUBLANES, D_QKV)),
                   seq_blk((nseq, GDN_HEADS, GDN_DK, GDN_DK)),
                   seq_blk((nseq, rv, D_B))),
        out_shape=out_shapes,
        compiler_params=pltpu.CompilerParams(
            dimension_semantics=("arbitrary", "arbitrary"), vmem_limit_bytes=VMEM_LIMIT_BYTES),
        name="ab_mixer",
    )(h, conv0, delta0, *params)


def _cd_body(h_ref, pool0_ref, s50_ref, gmix_ref, win_ref, wpool_ref, pscale_ref, wb_ref, lam_re_ref,
             lam_im_ref, wcre_ref, wcim_ref, dskip_ref, wglu_ref, bglu_ref, wout_ref,
             hout_ref, pool_ref, s5_ref, bu_ref, *, nseq, tq, pos0):
    R = nseq * tq
    t_idx = pl.program_id(1)

    @pl.when(t_idx == 0)
    def _init():
        pool_ref[...] = pool0_ref[...]
        s5_ref[...] = s50_ref[...]

    x3 = h_ref[...]
    xn3 = _rms(x3, gmix_ref[...]).astype(BF16)
    nper = R // CD_PERM_ROWS
    tsub = tq // nper
    log2n = _log2(nseq)
    ri = lax.broadcasted_iota(jnp.int32, (CD_PERM_ROWS, CD_PERM_ROWS), 0)
    ci = lax.broadcasted_iota(jnp.int32, (CD_PERM_ROWS, CD_PERM_ROWS), 1)
    to_tm = (ci == (ri & (nseq - 1)) * tsub + lax.shift_right_logical(ri, log2n)).astype(BF16)
    to_sm = (ri == (ci & (nseq - 1)) * tsub + lax.shift_right_logical(ci, log2n)).astype(BF16)
    xt = jnp.concatenate(
        [_dot(to_tm, xn3[:, i * tsub:(i + 1) * tsub, :].reshape(CD_PERM_ROWS, D_MODEL)).astype(BF16)
         for i in range(nper)], axis=0)
    z = _dot(xt, win_ref[...])
    xc = z[:, :D_C].reshape(tq, nseq, D_C)
    xd = z[:, D_C:]

    xext = jnp.concatenate([pool_ref[...], xc], axis=0)
    pool_ref[...] = xext[tq:]
    pos = pos0 + t_idx * tq + lax.broadcasted_iota(jnp.int32, (tq, 1, 1), 0)
    sums = xext
    outs = []
    for gi, win in enumerate(POOL_WINDOWS):
        sums = sums[win // 2:] + sums[:-(win // 2)]
        cnt = jnp.minimum(win, pos + 1).astype(F32)
        sl = slice(gi * LANES, (gi + 1) * LANES)
        m = sums[sums.shape[0] - tq:, :, sl] / cnt - xc[:, :, sl]
        outs.append(_dot(m.reshape(R, LANES).astype(BF16), wpool_ref[gi]))
    y_c = jnp.concatenate(outs, axis=1) * pscale_ref[...]

    xd_bf = xd.astype(BF16)
    for half in range(2):
        bu_ref[:, half * 2 * S5_HALF:(half + 1) * 2 * S5_HALF] = _dot(
            xd_bf[:, half * 256:(half + 1) * 256], wb_ref[half])
    piece = S5_SCAN_VREGS * SUBLANES * LANES // (2 * nseq)
    for half in range(2):
        for p0 in range(0, S5_HALF, piece):
            o_re = half * 2 * S5_HALF + p0
            o_im = o_re + S5_HALF
            ms = slice(half * S5_HALF + p0, half * S5_HALF + p0 + piece)
            lr = jnp.broadcast_to(lam_re_ref[:, ms], (nseq, piece))
            li = jnp.broadcast_to(lam_im_ref[:, ms], (nseq, piece))

            def step(t, carry, o_re=o_re, o_im=o_im, lr=lr, li=li):
                re, im = carry
                r0 = pl.multiple_of(t * nseq, nseq)
                nre = lr * re - li * im + bu_ref[pl.ds(r0, nseq), o_re:o_re + piece]
                nim = lr * im + li * re + bu_ref[pl.ds(r0, nseq), o_im:o_im + piece]
                bu_ref[pl.ds(r0, nseq), o_re:o_re + piece] = nre
                bu_ref[pl.ds(r0, nseq), o_im:o_im + piece] = nim
                return nre, nim

            re, im = lax.fori_loop(0, tq, step, (s5_ref[:, o_re:o_re + piece], s5_ref[:, o_im:o_im + piece]),
                                   unroll=4)
            s5_ref[:, o_re:o_re + piece] = re
            s5_ref[:, o_im:o_im + piece] = im

    ys = []
    ngrp = S5_GROUPS // wcre_ref.shape[0]
    for n in range(wcre_ref.shape[0]):
        half, off = divmod(n * ngrp * S5_STATE, S5_HALF)
        o_re = half * 2 * S5_HALF + off
        s_re = bu_ref[:, o_re:o_re + ngrp * S5_STATE].astype(BF16)
        s_im = bu_ref[:, o_re + S5_HALF:o_re + S5_HALF + ngrp * S5_STATE].astype(BF16)
        ys.append(_dot(s_re, wcre_ref[n]) - _dot(s_im, wcim_ref[n]))
    y = jnp.concatenate(ys, axis=1) + dskip_ref[...] * xd
    y = _gelu_tanh(y)
    y_d = y * _sigmoid(_dot(y.astype(BF16), wglu_ref[...]) + bglu_ref[...])

    y_in = jnp.concatenate([y_c, y_d], axis=1).astype(BF16)
    y_sm = jnp.concatenate(
        [_dot(to_sm, y_in[i * CD_PERM_ROWS:(i + 1) * CD_PERM_ROWS]).astype(BF16) for i in range(nper)], axis=0)
    proj = _dot(y_sm, wout_ref[...])
    for i in range(nper):
        ts = slice(i * tsub, (i + 1) * tsub)
        blk = proj[i * CD_PERM_ROWS:(i + 1) * CD_PERM_ROWS].reshape(nseq, tsub, D_MODEL)
        hout_ref[:, ts, :] = x3[:, ts, :] + blk


def _cd_layer(h, pool0, s50, w, *, nseq, tq, pos0):
    nb, t, _ = h.shape
    grid = (nb // nseq, t // tq)
    full = lambda a: pl.BlockSpec(a.shape, lambda b, s: (0,) * a.ndim)
    params = [w['gmix1'], w['win_cd'], w['wpool'], w['pscale'], w['wb'], w['lam_re'], w['lam_im'],
              w['wcre'], w['wcim'], w['dskip'], w['wglu'], w['bglu'], w['wout_cd']]
    pool_spec = pl.BlockSpec((2 * SUBLANES, nseq, D_C), lambda b, s: (0, b, 0))
    s5_spec = pl.BlockSpec((nseq, 2 * S5_MODES), lambda b, s: (b, 0))
    out_shapes = (
        jax.ShapeDtypeStruct(h.shape, F32),
        jax.ShapeDtypeStruct((2 * SUBLANES, nb, D_C), F32),
        jax.ShapeDtypeStruct((nb, 2 * S5_MODES), F32),
    )
    return pl.pallas_call(
        functools.partial(_cd_body, nseq=nseq, tq=tq, pos0=pos0),
        grid=grid,
        in_specs=[pl.BlockSpec((nseq, tq, D_MODEL), lambda b, s: (b, s, 0)), pool_spec, s5_spec]
        + [full(a) for a in params],
        out_specs=(pl.BlockSpec((nseq, tq, D_MODEL), lambda b, s: (b, s, 0)), pool_spec, s5_spec),
        out_shape=out_shapes,
        scratch_shapes=[pltpu.VMEM((nseq * tq, 2 * S5_MODES), F32)],
        compiler_params=pltpu.CompilerParams(
            dimension_semantics=("arbitrary", "arbitrary"), vmem_limit_bytes=VMEM_LIMIT_BYTES),
        name="cd_mixer",
    )(h, pool0, s50, *params)


def _ffn_pe_body(h_ref, p_ref, gffn_ref, wup_ref, wdn_ref, gpe_ref, wgate_ref, wproj_ref, gfin_ref,
                 out_ref, *, final):
    x = h_ref[...]
    xn = _rms(x, gffn_ref[...]).astype(BF16)
    h = x
    ffc = D_FF // 4
    for ci in range(4):
        a = jnp.maximum(_dot(xn, wup_ref[:, ci * ffc:(ci + 1) * ffc]), 0.0)
        h = h + _dot((a * a).astype(BF16), wdn_ref[ci * ffc:(ci + 1) * ffc, :])
    hn = _rms(h, gpe_ref[...]).astype(BF16)
    gate = _sigmoid(_dot(hn, wgate_ref[...]))
    h = h + _dot(p_ref[...].astype(BF16), wproj_ref[...]) * gate
    if final:
        h = _rms(h, gfin_ref[...])
    out_ref[...] = h


def _ffn_pe(h2, p3, w, layer, *, tm, final):
    n = h2.shape[0]
    of_layer = lambda a: pl.BlockSpec((None,) + a.shape[1:], lambda i: (layer,) + (0,) * (a.ndim - 1))
    params = [w['gffn'], w['wup'], w['wdn'], w['gpe'], w['wgate'], w['wproj']]
    return pl.pallas_call(
        functools.partial(_ffn_pe_body, final=final),
        grid=(n // tm,),
        in_specs=[pl.BlockSpec((tm, D_MODEL), lambda i: (i, 0)),
                  pl.BlockSpec((None, tm, D_PLE), lambda i: (layer, i, 0))]
        + [of_layer(a) for a in params] + [pl.BlockSpec(w['gfin'].shape, lambda i: (0, 0))],
        out_specs=pl.BlockSpec((tm, D_MODEL), lambda i: (i, 0)),
        out_shape=jax.ShapeDtypeStruct(h2.shape, F32),
        compiler_params=pltpu.CompilerParams(
            dimension_semantics=("arbitrary",), vmem_limit_bytes=VMEM_LIMIT_BYTES),
        name="ffn_pe",
    )(h2, p3, *params, w['gfin'])


def _lam_bar(lr, li, ldt):
    dt = jnp.exp(ldt)
    mag = jnp.exp(lr * dt)
    ang = li * dt
    return mag * jnp.cos(ang), mag * jnp.sin(ang)


def _s5_prep_body(lre_r_ref, lim_r_ref, ldt_r_ref, lre_ref, lim_ref, ldt_ref, bre_ref, bim_ref,
                  lbre_ref, lbim_ref, bbre_ref, bbim_ref):
    lbre_ref[...], lbim_ref[...] = _lam_bar(lre_r_ref[...], lim_r_ref[...], ldt_r_ref[...])
    lr = lre_ref[...]
    li = lim_ref[...]
    lb_re, lb_im = _lam_bar(lr, li, ldt_ref[...])
    a = lb_re - 1.0
    b = lb_im
    den = lr * lr + li * li
    cr = (a * lr + b * li) / den
    ci = (b * lr - a * li) / den
    bre = bre_ref[...]
    bim = bim_ref[...]
    bbre_ref[...] = cr * bre - ci * bim
    bbim_ref[...] = cr * bim + ci * bre


def _s5_prep(lam_re, lam_im, log_dt, b_re, b_im):
    ldt = jnp.broadcast_to(log_dt[:, None], (S5_GROUPS, S5_STATE))
    rows = [a.reshape(1, S5_MODES) for a in (lam_re, lam_im, ldt)]
    cols = [a.reshape(S5_MODES, 1) for a in (lam_re, lam_im, ldt)]
    args = rows + cols + [b_re.reshape(S5_MODES, S5_GW), b_im.reshape(S5_MODES, S5_GW)]
    shp = jax.ShapeDtypeStruct((S5_MODES, S5_GW), F32)
    row_shp = jax.ShapeDtypeStruct((1, S5_MODES), F32)
    return pl.pallas_call(_s5_prep_body, out_shape=(row_shp, row_shp, shp, shp), name="s5_prep")(*args)


def _block_diag(a, ngrp):
    p, q = a.shape[1], a.shape[2]
    eye = jnp.eye(ngrp, dtype=a.dtype)
    a = a.reshape(S5_GROUPS // ngrp, ngrp, p, q)
    return jnp.einsum('hgpq,gk->hgpkq', a, eye).reshape(S5_GROUPS // ngrp, ngrp * p, ngrp * q)


def _mode_lanes(a):
    return a.reshape(a.shape[:-2] + (2, S5_HALF))


def _prepare_weights(norm_mix, norm_ffn, norm_pe, norm_final, w_in_ab, conv_qkv, a_log, dt_bias, norm_o,
                     ln_v_gain, ln_v_bias, w_spatial, b_spatial, w_out_ab, w_in_cd, w_pool, pool_scale,
                     lam_re, lam_im, log_dt, b_re, b_im, c_re, c_im, d_skip, w_glu, b_glu, w_out_cd,
                     w_ffn_up, w_ffn_down, w_pe_proj, w_pe_gate):
    row = lambda a: a.reshape(1, -1)
    w = {}
    w['gmix0'] = row(norm_mix[0])
    w['gmix1'] = row(norm_mix[1])
    wi = w_in_ab[0]
    cut0 = D_QKV + D_A
    cut1 = cut0 + 2 * GDN_HEADS
    w['win_ab'] = jnp.concatenate([wi[:, :cut0], wi[:, cut1:]], axis=1).astype(BF16)
    w['wba'] = jnp.pad(wi[:, cut0:cut1], ((0, 0), (0, LANES - 2 * GDN_HEADS))).astype(BF16)
    w['convw'] = jnp.pad(conv_qkv[0], ((0, SUBLANES - 4), (0, 0)))
    gp = jnp.zeros((SUBLANES, LANES), F32)
    gp = gp.at[0, GDN_HEADS:2 * GDN_HEADS].set(a_log[0]).at[1, GDN_HEADS:2 * GDN_HEADS].set(dt_bias[0])
    w['gpar'] = gp
    w['normo'] = row(norm_o[0])
    w['lng'] = row(ln_v_gain[0])
    w['lnb'] = row(ln_v_bias[0])
    w['wout_ab'] = w_out_ab[0].astype(BF16)
    w['w_spatial'] = w_spatial[0]
    w['b_spatial'] = b_spatial[0]
    w['win_cd'] = w_in_cd[0].astype(BF16)
    w['wpool'] = w_pool[0].astype(BF16)
    w['pscale'] = row(pool_scale[0])
    lb_re, lb_im, bb_re, bb_im = _s5_prep(lam_re[0], lam_im[0], log_dt[0], b_re[0], b_im[0])
    w['lam_re'] = lb_re
    w['lam_im'] = lb_im
    to_gcn = lambda a: jnp.swapaxes(a.reshape(S5_GROUPS, S5_STATE, S5_GW), 1, 2)
    half = S5_GROUPS // 2
    w['wb'] = jnp.concatenate([_block_diag(to_gcn(bb_re), half), _block_diag(to_gcn(bb_im), half)],
                              axis=2).astype(BF16)
    per_tile = LANES // S5_GW
    w['wcre'] = _block_diag(jnp.swapaxes(c_re[0], 1, 2), per_tile).astype(BF16)
    w['wcim'] = _block_diag(jnp.swapaxes(c_im[0], 1, 2), per_tile).astype(BF16)
    w['dskip'] = row(d_skip[0])
    w['wglu'] = w_glu[0].astype(BF16)
    w['bglu'] = row(b_glu[0])
    w['wout_cd'] = w_out_cd[0].astype(BF16)
    w['gffn'] = norm_ffn.reshape(2, 1, D_MODEL)
    w['gpe'] = norm_pe.reshape(2, 1, D_MODEL)
    w['gfin'] = row(norm_final)
    w['wup'] = w_ffn_up.astype(BF16)
    w['wdn'] = w_ffn_down.astype(BF16)
    w['wgate'] = w_pe_gate.astype(BF16)
    w['wproj'] = w_pe_proj.astype(BF16)
    return w


def _trunk(x, p, conv0, delta0, pool0, s5re0, s5im0, pos0, w, *, ab_tile, cd_tile, tm):
    nb, t, _ = x.shape
    nseq, tq = ab_tile
    c = min(64, tq)
    cs = min(SGU_CHUNK, tq)
    reps = SGU_CHUNK // cs
    wl = dict(w)
    wl['wsp'] = jnp.tile(w['w_spatial'][:, :cs, :cs], (1, reps, reps))
    wl['bsp'] = jnp.tile(w['b_spatial'][:, :cs], (1, reps))[:, :, None]
    conv_in = jnp.pad(conv0, ((0, 0), (SUBLANES - 3, 0), (0, 0)))
    h, conv_o, delta_o, vrows = _ab_layer(x, conv_in, delta0, wl, nseq=nseq, tq=tq, c=c, cs=cs)
    p3 = p.reshape(p.shape[0], nb * t, D_PLE)
    h = _ffn_pe(h.reshape(nb * t, D_MODEL), p3, w, 0, tm=tm, final=False)
    pool_in = jnp.swapaxes(jnp.pad(pool0, ((0, 0), (2 * SUBLANES - POOL_BUF, 0), (0, 0))), 0, 1)
    s5_in = jnp.concatenate([_mode_lanes(s5re0), _mode_lanes(s5im0)], axis=-1).reshape(nb, 2 * S5_MODES)
    h, pool_o, s5_o = _cd_layer(h.reshape(nb, t, D_MODEL), pool_in, s5_in, w,
                                nseq=cd_tile[0], tq=cd_tile[1], pos0=pos0)
    y = _ffn_pe(h.reshape(nb * t, D_MODEL), p3, w, 1, tm=tm, final=True)
    s5_o = s5_o.reshape(nb, 2, 2, S5_HALF)
    s5re = s5_o[:, :, 0, :].reshape(nb, S5_GROUPS, S5_STATE)
    s5im = s5_o[:, :, 1, :].reshape(nb, S5_GROUPS, S5_STATE)
    pool_o = jnp.swapaxes(pool_o, 0, 1)[:, 2 * SUBLANES - POOL_BUF:, :]
    return (y.reshape(nb, t, D_MODEL), conv_o[None, :, SUBLANES - 3:, :], delta_o[None], vrows[None],
            pool_o[None], s5re[None], s5im[None])


def kernel(x_prompt, x_sample, state_conv, state_delta, state_pool, state_s5_re, state_s5_im, p_prompt, p_sample, norm_mix, norm_ffn, norm_pe, norm_final, w_in_ab, conv_qkv, a_log, dt_bias, norm_o, ln_v_gain, ln_v_bias, w_spatial, b_spatial, w_out_ab, w_in_cd, w_pool, pool_scale, lam_re, lam_im, log_dt, b_re, b_im, c_re, c_im, d_skip, w_glu, b_glu, w_out_cd, w_ffn_up, w_ffn_down, w_pe_proj, w_pe_gate):
    w = _prepare_weights(norm_mix, norm_ffn, norm_pe, norm_final, w_in_ab, conv_qkv, a_log, dt_bias, norm_o,
                         ln_v_gain, ln_v_bias, w_spatial, b_spatial, w_out_ab, w_in_cd, w_pool, pool_scale,
                         lam_re, lam_im, log_dt, b_re, b_im, c_re, c_im, d_skip, w_glu, b_glu, w_out_cd,
                         w_ffn_up, w_ffn_down, w_pe_proj, w_pe_gate)
    bp = x_prompt.shape[0]
    zeros = lambda a: jnp.zeros((bp,) + a.shape[2:], F32)
    outs_p = _trunk(x_prompt, p_prompt, zeros(state_conv), zeros(state_delta), zeros(state_pool),
                    zeros(state_s5_re), zeros(state_s5_im), 0, w,
                    ab_tile=(2, 256), cd_tile=(8, 64), tm=512)
    outs_s = _trunk(x_sample, p_sample, state_conv[0], state_delta[0], state_pool[0],
                    state_s5_re[0], state_s5_im[0], PAST_LEN, w,
                    ab_tile=(16, 8), cd_tile=(32, 8), tm=512)
    return (outs_p[0], outs_s[0]) + outs_p[1:] + outs_s[1:]
```

```python
import functools
import math

import jax
import jax.numpy as jnp
from jax import lax
from jax.experimental import pallas as pl
from jax.experimental.pallas import tpu as pltpu

F32 = jnp.float32
BF16 = jnp.bfloat16
HIGHEST = lax.Precision.HIGHEST

EPS = 1e-6
D_MODEL = 1024
PAST_LEN = 16384
GDN_HEADS = 4
GDN_DK = 128
GDN_CONV = 4
D_A = 512
D_QKV = 1536
D_B = 512
SGU_GROUPS = 4
SGU_CHUNK = 128
POOL_WINDOWS = (2, 4, 8, 16)
POOL_BUF = 15
D_C = 512
D_D = 512
S5_GROUPS = 32
S5_STATE = 64
S5_GW = 16
D_FF = 4096
D_PLE = 256
S5_MODES = S5_GROUPS * S5_STATE
S5_HALF = S5_MODES // 2
LANES = 128
SUBLANES = 8
GDN_SUB = 128
S5_SCAN_VREGS = 16
CD_PERM_ROWS = 256
AB_PROJ_ROWS = 256
VMEM_LIMIT_BYTES = 56 * 1024 * 1024


def _dot(a, b):
    return jnp.dot(a, b, preferred_element_type=F32)


def _dot_nt(a, b):
    return lax.dot_general(a, b, (((1,), (1,)), ((), ())), preferred_element_type=F32)


def _dot_exact(a, b):
    return jnp.dot(a, b, precision=HIGHEST, preferred_element_type=F32)


def _rms(x, g):
    return x * lax.rsqrt(jnp.mean(x * x, axis=-1, keepdims=True) + EPS) * g


def _sigmoid(x):
    return 0.5 * jnp.tanh(0.5 * x) + 0.5


def _silu(x):
    return x * _sigmoid(x)


def _gelu_tanh(x):
    return 0.5 * x * (1.0 + jnp.tanh(math.sqrt(2.0 / math.pi) * (x + 0.044715 * (x * x * x))))


def _softplus(x):
    return jnp.maximum(x, 0.0) + jnp.log(1.0 + jnp.exp(-jnp.abs(x)))


def _log2(n):
    l = n.bit_length() - 1
    assert (1 << l) == n, n
    return l


def _ab_body(*refs, nseq, tq, c, cs, has_state):
    h_ref, refs = refs[0], refs[1:]
    if has_state:
        (conv0_ref, delta0_ref), refs = refs[:2], refs[2:]
    (gmix_ref, win_ref, wba_ref, convw_ref, gpar_ref, normo_ref, lng_ref, lnb_ref, wsp_ref, bsp_ref,
     wout_ref, hout_ref, conv_ref, delta_ref, vrows_ref) = refs
    R = nseq * tq
    t_idx = pl.program_id(1)

    @pl.when(t_idx == 0)
    def _init():
        conv_ref[...] = jnp.zeros(conv_ref.shape, F32)
        if has_state:
            conv_ref[:, SUBLANES - (GDN_CONV - 1):, :] = conv0_ref[...]
            delta_ref[...] = delta0_ref[...]
        else:
            delta_ref[...] = jnp.zeros(delta_ref.shape, F32)

    cat = lambda parts, axis: parts[0] if len(parts) == 1 else jnp.concatenate(parts, axis=axis)
    pr = min(R, AB_PROJ_ROWS)
    spp = pr // tq
    cw = convw_ref[...]
    z_parts, zba_parts, qkv_parts = [], [], []
    for i in range(R // pr):
        xn = _rms(h_ref[i * spp:(i + 1) * spp].reshape(pr, D_MODEL), gmix_ref[...]).astype(BF16)
        zi = _dot(xn, win_ref[...])
        z_parts.append(zi)
        zba_parts.append(_dot(xn, wba_ref[...]))
        pre = zi[:, :D_QKV].reshape(spp, tq, D_QKV)
        xext = jnp.concatenate([conv_ref[i * spp:(i + 1) * spp], pre], axis=1)
        acc = pre * cw[3:4, :]
        for k in range(1, 4):
            acc = acc + pltpu.roll(xext, k, axis=1)[:, SUBLANES:, :] * cw[3 - k:4 - k, :]
        conv_ref[i * spp:(i + 1) * spp] = xext[:, tq:, :]
        qkv_parts.append(_silu(acc).reshape(pr, D_QKV))
    z = cat(z_parts, 0)
    zba = cat(zba_parts, 0)
    qkv = cat(qkv_parts, 0)

    gpar = gpar_ref[...]
    beta_all = _sigmoid(zba)
    g_all = -jnp.exp(gpar[0:1, :]) * _softplus(zba + gpar[1:2, :])

    sub = GDN_SUB
    log2c = _log2(c)
    row = lax.broadcasted_iota(jnp.int32, (sub, sub), 0)
    col = lax.broadcasted_iota(jnp.int32, (sub, sub), 1)
    cblk = lax.shift_right_logical(col, log2c)
    same = lax.shift_right_logical(row, log2c) == cblk
    tril = same & (row >= col)
    strict = same & (row > col)
    tril_f = tril.astype(F32)
    eye = (row == col).astype(F32)
    cblk_row = cblk[0:1, :]
    nblk = sub // c
    blk_per_seq = tq // c
    nsub = R // sub
    pairs = [(st, h) for st in range(nsub) for h in range(GDN_HEADS)]
    gc_alls, glasts, grems, gc_ts = [], [], [], []
    for st in range(nsub):
        gc_all = _dot_exact(tril_f, g_all[st * sub:(st + 1) * sub])
        glast = [gc_all[(j + 1) * c - 1:(j + 1) * c, :] for j in range(nblk)]
        gc_alls.append(gc_all)
        glasts.append(glast)
        grems.append(cat([jnp.broadcast_to(gl, (c, LANES)) for gl in glast], 0) - gc_all)
        gc_ts.append(gc_all.T)
    qs, ks, vs, betas, egs, decays, qkks = {}, {}, {}, {}, {}, {}, {}
    for p in pairs:
        st, h = p
        r0, lo = st * sub, h * GDN_DK
        q = qkv[r0:r0 + sub, lo:lo + GDN_DK]
        k = qkv[r0:r0 + sub, D_A + lo:D_A + lo + GDN_DK]
        qs[p] = q * lax.rsqrt(jnp.sum(q * q, axis=-1, keepdims=True) + EPS) * (GDN_DK ** -0.5)
        ks[p] = k * lax.rsqrt(jnp.sum(k * k, axis=-1, keepdims=True) + EPS)
        vs[p] = qkv[r0:r0 + sub, 2 * D_A + lo:2 * D_A + lo + GDN_DK]
        betas[p] = beta_all[r0:r0 + sub, h:h + 1]
        gc = gc_alls[st][:, 4 + h:5 + h]
        egs[p] = jnp.exp(gc)
        decays[p] = jnp.where(tril, jnp.exp(jnp.where(tril, gc - gc_ts[st][4 + h:5 + h, :], 0.0)), 0.0)
    for p in pairs:
        kb = ks[p].astype(BF16)
        qkks[p] = _dot_nt(jnp.concatenate([qs[p].astype(BF16), kb], axis=0), kb)
    xks, tinvs, qkds = {}, {}, {}
    for p in pairs:
        qkds[p] = (qkks[p][:sub] * decays[p]).astype(BF16)
        xks[p] = jnp.where(strict, -(betas[p] * qkks[p][sub:] * decays[p]), 0.0)
        tinvs[p] = eye + xks[p]
    if log2c > 1:
        for p in pairs:
            xb = xks[p].astype(BF16)
            xks[p] = _dot(xb, xb)
    for lev in range(1, log2c):
        res = {}
        for p in pairs:
            xb = xks[p].astype(BF16)
            if lev < log2c - 1:
                res[p] = _dot(jnp.concatenate([xb, tinvs[p].astype(BF16)], axis=0), xb)
            else:
                res[p] = _dot(tinvs[p].astype(BF16), xb)
        for p in pairs:
            if lev < log2c - 1:
                xks[p] = res[p][:sub]
                tinvs[p] = tinvs[p] + res[p][sub:]
            else:
                tinvs[p] = tinvs[p] + res[p]
    uws, q_heads, kts = {}, {}, {}
    for p in pairs:
        rhs = jnp.concatenate([(vs[p] * betas[p]).astype(BF16),
                               (ks[p] * (betas[p] * egs[p])).astype(BF16)], axis=1)
        uws[p] = _dot(tinvs[p].astype(BF16), rhs)
        q_heads[p] = qs[p] * egs[p]
        st, h = p
        kts[p] = (ks[p] * jnp.exp(grems[st][:, 4 + h:5 + h])).T
    u_lists = {p: [] for p in pairs}
    qs_lists = {p: [] for p in pairs}
    for kstep in range(blk_per_seq):
        items = []
        for seq in range(nseq):
            st, j = divmod((seq * tq + kstep * c) // c, nblk)
            items += [(seq, st, j, h) for h in range(GDN_HEADS)]
        s_olds, tmps = {}, {}
        for it in items:
            seq, st, j, h = it
            p = (st, h)
            sl = slice(j * c, (j + 1) * c)
            s_olds[it] = delta_ref[seq, h]
            both = jnp.concatenate([uws[p][sl, GDN_DK:], q_heads[p][sl]], axis=0).astype(BF16)
            tmps[it] = _dot(both, s_olds[it].astype(BF16))
        for it in items:
            seq, st, j, h = it
            p = (st, h)
            sl = slice(j * c, (j + 1) * c)
            u_base = uws[p][:, :GDN_DK]
            u_lists[p].append(u_base[sl] - tmps[it][:c])
            qs_lists[p].append(tmps[it][c:])
            u_full = cat(u_lists[p] + ([u_base[(j + 1) * c:]] if j + 1 < nblk else []), 0)
            kt_j = jnp.where(cblk_row == j, kts[p], 0.0).astype(BF16)
            delta_ref[seq, h] = (s_olds[it] * jnp.exp(glasts[st][j][:, 4 + h:5 + h])
                                 + _dot(kt_j, u_full.astype(BF16)))
    o_raw = {}
    for p in pairs:
        o_raw[p] = cat(qs_lists[p], 0) + _dot(qkds[p], cat(u_lists[p], 0).astype(BF16))
    o_tiles = []
    for st in range(nsub):
        o_heads = []
        for h in range(GDN_HEADS):
            zg = z[st * sub:(st + 1) * sub, D_QKV + h * GDN_DK:D_QKV + (h + 1) * GDN_DK]
            o_heads.append(_rms(o_raw[(st, h)], normo_ref[...]) * _silu(zg))
        o_tiles.append(jnp.concatenate(o_heads, axis=1))
    o_a = cat(o_tiles, 0)

    zu = z[:, D_QKV + D_A:D_QKV + D_A + D_B]
    zv = z[:, D_QKV + D_A + D_B:]
    u_act = _gelu_tanh(zu)
    gv = _gelu_tanh(zv)
    vc = gv - jnp.mean(gv, axis=-1, keepdims=True)
    var = jnp.mean(vc * vc, axis=-1, keepdims=True)
    v_ln = vc * lax.rsqrt(var + EPS) * lng_ref[...] + lnb_ref[...]
    rv = vrows_ref.shape[1]
    vrows_ref[...] = v_ln.reshape(nseq, tq, D_B)[:, tq - rv:, :]
    log2cs = _log2(cs)
    srow = lax.broadcasted_iota(jnp.int32, (SGU_CHUNK, SGU_CHUNK), 0)
    scol = lax.broadcasted_iota(jnp.int32, (SGU_CHUNK, SGU_CHUNK), 1)
    smask = (lax.shift_right_logical(srow, log2cs) == lax.shift_right_logical(scol, log2cs)) & (srow >= scol)
    v_bf = v_ln.astype(BF16)
    mixed_groups = []
    for g in range(SGU_GROUPS):
        wg = jnp.where(smask, wsp_ref[g], 0.0).astype(BF16)
        bg = bsp_ref[g]
        parts = []
        for r0 in range(0, R, SGU_CHUNK):
            parts.append(_dot(wg, v_bf[r0:r0 + SGU_CHUNK, g * LANES:(g + 1) * LANES]) + bg)
        mixed_groups.append(parts[0] if len(parts) == 1 else jnp.concatenate(parts, axis=0))
    o_b = u_act * jnp.concatenate(mixed_groups, axis=1)

    y_in = jnp.concatenate([o_a, o_b], axis=1).astype(BF16)
    for i in range(R // pr):
        seqs = slice(i * spp, (i + 1) * spp)
        proj = _dot(y_in[i * pr:(i + 1) * pr], wout_ref[...])
        hout_ref[seqs] = h_ref[seqs] + proj.reshape(spp, tq, D_MODEL)


def _ab_layer(h, states, w, *, nseq, tq, c, cs):
    nb, t, _ = h.shape
    has_state = states is not None
    grid = (nb // nseq, t // tq)
    rv = min(SGU_CHUNK, tq)
    seq_blk = lambda shape: pl.BlockSpec(shape, lambda b, s: (b,) + (0,) * (len(shape) - 1))
    full = lambda a: pl.BlockSpec(a.shape, lambda b, s: (0,) * a.ndim)
    params = [w['gmix0'], w['win_ab'], w['wba'], w['convw'], w['gpar'], w['normo'], w['lng'], w['lnb'],
              w['wsp'], w['bsp'], w['wout_ab']]
    out_shapes = (
        jax.ShapeDtypeStruct(h.shape, F32),
        jax.ShapeDtypeStruct((nb, SUBLANES, D_QKV), F32),
        jax.ShapeDtypeStruct((nb, GDN_HEADS, GDN_DK, GDN_DK), F32),
        jax.ShapeDtypeStruct((nb, rv, D_B), F32),
    )
    return pl.pallas_call(
        functools.partial(_ab_body, nseq=nseq, tq=tq, c=c, cs=cs, has_state=has_state),
        grid=grid,
        in_specs=[pl.BlockSpec((nseq, tq, D_MODEL), lambda b, s: (b, s, 0))]
        + ([seq_blk((nseq, GDN_CONV - 1, D_QKV)), seq_blk((nseq, GDN_HEADS, GDN_DK, GDN_DK))]
           if has_state else []) + [full(a) for a in params],
        out_specs=(pl.BlockSpec((nseq, tq, D_MODEL), lambda b, s: (b, s, 0)),
                   seq_blk((nseq, SUBLANES, D_QKV)),
                   seq_blk((nseq, GDN_HEADS, GDN_DK, GDN_DK)),
                   seq_blk((nseq, rv, D_B))),
        out_shape=out_shapes,
        compiler_params=pltpu.CompilerParams(
            dimension_semantics=("arbitrary", "arbitrary"), vmem_limit_bytes=VMEM_LIMIT_BYTES),
        name="ab_mixer",
    )(h, *(states if has_state else ()), *params)


def _cd_body(*refs, nseq, tq, pos0, has_state):
    h_ref, refs = refs[0], refs[1:]
    if has_state:
        (pool0_ref, s5re0_ref, s5im0_ref), refs = refs[:3], refs[3:]
    (gmix_ref, win_ref, wpool_ref, pscale_ref, wb_ref, lam_re_ref, lam_im_ref, wcre_ref, wcim_ref,
     dskip_ref, wglu_ref, bglu_ref, wout_ref, hout_ref, pool_ref, s5re_ref, s5im_ref, bu_ref) = refs
    R = nseq * tq
    t_idx = pl.program_id(1)

    @pl.when(t_idx == 0)
    def _init():
        if has_state:
            pool_ref[...] = pool0_ref[...]
            s5re_ref[...] = s5re0_ref[...]
            s5im_ref[...] = s5im0_ref[...]
        else:
            pool_ref[...] = jnp.zeros(pool_ref.shape, F32)
            s5re_ref[...] = jnp.zeros(s5re_ref.shape, F32)
            s5im_ref[...] = jnp.zeros(s5im_ref.shape, F32)

    x3 = h_ref[...]
    xn3 = _rms(x3, gmix_ref[...]).astype(BF16)
    nper = R // CD_PERM_ROWS
    tsub = tq // nper
    log2n = _log2(nseq)
    ri = lax.broadcasted_iota(jnp.int32, (CD_PERM_ROWS, CD_PERM_ROWS), 0)
    ci = lax.broadcasted_iota(jnp.int32, (CD_PERM_ROWS, CD_PERM_ROWS), 1)
    to_tm = (ci == (ri & (nseq - 1)) * tsub + lax.shift_right_logical(ri, log2n)).astype(BF16)
    to_sm = (ri == (ci & (nseq - 1)) * tsub + lax.shift_right_logical(ci, log2n)).astype(BF16)
    xt = jnp.concatenate(
        [_dot(to_tm, xn3[:, i * tsub:(i + 1) * tsub, :].reshape(CD_PERM_ROWS, D_MODEL)).astype(BF16)
         for i in range(nper)], axis=0)
    z = _dot(xt, win_ref[...])
    xc = z[:, :D_C].reshape(tq, nseq, D_C)
    xd = z[:, D_C:]

    xext = jnp.concatenate([pool_ref[...], xc], axis=0)
    pool_ref[...] = xext[tq:]
    pos = pos0 + t_idx * tq + lax.broadcasted_iota(jnp.int32, (tq, 1, 1), 0)
    sums = xext
    outs = []
    for gi, win in enumerate(POOL_WINDOWS):
        sums = sums[win // 2:] + sums[:-(win // 2)]
        cnt = jnp.minimum(win, pos + 1).astype(F32)
        sl = slice(gi * LANES, (gi + 1) * LANES)
        m = sums[sums.shape[0] - tq:, :, sl] / cnt - xc[:, :, sl]
        outs.append(_dot(m.reshape(R, LANES).astype(BF16), wpool_ref[gi]))
    y_c = jnp.concatenate(outs, axis=1) * pscale_ref[...]

    xd_bf = xd.astype(BF16)
    for half in range(2):
        res = _dot(xd_bf[:, half * 256:(half + 1) * 256], wb_ref[half])
        bu_ref[:, half * S5_HALF:(half + 1) * S5_HALF] = res[:, :S5_HALF]
        bu_ref[:, S5_MODES + half * S5_HALF:S5_MODES + (half + 1) * S5_HALF] = res[:, S5_HALF:]
    piece = S5_SCAN_VREGS * SUBLANES * LANES // (2 * nseq)
    for p0 in range(0, S5_MODES, piece):
        ms = slice(p0, p0 + piece)
        lr = jnp.broadcast_to(lam_re_ref[:, ms], (nseq, piece))
        li = jnp.broadcast_to(lam_im_ref[:, ms], (nseq, piece))

        def step(t, carry, p0=p0, lr=lr, li=li):
            re, im = carry
            r0 = pl.multiple_of(t * nseq, nseq)
            nre = lr * re - li * im + bu_ref[pl.ds(r0, nseq), p0:p0 + piece]
            nim = lr * im + li * re + bu_ref[pl.ds(r0, nseq), S5_MODES + p0:S5_MODES + p0 + piece]
            bu_ref[pl.ds(r0, nseq), p0:p0 + piece] = nre
            bu_ref[pl.ds(r0, nseq), S5_MODES + p0:S5_MODES + p0 + piece] = nim
            return nre, nim

        re, im = lax.fori_loop(0, tq, step, (s5re_ref[:, ms], s5im_ref[:, ms]), unroll=4)
        s5re_ref[:, ms] = re
        s5im_ref[:, ms] = im

    ys = []
    kn = wcre_ref.shape[1]
    for n in range(wcre_ref.shape[0]):
        s_re = bu_ref[:, n * kn:(n + 1) * kn].astype(BF16)
        s_im = bu_ref[:, S5_MODES + n * kn:S5_MODES + (n + 1) * kn].astype(BF16)
        ys.append(_dot(s_re, wcre_ref[n]) - _dot(s_im, wcim_ref[n]))
    y = jnp.concatenate(ys, axis=1) + dskip_ref[...] * xd
    y = _gelu_tanh(y)
    y_d = y * _sigmoid(_dot(y.astype(BF16), wglu_ref[...]) + bglu_ref[...])

    y_in = jnp.concatenate([y_c, y_d], axis=1).astype(BF16)
    y_sm = jnp.concatenate(
        [_dot(to_sm, y_in[i * CD_PERM_ROWS:(i + 1) * CD_PERM_ROWS]).astype(BF16) for i in range(nper)], axis=0)
    proj = _dot(y_sm, wout_ref[...])
    for i in range(nper):
        ts = slice(i * tsub, (i + 1) * tsub)
        blk = proj[i * CD_PERM_ROWS:(i + 1) * CD_PERM_ROWS].reshape(nseq, tsub, D_MODEL)
        hout_ref[:, ts, :] = x3[:, ts, :] + blk


def _cd_layer(h, states, w, *, nseq, tq, pos0):
    nb, t, _ = h.shape
    grid = (nb // nseq, t // tq)
    full = lambda a: pl.BlockSpec(a.shape, lambda b, s: (0,) * a.ndim)
    params = [w['gmix1'], w['win_cd'], w['wpool'], w['pscale'], w['wb'], w['lam_re'], w['lam_im'],
              w['wcre'], w['wcim'], w['dskip'], w['wglu'], w['bglu'], w['wout_cd']]
    h_spec = pl.BlockSpec((nseq, tq, D_MODEL), lambda b, s: (b, s, 0))
    pool_spec = pl.BlockSpec((2 * SUBLANES, nseq, D_C), lambda b, s: (0, b, 0))
    s5_spec = pl.BlockSpec((nseq, S5_MODES), lambda b, s: (b, 0))
    state_specs = [pool_spec, s5_spec, s5_spec]
    has_state = states is not None
    out_shapes = (
        jax.ShapeDtypeStruct(h.shape, F32),
        jax.ShapeDtypeStruct((2 * SUBLANES, nb, D_C), F32),
        jax.ShapeDtypeStruct((nb, S5_MODES), F32),
        jax.ShapeDtypeStruct((nb, S5_MODES), F32),
    )
    return pl.pallas_call(
        functools.partial(_cd_body, nseq=nseq, tq=tq, pos0=pos0, has_state=has_state),
        grid=grid,
        in_specs=[h_spec] + (state_specs if has_state else []) + [full(a) for a in params],
        out_specs=tuple([h_spec] + state_specs),
        out_shape=out_shapes,
        scratch_shapes=[pltpu.VMEM((nseq * tq, 2 * S5_MODES), F32)],
        compiler_params=pltpu.CompilerParams(
            dimension_semantics=("arbitrary", "arbitrary"), vmem_limit_bytes=VMEM_LIMIT_BYTES),
        name="cd_mixer",
    )(h, *(states if has_state else ()), *params)


def _ffn_pe_body(h_ref, p_ref, gffn_ref, wup_ref, wdn_ref, gpe_ref, wgate_ref, wproj_ref, gfin_ref,
                 out_ref, *, final):
    x = h_ref[...]
    xn = _rms(x, gffn_ref[...]).astype(BF16)
    h = x
    ffc = D_FF // 4
    for ci in range(4):
        a = jnp.maximum(_dot(xn, wup_ref[:, ci * ffc:(ci + 1) * ffc]), 0.0)
        h = h + _dot((a * a).astype(BF16), wdn_ref[ci * ffc:(ci + 1) * ffc, :])
    hn = _rms(h, gpe_ref[...]).astype(BF16)
    gate = _sigmoid(_dot(hn, wgate_ref[...]))
    h = h + _dot(p_ref[...].astype(BF16), wproj_ref[...]) * gate
    if final:
        h = _rms(h, gfin_ref[...])
    out_ref[...] = h


def _ffn_pe(h2, p3, w, layer, *, tm, final):
    n = h2.shape[0]
    of_layer = lambda a: pl.BlockSpec((None,) + a.shape[1:], lambda i: (layer,) + (0,) * (a.ndim - 1))
    params = [w['gffn'], w['wup'], w['wdn'], w['gpe'], w['wgate'], w['wproj']]
    return pl.pallas_call(
        functools.partial(_ffn_pe_body, final=final),
        grid=(n // tm,),
        in_specs=[pl.BlockSpec((tm, D_MODEL), lambda i: (i, 0)),
                  pl.BlockSpec((None, tm, D_PLE), lambda i: (layer, i, 0))]
        + [of_layer(a) for a in params] + [pl.BlockSpec(w['gfin'].shape, lambda i: (0, 0))],
        out_specs=pl.BlockSpec((tm, D_MODEL), lambda i: (i, 0)),
        out_shape=jax.ShapeDtypeStruct(h2.shape, F32),
        compiler_params=pltpu.CompilerParams(
            dimension_semantics=("arbitrary",), vmem_limit_bytes=VMEM_LIMIT_BYTES),
        name="ffn_pe",
    )(h2, p3, *params, w['gfin'])


def _s5_prep_body(lre_ref, lim_ref, ldt_ref, bre_ref, bim_ref, lbre_ref, lbim_ref, bbre_ref, bbim_ref):
    lr = lre_ref[...]
    li = lim_ref[...]
    dt = jnp.exp(ldt_ref[...])
    mag = jnp.exp(lr * dt)
    ang = li * dt
    lb_re = mag * jnp.cos(ang)
    lb_im = mag * jnp.sin(ang)
    lbre_ref[...] = lb_re
    lbim_ref[...] = lb_im
    a = lb_re - 1.0
    b = lb_im
    den = lr * lr + li * li
    cr = (a * lr + b * li) / den
    ci = (b * lr - a * li) / den
    bre = bre_ref[...]
    bim = bim_ref[...]
    bbre_ref[...] = cr * bre - ci * bim
    bbim_ref[...] = cr * bim + ci * bre


def _s5_prep(lam_re, lam_im, log_dt, b_re, b_im):
    ldt = jnp.broadcast_to(log_dt[:, None], (S5_GROUPS, S5_STATE))
    rows = [a.reshape(1, S5_MODES) for a in (lam_re, lam_im, ldt)]
    chan_major = lambda a: a.reshape(S5_MODES, S5_GW).T
    shp = jax.ShapeDtypeStruct((S5_GW, S5_MODES), F32)
    row_shp = jax.ShapeDtypeStruct((1, S5_MODES), F32)
    lb_re, lb_im, bb_re, bb_im = pl.pallas_call(
        _s5_prep_body, out_shape=(row_shp, row_shp, shp, shp), name="s5_prep",
    )(*rows, chan_major(b_re), chan_major(b_im))
    to_gcn = lambda a: jnp.swapaxes(a.reshape(S5_GW, S5_GROUPS, S5_STATE), 0, 1)
    return lb_re, lb_im, to_gcn(bb_re), to_gcn(bb_im)


def _block_diag(a, ngrp):
    p, q = a.shape[1], a.shape[2]
    eye = jnp.eye(ngrp, dtype=a.dtype)
    a = a.reshape(S5_GROUPS // ngrp, ngrp, p, q)
    return jnp.einsum('hgpq,gk->hgpkq', a, eye).reshape(S5_GROUPS // ngrp, ngrp * p, ngrp * q)


def _prepare_weights(norm_mix, norm_ffn, norm_pe, norm_final, w_in_ab, conv_qkv, a_log, dt_bias, norm_o,
                     ln_v_gain, ln_v_bias, w_spatial, b_spatial, w_out_ab, w_in_cd, w_pool, pool_scale,
                     lam_re, lam_im, log_dt, b_re, b_im, c_re, c_im, d_skip, w_glu, b_glu, w_out_cd,
                     w_ffn_up, w_ffn_down, w_pe_proj, w_pe_gate):
    row = lambda a: a.reshape(1, -1)
    w = {}
    w['gmix0'] = row(norm_mix[0])
    w['gmix1'] = row(norm_mix[1])
    wi = w_in_ab[0]
    cut0 = D_QKV + D_A
    cut1 = cut0 + 2 * GDN_HEADS
    w['win_ab'] = jnp.concatenate([wi[:, :cut0], wi[:, cut1:]], axis=1).astype(BF16)
    w['wba'] = jnp.pad(wi[:, cut0:cut1], ((0, 0), (0, LANES - 2 * GDN_HEADS))).astype(BF16)
    w['convw'] = jnp.pad(conv_qkv[0], ((0, SUBLANES - 4), (0, 0)))
    gp = jnp.zeros((SUBLANES, LANES), F32)
    gp = gp.at[0, GDN_HEADS:2 * GDN_HEADS].set(a_log[0]).at[1, GDN_HEADS:2 * GDN_HEADS].set(dt_bias[0])
    w['gpar'] = gp
    w['normo'] = row(norm_o[0])
    w['lng'] = row(ln_v_gain[0])
    w['lnb'] = row(ln_v_bias[0])
    w['wout_ab'] = w_out_ab[0].astype(BF16)
    w['w_spatial'] = w_spatial[0]
    w['b_spatial'] = b_spatial[0]
    w['win_cd'] = w_in_cd[0].astype(BF16)
    w['wpool'] = w_pool[0].astype(BF16)
    w['pscale'] = row(pool_scale[0])
    lb_re, lb_im, bb_re, bb_im = _s5_prep(lam_re[0], lam_im[0], log_dt[0], b_re[0], b_im[0])
    w['lam_re'] = lb_re
    w['lam_im'] = lb_im
    half = S5_GROUPS // 2
    w['wb'] = jnp.concatenate([_block_diag(bb_re, half), _block_diag(bb_im, half)],
                              axis=2).astype(BF16)
    per_tile = LANES // S5_GW
    w['wcre'] = _block_diag(jnp.swapaxes(c_re[0], 1, 2), per_tile).astype(BF16)
    w['wcim'] = _block_diag(jnp.swapaxes(c_im[0], 1, 2), per_tile).astype(BF16)
    w['dskip'] = row(d_skip[0])
    w['wglu'] = w_glu[0].astype(BF16)
    w['bglu'] = row(b_glu[0])
    w['wout_cd'] = w_out_cd[0].astype(BF16)
    w['gffn'] = norm_ffn.reshape(2, 1, D_MODEL)
    w['gpe'] = norm_pe.reshape(2, 1, D_MODEL)
    w['gfin'] = row(norm_final)
    w['wup'] = w_ffn_up.astype(BF16)
    w['wdn'] = w_ffn_down.astype(BF16)
    w['wgate'] = w_pe_gate.astype(BF16)
    w['wproj'] = w_pe_proj.astype(BF16)
    return w


def _trunk(x, p, states, pos0, w, *, ab_tile, cd_tile, tm):
    nb, t, _ = x.shape
    nseq, tq = ab_tile
    c = min(64, tq)
    cs = min(SGU_CHUNK, tq)
    reps = SGU_CHUNK // cs
    wl = dict(w)
    wl['wsp'] = jnp.tile(w['w_spatial'][:, :cs, :cs], (1, reps, reps))
    wl['bsp'] = jnp.tile(w['b_spatial'][:, :cs], (1, reps))[:, :, None]
    ab_states = cd_states = None
    if states is not None:
        conv0, delta0, pool0, s5re0, s5im0 = states
        ab_states = (conv0, delta0)
        pool_in = jnp.swapaxes(jnp.pad(pool0, ((0, 0), (2 * SUBLANES - POOL_BUF, 0), (0, 0))), 0, 1)
        cd_states = (pool_in, s5re0.reshape(nb, S5_MODES), s5im0.reshape(nb, S5_MODES))
    h, conv_o, delta_o, vrows = _ab_layer(x, ab_states, wl, nseq=nseq, tq=tq, c=c, cs=cs)
    p3 = p.reshape(p.shape[0], nb * t, D_PLE)
    h = _ffn_pe(h.reshape(nb * t, D_MODEL), p3, w, 0, tm=tm, final=False)
    h, pool_o, s5re, s5im = _cd_layer(h.reshape(nb, t, D_MODEL), cd_states, w,
                                      nseq=cd_tile[0], tq=cd_tile[1], pos0=pos0)
    y = _ffn_pe(h.reshape(nb * t, D_MODEL), p3, w, 1, tm=tm, final=True)
    pool_o = jnp.swapaxes(pool_o, 0, 1)[:, 2 * SUBLANES - POOL_BUF:, :]
    return (y.reshape(nb, t, D_MODEL), conv_o[None, :, SUBLANES - (GDN_CONV - 1):, :], delta_o[None],
            vrows[None], pool_o[None], s5re.reshape(1, nb, S5_GROUPS, S5_STATE),
            s5im.reshape(1, nb, S5_GROUPS, S5_STATE))


def kernel(x_prompt, x_sample, state_conv, state_delta, state_pool, state_s5_re, state_s5_im, p_prompt, p_sample, norm_mix, norm_ffn, norm_pe, norm_final, w_in_ab, conv_qkv, a_log, dt_bias, norm_o, ln_v_gain, ln_v_bias, w_spatial, b_spatial, w_out_ab, w_in_cd, w_pool, pool_scale, lam_re, lam_im, log_dt, b_re, b_im, c_re, c_im, d_skip, w_glu, b_glu, w_out_cd, w_ffn_up, w_ffn_down, w_pe_proj, w_pe_gate):
    w = _prepare_weights(norm_mix, norm_ffn, norm_pe, norm_final, w_in_ab, conv_qkv, a_log, dt_bias, norm_o,
                         ln_v_gain, ln_v_bias, w_spatial, b_spatial, w_out_ab, w_in_cd, w_pool, pool_scale,
                         lam_re, lam_im, log_dt, b_re, b_im, c_re, c_im, d_skip, w_glu, b_glu, w_out_cd,
                         w_ffn_up, w_ffn_down, w_pe_proj, w_pe_gate)
    outs_p = _trunk(x_prompt, p_prompt, None, 0, w, ab_tile=(2, 256), cd_tile=(8, 64), tm=512)
    states = (state_conv[0], state_delta[0], state_pool[0], state_s5_re[0], state_s5_im[0])
    outs_s = _trunk(x_sample, p_sample, states, PAST_LEN, w, ab_tile=(16, 8), cd_tile=(32, 8), tm=512)
    return (outs_p[0], outs_s[0]) + outs_p[1:] + outs_s[1:]
```

```python
import functools
import math

import jax
import jax.numpy as jnp
from jax import lax
from jax.experimental import pallas as pl
from jax.experimental.pallas import tpu as pltpu

F32 = jnp.float32
BF16 = jnp.bfloat16
HIGHEST = lax.Precision.HIGHEST

EPS = 1e-6
D_MODEL = 1024
PAST_LEN = 16384
GDN_HEADS = 4
GDN_DK = 128
GDN_CONV = 4
D_A = 512
D_QKV = 1536
D_B = 512
SGU_GROUPS = 4
SGU_CHUNK = 128
POOL_WINDOWS = (2, 4, 8, 16)
POOL_BUF = 15
D_C = 512
D_D = 512
S5_GROUPS = 32
S5_STATE = 64
S5_GW = 16
D_FF = 4096
D_PLE = 256
S5_MODES = S5_GROUPS * S5_STATE
S5_HALF = S5_MODES // 2
LANES = 128
SUBLANES = 8
GDN_SUB = 128
S5_SCAN_VREGS = 16
CD_PERM_ROWS = 256
AB_PROJ_ROWS = 256
VMEM_LIMIT_BYTES = 56 * 1024 * 1024


def _dot(a, b):
    return jnp.dot(a, b, preferred_element_type=F32)


def _dot_nt(a, b):
    return lax.dot_general(a, b, (((1,), (1,)), ((), ())), preferred_element_type=F32)


def _dot_exact(a, b):
    return jnp.dot(a, b, precision=HIGHEST, preferred_element_type=F32)


def _rms(x, g):
    return x * lax.rsqrt(jnp.mean(x * x, axis=-1, keepdims=True) + EPS) * g


def _sigmoid(x):
    return 0.5 * jnp.tanh(0.5 * x) + 0.5


def _silu(x):
    return x * _sigmoid(x)


def _gelu_tanh(x):
    return 0.5 * x * (1.0 + jnp.tanh(math.sqrt(2.0 / math.pi) * (x + 0.044715 * (x * x * x))))


def _softplus(x):
    return jnp.maximum(x, 0.0) + jnp.log(1.0 + jnp.exp(-jnp.abs(x)))


def _log2(n):
    l = n.bit_length() - 1
    assert (1 << l) == n, n
    return l


def _ab_body(*refs, nseq, tq, c, cs, has_state):
    h_ref, refs = refs[0], refs[1:]
    if has_state:
        (conv0_ref, delta0_ref), refs = refs[:2], refs[2:]
    (gmix_ref, win_ref, wba_ref, convw_ref, gpar_ref, normo_ref, lng_ref, lnb_ref, wsp_ref, bsp_ref,
     wout_ref, hout_ref, conv_ref, delta_ref, vrows_ref) = refs
    R = nseq * tq
    t_idx = pl.program_id(1)

    @pl.when(t_idx == 0)
    def _init():
        conv_ref[...] = jnp.zeros(conv_ref.shape, F32)
        if has_state:
            conv_ref[:, SUBLANES - (GDN_CONV - 1):, :] = conv0_ref[...]
            delta_ref[...] = delta0_ref[...]
        else:
            delta_ref[...] = jnp.zeros(delta_ref.shape, F32)

    cat = lambda parts, axis: parts[0] if len(parts) == 1 else jnp.concatenate(parts, axis=axis)
    pr = min(R, AB_PROJ_ROWS)
    spp = pr // tq
    cw = convw_ref[...]
    z_parts, zba_parts, qkv_parts = [], [], []
    for i in range(R // pr):
        xn = _rms(h_ref[i * spp:(i + 1) * spp].reshape(pr, D_MODEL), gmix_ref[...]).astype(BF16)
        zi = _dot(xn, win_ref[...])
        z_parts.append(zi)
        zba_parts.append(_dot(xn, wba_ref[...]))
        pre = zi[:, :D_QKV].reshape(spp, tq, D_QKV)
        xext = jnp.concatenate([conv_ref[i * spp:(i + 1) * spp], pre], axis=1)
        acc = pre * cw[3:4, :]
        for k in range(1, 4):
            acc = acc + pltpu.roll(xext, k, axis=1)[:, SUBLANES:, :] * cw[3 - k:4 - k, :]
        conv_ref[i * spp:(i + 1) * spp] = xext[:, tq:, :]
        qkv_parts.append(_silu(acc).reshape(pr, D_QKV))
    z = cat(z_parts, 0)
    zba = cat(zba_parts, 0)
    qkv = cat(qkv_parts, 0)

    gpar = gpar_ref[...]
    beta_all = _sigmoid(zba)
    g_all = -jnp.exp(gpar[0:1, :]) * _softplus(zba + gpar[1:2, :])

    sub = GDN_SUB
    log2c = _log2(c)
    row = lax.broadcasted_iota(jnp.int32, (sub, sub), 0)
    col = lax.broadcasted_iota(jnp.int32, (sub, sub), 1)
    cblk = lax.shift_right_logical(col, log2c)
    same = lax.shift_right_logical(row, log2c) == cblk
    tril = same & (row >= col)
    strict = same & (row > col)
    tril_f = tril.astype(F32)
    eye = (row == col).astype(F32)
    cblk_row = cblk[0:1, :]
    nblk = sub // c
    blk_per_seq = tq // c
    nsub = R // sub
    pairs = [(st, h) for st in range(nsub) for h in range(GDN_HEADS)]
    gc_alls, glasts, grems, gc_ts = [], [], [], []
    for st in range(nsub):
        gc_all = _dot_exact(tril_f, g_all[st * sub:(st + 1) * sub])
        glast = [gc_all[(j + 1) * c - 1:(j + 1) * c, :] for j in range(nblk)]
        gc_alls.append(gc_all)
        glasts.append(glast)
        grems.append(cat([jnp.broadcast_to(gl, (c, LANES)) for gl in glast], 0) - gc_all)
        gc_ts.append(gc_all.T)
    qs, ks, vs, betas, egs, decays, qkks = {}, {}, {}, {}, {}, {}, {}
    for p in pairs:
        st, h = p
        r0, lo = st * sub, h * GDN_DK
        q = qkv[r0:r0 + sub, lo:lo + GDN_DK]
        k = qkv[r0:r0 + sub, D_A + lo:D_A + lo + GDN_DK]
        qs[p] = q * lax.rsqrt(jnp.sum(q * q, axis=-1, keepdims=True) + EPS) * (GDN_DK ** -0.5)
        ks[p] = k * lax.rsqrt(jnp.sum(k * k, axis=-1, keepdims=True) + EPS)
        vs[p] = qkv[r0:r0 + sub, 2 * D_A + lo:2 * D_A + lo + GDN_DK]
        betas[p] = beta_all[r0:r0 + sub, h:h + 1]
        gc = gc_alls[st][:, 4 + h:5 + h]
        egs[p] = jnp.exp(gc)
        decays[p] = jnp.where(tril, jnp.exp(jnp.where(tril, gc - gc_ts[st][4 + h:5 + h, :], 0.0)), 0.0)
    for p in pairs:
        kb = ks[p].astype(BF16)
        qkks[p] = _dot_nt(jnp.concatenate([qs[p].astype(BF16), kb], axis=0), kb)
    xks, tinvs, qkds = {}, {}, {}
    for p in pairs:
        qkds[p] = (qkks[p][:sub] * decays[p]).astype(BF16)
        xks[p] = jnp.where(strict, -(betas[p] * qkks[p][sub:] * decays[p]), 0.0)
        tinvs[p] = eye + xks[p]
    if log2c > 1:
        for p in pairs:
            xb = xks[p].astype(BF16)
            xks[p] = _dot(xb, xb)
    for lev in range(1, log2c):
        res = {}
        for p in pairs:
            xb = xks[p].astype(BF16)
            if lev < log2c - 1:
                res[p] = _dot(jnp.concatenate([xb, tinvs[p].astype(BF16)], axis=0), xb)
            else:
                res[p] = _dot(tinvs[p].astype(BF16), xb)
        for p in pairs:
            if lev < log2c - 1:
                xks[p] = res[p][:sub]
                tinvs[p] = tinvs[p] + res[p][sub:]
            else:
                tinvs[p] = tinvs[p] + res[p]
    uws, q_heads, kts = {}, {}, {}
    for p in pairs:
        rhs = jnp.concatenate([(vs[p] * betas[p]).astype(BF16),
                               (ks[p] * (betas[p] * egs[p])).astype(BF16)], axis=1)
        uws[p] = _dot(tinvs[p].astype(BF16), rhs)
        q_heads[p] = qs[p] * egs[p]
        st, h = p
        kts[p] = (ks[p] * jnp.exp(grems[st][:, 4 + h:5 + h])).T
    u_lists = {p: [] for p in pairs}
    qs_lists = {p: [] for p in pairs}
    for kstep in range(blk_per_seq):
        items = []
        for seq in range(nseq):
            st, j = divmod((seq * tq + kstep * c) // c, nblk)
            items += [(seq, st, j, h) for h in range(GDN_HEADS)]
        s_olds, tmps = {}, {}
        for it in items:
            seq, st, j, h = it
            p = (st, h)
            sl = slice(j * c, (j + 1) * c)
            s_olds[it] = delta_ref[seq, h]
            both = jnp.concatenate([uws[p][sl, GDN_DK:], q_heads[p][sl]], axis=0).astype(BF16)
            tmps[it] = _dot(both, s_olds[it].astype(BF16))
        for it in items:
            seq, st, j, h = it
            p = (st, h)
            sl = slice(j * c, (j + 1) * c)
            u_base = uws[p][:, :GDN_DK]
            u_lists[p].append(u_base[sl] - tmps[it][:c])
            qs_lists[p].append(tmps[it][c:])
            u_full = cat(u_lists[p] + ([u_base[(j + 1) * c:]] if j + 1 < nblk else []), 0)
            kt_j = jnp.where(cblk_row == j, kts[p], 0.0).astype(BF16)
            delta_ref[seq, h] = (s_olds[it] * jnp.exp(glasts[st][j][:, 4 + h:5 + h])
                                 + _dot(kt_j, u_full.astype(BF16)))
    o_raw = {}
    for p in pairs:
        o_raw[p] = cat(qs_lists[p], 0) + _dot(qkds[p], cat(u_lists[p], 0).astype(BF16))
    o_tiles = []
    for st in range(nsub):
        o_heads = []
        for h in range(GDN_HEADS):
            zg = z[st * sub:(st + 1) * sub, D_QKV + h * GDN_DK:D_QKV + (h + 1) * GDN_DK]
            o_heads.append(_rms(o_raw[(st, h)], normo_ref[...]) * _silu(zg))
        o_tiles.append(jnp.concatenate(o_heads, axis=1))
    o_a = cat(o_tiles, 0)

    zu = z[:, D_QKV + D_A:D_QKV + D_A + D_B]
    zv = z[:, D_QKV + D_A + D_B:]
    u_act = _gelu_tanh(zu)
    gv = _gelu_tanh(zv)
    vc = gv - jnp.mean(gv, axis=-1, keepdims=True)
    var = jnp.mean(vc * vc, axis=-1, keepdims=True)
    v_ln = vc * lax.rsqrt(var + EPS) * lng_ref[...] + lnb_ref[...]
    rv = vrows_ref.shape[1]
    vrows_ref[...] = v_ln.reshape(nseq, tq, D_B)[:, tq - rv:, :]
    log2cs = _log2(cs)
    srow = lax.broadcasted_iota(jnp.int32, (SGU_CHUNK, SGU_CHUNK), 0)
    scol = lax.broadcasted_iota(jnp.int32, (SGU_CHUNK, SGU_CHUNK), 1)
    smask = (lax.shift_right_logical(srow, log2cs) == lax.shift_right_logical(scol, log2cs)) & (srow >= scol)
    v_bf = v_ln.astype(BF16)
    mixed_groups = []
    for g in range(SGU_GROUPS):
        wg = jnp.where(smask, wsp_ref[g], 0.0).astype(BF16)
        bg = bsp_ref[g]
        parts = []
        for r0 in range(0, R, SGU_CHUNK):
            parts.append(_dot(wg, v_bf[r0:r0 + SGU_CHUNK, g * LANES:(g + 1) * LANES]) + bg)
        mixed_groups.append(parts[0] if len(parts) == 1 else jnp.concatenate(parts, axis=0))
    o_b = u_act * jnp.concatenate(mixed_groups, axis=1)

    y_in = jnp.concatenate([o_a, o_b], axis=1).astype(BF16)
    for i in range(R // pr):
        seqs = slice(i * spp, (i + 1) * spp)
        proj = _dot(y_in[i * pr:(i + 1) * pr], wout_ref[...])
        hout_ref[seqs] = h_ref[seqs] + proj.reshape(spp, tq, D_MODEL)


def _ab_layer(h, states, w, *, nseq, tq, c, cs):
    nb, t, _ = h.shape
    has_state = states is not None
    grid = (nb // nseq, t // tq)
    rv = min(SGU_CHUNK, tq)
    seq_blk = lambda shape: pl.BlockSpec(shape, lambda b, s: (b,) + (0,) * (len(shape) - 1))
    full = lambda a: pl.BlockSpec(a.shape, lambda b, s: (0,) * a.ndim)
    params = [w['gmix0'], w['win_ab'], w['wba'], w['convw'], w['gpar'], w['normo'], w['lng'], w['lnb'],
              w['wsp'], w['bsp'], w['wout_ab']]
    out_shapes = (
        jax.ShapeDtypeStruct(h.shape, F32),
        jax.ShapeDtypeStruct((nb, SUBLANES, D_QKV), F32),
        jax.ShapeDtypeStruct((nb, GDN_HEADS, GDN_DK, GDN_DK), F32),
        jax.ShapeDtypeStruct((nb, rv, D_B), F32),
    )
    return pl.pallas_call(
        functools.partial(_ab_body, nseq=nseq, tq=tq, c=c, cs=cs, has_state=has_state),
        grid=grid,
        in_specs=[pl.BlockSpec((nseq, tq, D_MODEL), lambda b, s: (b, s, 0))]
        + ([seq_blk((nseq, GDN_CONV - 1, D_QKV)), seq_blk((nseq, GDN_HEADS, GDN_DK, GDN_DK))]
           if has_state else []) + [full(a) for a in params],
        out_specs=(pl.BlockSpec((nseq, tq, D_MODEL), lambda b, s: (b, s, 0)),
                   seq_blk((nseq, SUBLANES, D_QKV)),
                   seq_blk((nseq, GDN_HEADS, GDN_DK, GDN_DK)),
                   seq_blk((nseq, rv, D_B))),
        out_shape=out_shapes,
        compiler_params=pltpu.CompilerParams(
            dimension_semantics=("arbitrary", "arbitrary"), vmem_limit_bytes=VMEM_LIMIT_BYTES),
        name="ab_mixer",
    )(h, *(states if has_state else ()), *params)


def _cd_body(*refs, nseq, tq, pos0, has_state):
    h_ref, refs = refs[0], refs[1:]
    if has_state:
        (pool0_ref, s5re0_ref, s5im0_ref), refs = refs[:3], refs[3:]
    (gmix_ref, win_ref, wpool_ref, pscale_ref, wb_ref, lam_re_ref, lam_im_ref, wcre_ref, wcim_ref,
     dskip_ref, wglu_ref, bglu_ref, wout_ref, hout_ref, pool_ref, s5re_ref, s5im_ref, bu_ref) = refs
    R = nseq * tq
    t_idx = pl.program_id(1)

    @pl.when(t_idx == 0)
    def _init():
        if has_state:
            pool_ref[...] = pool0_ref[...]
            s5re_ref[...] = s5re0_ref[...]
            s5im_ref[...] = s5im0_ref[...]
        else:
            pool_ref[...] = jnp.zeros(pool_ref.shape, F32)
            s5re_ref[...] = jnp.zeros(s5re_ref.shape, F32)
            s5im_ref[...] = jnp.zeros(s5im_ref.shape, F32)

    x3 = h_ref[...]
    xn3 = _rms(x3, gmix_ref[...]).astype(BF16)
    nper = R // CD_PERM_ROWS
    tsub = tq // nper
    log2n = _log2(nseq)
    ri = lax.broadcasted_iota(jnp.int32, (CD_PERM_ROWS, CD_PERM_ROWS), 0)
    ci = lax.broadcasted_iota(jnp.int32, (CD_PERM_ROWS, CD_PERM_ROWS), 1)
    to_tm = (ci == (ri & (nseq - 1)) * tsub + lax.shift_right_logical(ri, log2n)).astype(BF16)
    to_sm = (ri == (ci & (nseq - 1)) * tsub + lax.shift_right_logical(ci, log2n)).astype(BF16)
    xt = jnp.concatenate(
        [_dot(to_tm, xn3[:, i * tsub:(i + 1) * tsub, :].reshape(CD_PERM_ROWS, D_MODEL)).astype(BF16)
         for i in range(nper)], axis=0)
    z = _dot(xt, win_ref[...])
    xc = z[:, :D_C].reshape(tq, nseq, D_C)
    xd = z[:, D_C:]

    xext = jnp.concatenate([pool_ref[...], xc], axis=0)
    pool_ref[...] = xext[tq:]
    pos = pos0 + t_idx * tq + lax.broadcasted_iota(jnp.int32, (tq, 1, 1), 0)
    sums = xext
    outs = []
    for gi, win in enumerate(POOL_WINDOWS):
        sums = sums[win // 2:] + sums[:-(win // 2)]
        cnt = jnp.minimum(win, pos + 1).astype(F32)
        sl = slice(gi * LANES, (gi + 1) * LANES)
        m = sums[sums.shape[0] - tq:, :, sl] / cnt - xc[:, :, sl]
        outs.append(_dot(m.reshape(R, LANES).astype(BF16), wpool_ref[gi]))
    y_c = jnp.concatenate(outs, axis=1) * pscale_ref[...]

    xd_bf = xd.astype(BF16)
    for half in range(2):
        res = _dot(xd_bf[:, half * 256:(half + 1) * 256], wb_ref[half])
        bu_ref[:, half * S5_HALF:(half + 1) * S5_HALF] = res[:, :S5_HALF]
        bu_ref[:, S5_MODES + half * S5_HALF:S5_MODES + (half + 1) * S5_HALF] = res[:, S5_HALF:]
    piece = S5_SCAN_VREGS * SUBLANES * LANES // (2 * nseq)
    for p0 in range(0, S5_MODES, piece):
        ms = slice(p0, p0 + piece)
        lr = jnp.broadcast_to(lam_re_ref[:, ms], (nseq, piece))
        li = jnp.broadcast_to(lam_im_ref[:, ms], (nseq, piece))

        def step(t, carry, p0=p0, lr=lr, li=li):
            re, im = carry
            r0 = pl.multiple_of(t * nseq, nseq)
            nre = lr * re - li * im + bu_ref[pl.ds(r0, nseq), p0:p0 + piece]
            nim = lr * im + li * re + bu_ref[pl.ds(r0, nseq), S5_MODES + p0:S5_MODES + p0 + piece]
            bu_ref[pl.ds(r0, nseq), p0:p0 + piece] = nre
            bu_ref[pl.ds(r0, nseq), S5_MODES + p0:S5_MODES + p0 + piece] = nim
            return nre, nim

        re, im = lax.fori_loop(0, tq, step, (s5re_ref[:, ms], s5im_ref[:, ms]), unroll=4)
        s5re_ref[:, ms] = re
        s5im_ref[:, ms] = im

    ys = []
    kn = wcre_ref.shape[1]
    for n in range(wcre_ref.shape[0]):
        s_re = bu_ref[:, n * kn:(n + 1) * kn].astype(BF16)
        s_im = bu_ref[:, S5_MODES + n * kn:S5_MODES + (n + 1) * kn].astype(BF16)
        ys.append(_dot(s_re, wcre_ref[n]) - _dot(s_im, wcim_ref[n]))
    y = jnp.concatenate(ys, axis=1) + dskip_ref[...] * xd
    y = _gelu_tanh(y)
    y_d = y * _sigmoid(_dot(y.astype(BF16), wglu_ref[...]) + bglu_ref[...])

    y_in = jnp.concatenate([y_c, y_d], axis=1).astype(BF16)
    y_sm = jnp.concatenate(
        [_dot(to_sm, y_in[i * CD_PERM_ROWS:(i + 1) * CD_PERM_ROWS]).astype(BF16) for i in range(nper)], axis=0)
    proj = _dot(y_sm, wout_ref[...])
    for i in range(nper):
        ts = slice(i * tsub, (i + 1) * tsub)
        blk = proj[i * CD_PERM_ROWS:(i + 1) * CD_PERM_ROWS].reshape(nseq, tsub, D_MODEL)
        hout_ref[:, ts, :] = x3[:, ts, :] + blk


def _cd_layer(h, states, w, *, nseq, tq, pos0):
    nb, t, _ = h.shape
    grid = (nb // nseq, t // tq)
    full = lambda a: pl.BlockSpec(a.shape, lambda b, s: (0,) * a.ndim)
    params = [w['gmix1'], w['win_cd'], w['wpool'], w['pscale'], w['wb'], w['lam_re'], w['lam_im'],
              w['wcre'], w['wcim'], w['dskip'], w['wglu'], w['bglu'], w['wout_cd']]
    h_spec = pl.BlockSpec((nseq, tq, D_MODEL), lambda b, s: (b, s, 0))
    pool_spec = pl.BlockSpec((2 * SUBLANES, nseq, D_C), lambda b, s: (0, b, 0))
    s5_spec = pl.BlockSpec((nseq, S5_MODES), lambda b, s: (b, 0))
    state_specs = [pool_spec, s5_spec, s5_spec]
    has_state = states is not None
    out_shapes = (
        jax.ShapeDtypeStruct(h.shape, F32),
        jax.ShapeDtypeStruct((2 * SUBLANES, nb, D_C), F32),
        jax.ShapeDtypeStruct((nb, S5_MODES), F32),
        jax.ShapeDtypeStruct((nb, S5_MODES), F32),
    )
    return pl.pallas_call(
        functools.partial(_cd_body, nseq=nseq, tq=tq, pos0=pos0, has_state=has_state),
        grid=grid,
        in_specs=[h_spec] + (state_specs if has_state else []) + [full(a) for a in params],
        out_specs=tuple([h_spec] + state_specs),
        out_shape=out_shapes,
        scratch_shapes=[pltpu.VMEM((nseq * tq, 2 * S5_MODES), F32)],
        compiler_params=pltpu.CompilerParams(
            dimension_semantics=("arbitrary", "arbitrary"), vmem_limit_bytes=VMEM_LIMIT_BYTES),
        name="cd_mixer",
    )(h, *(states if has_state else ()), *params)


def _ffn_pe_body(h_ref, p_ref, gffn_ref, wup_ref, wdn_ref, gpe_ref, wgate_ref, wproj_ref, gfin_ref,
                 out_ref, *, final):
    x = h_ref[...]
    xn = _rms(x, gffn_ref[...]).astype(BF16)
    h = x
    ffc = D_FF // 4
    for ci in range(4):
        a = jnp.maximum(_dot(xn, wup_ref[:, ci * ffc:(ci + 1) * ffc]), 0.0)
        h = h + _dot((a * a).astype(BF16), wdn_ref[ci * ffc:(ci + 1) * ffc, :])
    hn = _rms(h, gpe_ref[...]).astype(BF16)
    gate = _sigmoid(_dot(hn, wgate_ref[...]))
    h = h + _dot(p_ref[...].astype(BF16), wproj_ref[...]) * gate
    if final:
        h = _rms(h, gfin_ref[...])
    out_ref[...] = h


def _ffn_pe(h2, p3, w, layer, *, tm, final):
    n = h2.shape[0]
    of_layer = lambda a: pl.BlockSpec((None,) + a.shape[1:], lambda i: (layer,) + (0,) * (a.ndim - 1),
                                      pipeline_mode=pl.Buffered(1))
    params = [w['gffn'], w['wup'], w['wdn'], w['gpe'], w['wgate'], w['wproj']]
    return pl.pallas_call(
        functools.partial(_ffn_pe_body, final=final),
        grid=(n // tm,),
        in_specs=[pl.BlockSpec((tm, D_MODEL), lambda i: (i, 0)),
                  pl.BlockSpec((None, tm, D_PLE), lambda i: (layer, i, 0))]
        + [of_layer(a) for a in params] + [pl.BlockSpec(w['gfin'].shape, lambda i: (0, 0))],
        out_specs=pl.BlockSpec((tm, D_MODEL), lambda i: (i, 0)),
        out_shape=jax.ShapeDtypeStruct(h2.shape, F32),
        compiler_params=pltpu.CompilerParams(
            dimension_semantics=("arbitrary",), vmem_limit_bytes=VMEM_LIMIT_BYTES),
        name="ffn_pe",
    )(h2, p3, *params, w['gfin'])


def _s5_prep_body(lre_ref, lim_ref, ldt_ref, bre_ref, bim_ref, lbre_ref, lbim_ref, bbre_ref, bbim_ref):
    lr = lre_ref[...]
    li = lim_ref[...]
    dt = jnp.exp(ldt_ref[...])
    mag = jnp.exp(lr * dt)
    ang = li * dt
    lb_re = mag * jnp.cos(ang)
    lb_im = mag * jnp.sin(ang)
    lbre_ref[...] = lb_re
    lbim_ref[...] = lb_im
    a = lb_re - 1.0
    b = lb_im
    den = lr * lr + li * li
    cr = (a * lr + b * li) / den
    ci = (b * lr - a * li) / den
    bre = bre_ref[...]
    bim = bim_ref[...]
    bbre_ref[...] = cr * bre - ci * bim
    bbim_ref[...] = cr * bim + ci * bre


def _s5_prep(lam_re, lam_im, log_dt, b_re, b_im):
    ldt = jnp.broadcast_to(log_dt[:, None], (S5_GROUPS, S5_STATE))
    rows = [a.reshape(1, S5_MODES) for a in (lam_re, lam_im, ldt)]
    chan_major = lambda a: a.reshape(S5_MODES, S5_GW).T
    shp = jax.ShapeDtypeStruct((S5_GW, S5_MODES), F32)
    row_shp = jax.ShapeDtypeStruct((1, S5_MODES), F32)
    lb_re, lb_im, bb_re, bb_im = pl.pallas_call(
        _s5_prep_body, out_shape=(row_shp, row_shp, shp, shp), name="s5_prep",
    )(*rows, chan_major(b_re), chan_major(b_im))
    to_gcn = lambda a: jnp.swapaxes(a.reshape(S5_GW, S5_GROUPS, S5_STATE), 0, 1)
    return lb_re, lb_im, to_gcn(bb_re), to_gcn(bb_im)


def _block_diag(a, ngrp):
    p, q = a.shape[1], a.shape[2]
    eye = jnp.eye(ngrp, dtype=a.dtype)
    a = a.reshape(S5_GROUPS // ngrp, ngrp, p, q)
    return jnp.einsum('hgpq,gk->hgpkq', a, eye).reshape(S5_GROUPS // ngrp, ngrp * p, ngrp * q)


def _prepare_weights(norm_mix, norm_ffn, norm_pe, norm_final, w_in_ab, conv_qkv, a_log, dt_bias, norm_o,
                     ln_v_gain, ln_v_bias, w_spatial, b_spatial, w_out_ab, w_in_cd, w_pool, pool_scale,
                     lam_re, lam_im, log_dt, b_re, b_im, c_re, c_im, d_skip, w_glu, b_glu, w_out_cd,
                     w_ffn_up, w_ffn_down, w_pe_proj, w_pe_gate):
    row = lambda a: a.reshape(1, -1)
    w = {}
    w['gmix0'] = row(norm_mix[0])
    w['gmix1'] = row(norm_mix[1])
    wi = w_in_ab[0]
    cut0 = D_QKV + D_A
    cut1 = cut0 + 2 * GDN_HEADS
    w['win_ab'] = jnp.concatenate([wi[:, :cut0], wi[:, cut1:]], axis=1).astype(BF16)
    w['wba'] = jnp.pad(wi[:, cut0:cut1], ((0, 0), (0, LANES - 2 * GDN_HEADS))).astype(BF16)
    w['convw'] = jnp.pad(conv_qkv[0], ((0, SUBLANES - 4), (0, 0)))
    gp = jnp.zeros((SUBLANES, LANES), F32)
    gp = gp.at[0, GDN_HEADS:2 * GDN_HEADS].set(a_log[0]).at[1, GDN_HEADS:2 * GDN_HEADS].set(dt_bias[0])
    w['gpar'] = gp
    w['normo'] = row(norm_o[0])
    w['lng'] = row(ln_v_gain[0])
    w['lnb'] = row(ln_v_bias[0])
    w['wout_ab'] = w_out_ab[0].astype(BF16)
    w['w_spatial'] = w_spatial[0]
    w['b_spatial'] = b_spatial[0]
    w['win_cd'] = w_in_cd[0].astype(BF16)
    w['wpool'] = w_pool[0].astype(BF16)
    w['pscale'] = row(pool_scale[0])
    lb_re, lb_im, bb_re, bb_im = _s5_prep(lam_re[0], lam_im[0], log_dt[0], b_re[0], b_im[0])
    w['lam_re'] = lb_re
    w['lam_im'] = lb_im
    half = S5_GROUPS // 2
    w['wb'] = jnp.concatenate([_block_diag(bb_re, half), _block_diag(bb_im, half)],
                              axis=2).astype(BF16)
    per_tile = LANES // S5_GW
    w['wcre'] = _block_diag(jnp.swapaxes(c_re[0], 1, 2), per_tile).astype(BF16)
    w['wcim'] = _block_diag(jnp.swapaxes(c_im[0], 1, 2), per_tile).astype(BF16)
    w['dskip'] = row(d_skip[0])
    w['wglu'] = w_glu[0].astype(BF16)
    w['bglu'] = row(b_glu[0])
    w['wout_cd'] = w_out_cd[0].astype(BF16)
    w['gffn'] = norm_ffn.reshape(2, 1, D_MODEL)
    w['gpe'] = norm_pe.reshape(2, 1, D_MODEL)
    w['gfin'] = row(norm_final)
    w['wup'] = w_ffn_up.astype(BF16)
    w['wdn'] = w_ffn_down.astype(BF16)
    w['wgate'] = w_pe_gate.astype(BF16)
    w['wproj'] = w_pe_proj.astype(BF16)
    return w


def _trunk(x, p, states, pos0, w, *, ab_tile, cd_tile, tm):
    nb, t, _ = x.shape
    nseq, tq = ab_tile
    c = min(64, tq)
    cs = min(SGU_CHUNK, tq)
    reps = SGU_CHUNK // cs
    wl = dict(w)
    wl['wsp'] = jnp.tile(w['w_spatial'][:, :cs, :cs], (1, reps, reps))
    wl['bsp'] = jnp.tile(w['b_spatial'][:, :cs], (1, reps))[:, :, None]
    ab_states = cd_states = None
    if states is not None:
        conv0, delta0, pool0, s5re0, s5im0 = states
        ab_states = (conv0, delta0)
        pool_in = jnp.swapaxes(jnp.pad(pool0, ((0, 0), (2 * SUBLANES - POOL_BUF, 0), (0, 0))), 0, 1)
        cd_states = (pool_in, s5re0.reshape(nb, S5_MODES), s5im0.reshape(nb, S5_MODES))
    h, conv_o, delta_o, vrows = _ab_layer(x, ab_states, wl, nseq=nseq, tq=tq, c=c, cs=cs)
    p3 = p.reshape(p.shape[0], nb * t, D_PLE)
    h = _ffn_pe(h.reshape(nb * t, D_MODEL), p3, w, 0, tm=tm, final=False)
    h, pool_o, s5re, s5im = _cd_layer(h.reshape(nb, t, D_MODEL), cd_states, w,
                                      nseq=cd_tile[0], tq=cd_tile[1], pos0=pos0)
    y = _ffn_pe(h.reshape(nb * t, D_MODEL), p3, w, 1, tm=tm, final=True)
    pool_o = jnp.swapaxes(pool_o, 0, 1)[:, 2 * SUBLANES - POOL_BUF:, :]
    return (y.reshape(nb, t, D_MODEL), conv_o[None, :, SUBLANES - (GDN_CONV - 1):, :], delta_o[None],
            vrows[None], pool_o[None], s5re.reshape(1, nb, S5_GROUPS, S5_STATE),
            s5im.reshape(1, nb, S5_GROUPS, S5_STATE))


def kernel(x_prompt, x_sample, state_conv, state_delta, state_pool, state_s5_re, state_s5_im, p_prompt, p_sample, norm_mix, norm_ffn, norm_pe, norm_final, w_in_ab, conv_qkv, a_log, dt_bias, norm_o, ln_v_gain, ln_v_bias, w_spatial, b_spatial, w_out_ab, w_in_cd, w_pool, pool_scale, lam_re, lam_im, log_dt, b_re, b_im, c_re, c_im, d_skip, w_glu, b_glu, w_out_cd, w_ffn_up, w_ffn_down, w_pe_proj, w_pe_gate):
    w = _prepare_weights(norm_mix, norm_ffn, norm_pe, norm_final, w_in_ab, conv_qkv, a_log, dt_bias, norm_o,
                         ln_v_gain, ln_v_bias, w_spatial, b_spatial, w_out_ab, w_in_cd, w_pool, pool_scale,
                         lam_re, lam_im, log_dt, b_re, b_im, c_re, c_im, d_skip, w_glu, b_glu, w_out_cd,
                         w_ffn_up, w_ffn_down, w_pe_proj, w_pe_gate)
    outs_p = _trunk(x_prompt, p_prompt, None, 0, w, ab_tile=(2, 256), cd_tile=(8, 64), tm=1024)
    states = (state_conv[0], state_delta[0], state_pool[0], state_s5_re[0], state_s5_im[0])
    outs_s = _trunk(x_sample, p_sample, states, PAST_LEN, w, ab_tile=(16, 8), cd_tile=(32, 8), tm=512)
    return (outs_p[0], outs_s[0]) + outs_p[1:] + outs_s[1:]
```

```python
import functools
import math

import jax
import jax.numpy as jnp
from jax import lax
from jax.experimental import pallas as pl
from jax.experimental.pallas import tpu as pltpu

F32 = jnp.float32
BF16 = jnp.bfloat16
HIGHEST = lax.Precision.HIGHEST

EPS = 1e-6
D_MODEL = 1024
PAST_LEN = 16384
GDN_HEADS = 4
GDN_DK = 128
GDN_CONV = 4
D_A = 512
D_QKV = 1536
D_B = 512
SGU_GROUPS = 4
SGU_CHUNK = 128
POOL_WINDOWS = (2, 4, 8, 16)
POOL_BUF = 15
D_C = 512
D_D = 512
S5_GROUPS = 32
S5_STATE = 64
S5_GW = 16
D_FF = 4096
D_PLE = 256
S5_MODES = S5_GROUPS * S5_STATE
S5_HALF = S5_MODES // 2
LANES = 128
SUBLANES = 8
GDN_SUB = 128
S5_SCAN_VREGS = 16
CD_PERM_ROWS = 256
AB_PROJ_ROWS = 256
VMEM_LIMIT_BYTES = 56 * 1024 * 1024


def _dot(a, b):
    return jnp.dot(a, b, preferred_element_type=F32)


def _dot_nt(a, b):
    return lax.dot_general(a, b, (((1,), (1,)), ((), ())), preferred_element_type=F32)


def _dot_exact(a, b):
    return jnp.dot(a, b, precision=HIGHEST, preferred_element_type=F32)


def _rms(x, g):
    return x * lax.rsqrt(jnp.mean(x * x, axis=-1, keepdims=True) + EPS) * g


def _sigmoid(x):
    return 0.5 * jnp.tanh(0.5 * x) + 0.5


def _silu(x):
    return x * _sigmoid(x)


def _gelu_tanh(x):
    return 0.5 * x * (1.0 + jnp.tanh(math.sqrt(2.0 / math.pi) * (x + 0.044715 * (x * x * x))))


def _softplus(x):
    return jnp.maximum(x, 0.0) + jnp.log(1.0 + jnp.exp(-jnp.abs(x)))


def _log2(n):
    l = n.bit_length() - 1
    assert (1 << l) == n, n
    return l


def _ab_body(*refs, nseq, tq, c, cs, has_state):
    h_ref, refs = refs[0], refs[1:]
    if has_state:
        (conv0_ref, delta0_ref), refs = refs[:2], refs[2:]
    (gmix_ref, win_ref, wba_ref, convw_ref, gpar_ref, normo_ref, lng_ref, lnb_ref, wsp_ref, bsp_ref,
     wout_ref, hout_ref, conv_ref, delta_ref, vrows_ref) = refs
    R = nseq * tq
    t_idx = pl.program_id(1)

    @pl.when(t_idx == 0)
    def _init():
        conv_ref[...] = jnp.zeros(conv_ref.shape, F32)
        if has_state:
            conv_ref[:, SUBLANES - (GDN_CONV - 1):, :] = conv0_ref[...]
            delta_ref[...] = delta0_ref[...]
        else:
            delta_ref[...] = jnp.zeros(delta_ref.shape, F32)

    cat = lambda parts, axis: parts[0] if len(parts) == 1 else jnp.concatenate(parts, axis=axis)
    pr = min(R, AB_PROJ_ROWS)
    spp = pr // tq
    cw = convw_ref[...]
    z_parts, zba_parts, qkv_parts = [], [], []
    for i in range(R // pr):
        xn = _rms(h_ref[i * spp:(i + 1) * spp].reshape(pr, D_MODEL), gmix_ref[...]).astype(BF16)
        zi = _dot(xn, win_ref[...])
        z_parts.append(zi)
        zba_parts.append(_dot(xn, wba_ref[...]))
        pre = zi[:, :D_QKV].reshape(spp, tq, D_QKV)
        xext = jnp.concatenate([conv_ref[i * spp:(i + 1) * spp], pre], axis=1)
        acc = pre * cw[3:4, :]
        for k in range(1, 4):
            acc = acc + pltpu.roll(xext, k, axis=1)[:, SUBLANES:, :] * cw[3 - k:4 - k, :]
        conv_ref[i * spp:(i + 1) * spp] = xext[:, tq:, :]
        qkv_parts.append(_silu(acc).reshape(pr, D_QKV))
    z = cat(z_parts, 0)
    zba = cat(zba_parts, 0)
    qkv = cat(qkv_parts, 0)

    gpar = gpar_ref[...]
    beta_all = _sigmoid(zba)
    g_all = -jnp.exp(gpar[0:1, :]) * _softplus(zba + gpar[1:2, :])

    sub = GDN_SUB
    log2c = _log2(c)
    row = lax.broadcasted_iota(jnp.int32, (sub, sub), 0)
    col = lax.broadcasted_iota(jnp.int32, (sub, sub), 1)
    cblk = lax.shift_right_logical(col, log2c)
    same = lax.shift_right_logical(row, log2c) == cblk
    tril = same & (row >= col)
    strict = same & (row > col)
    tril_f = tril.astype(F32)
    eye = (row == col).astype(F32)
    cblk_row = cblk[0:1, :]
    nblk = sub // c
    blk_per_seq = tq // c
    nsub = R // sub
    pairs = [(st, h) for st in range(nsub) for h in range(GDN_HEADS)]
    gc_alls, glasts, grems, gc_ts = [], [], [], []
    for st in range(nsub):
        gc_all = _dot_exact(tril_f, g_all[st * sub:(st + 1) * sub])
        glast = [gc_all[(j + 1) * c - 1:(j + 1) * c, :] for j in range(nblk)]
        gc_alls.append(gc_all)
        glasts.append(glast)
        grems.append(cat([jnp.broadcast_to(gl, (c, LANES)) for gl in glast], 0) - gc_all)
        gc_ts.append(gc_all.T)
    qs, ks, vs, betas, egs, decays, qkks = {}, {}, {}, {}, {}, {}, {}
    for p in pairs:
        st, h = p
        r0, lo = st * sub, h * GDN_DK
        q = qkv[r0:r0 + sub, lo:lo + GDN_DK]
        k = qkv[r0:r0 + sub, D_A + lo:D_A + lo + GDN_DK]
        qs[p] = q * lax.rsqrt(jnp.sum(q * q, axis=-1, keepdims=True) + EPS) * (GDN_DK ** -0.5)
        ks[p] = k * lax.rsqrt(jnp.sum(k * k, axis=-1, keepdims=True) + EPS)
        vs[p] = qkv[r0:r0 + sub, 2 * D_A + lo:2 * D_A + lo + GDN_DK]
        betas[p] = beta_all[r0:r0 + sub, h:h + 1]
        gc = gc_alls[st][:, 4 + h:5 + h]
        egs[p] = jnp.exp(gc)
        decays[p] = jnp.where(tril, jnp.exp(jnp.where(tril, gc - gc_ts[st][4 + h:5 + h, :], 0.0)), 0.0)
    for p in pairs:
        kb = ks[p].astype(BF16)
        qkks[p] = _dot_nt(jnp.concatenate([qs[p].astype(BF16), kb], axis=0), kb)
    xks, tinvs, qkds = {}, {}, {}
    for p in pairs:
        qkds[p] = (qkks[p][:sub] * decays[p]).astype(BF16)
        xks[p] = jnp.where(strict, -(betas[p] * qkks[p][sub:] * decays[p]), 0.0)
        tinvs[p] = eye + xks[p]
    if log2c > 1:
        for p in pairs:
            xb = xks[p].astype(BF16)
            xks[p] = _dot(xb, xb)
    for lev in range(1, log2c):
        res = {}
        for p in pairs:
            xb = xks[p].astype(BF16)
            if lev < log2c - 1:
                res[p] = _dot(jnp.concatenate([xb, tinvs[p].astype(BF16)], axis=0), xb)
            else:
                res[p] = _dot(tinvs[p].astype(BF16), xb)
        for p in pairs:
            if lev < log2c - 1:
                xks[p] = res[p][:sub]
                tinvs[p] = tinvs[p] + res[p][sub:]
            else:
                tinvs[p] = tinvs[p] + res[p]
    uws, q_heads, kts = {}, {}, {}
    for p in pairs:
        rhs = jnp.concatenate([(vs[p] * betas[p]).astype(BF16),
                               (ks[p] * (betas[p] * egs[p])).astype(BF16)], axis=1)
        uws[p] = _dot(tinvs[p].astype(BF16), rhs)
        q_heads[p] = qs[p] * egs[p]
        st, h = p
        kts[p] = (ks[p] * jnp.exp(grems[st][:, 4 + h:5 + h])).T
    u_lists = {p: [] for p in pairs}
    qs_lists = {p: [] for p in pairs}
    for kstep in range(blk_per_seq):
        items = []
        for seq in range(nseq):
            st, j = divmod((seq * tq + kstep * c) // c, nblk)
            items += [(seq, st, j, h) for h in range(GDN_HEADS)]
        s_olds, tmps = {}, {}
        for it in items:
            seq, st, j, h = it
            p = (st, h)
            sl = slice(j * c, (j + 1) * c)
            s_olds[it] = delta_ref[seq, h]
            both = jnp.concatenate([uws[p][sl, GDN_DK:], q_heads[p][sl]], axis=0).astype(BF16)
            tmps[it] = _dot(both, s_olds[it].astype(BF16))
        for it in items:
            seq, st, j, h = it
            p = (st, h)
            sl = slice(j * c, (j + 1) * c)
            u_base = uws[p][:, :GDN_DK]
            u_lists[p].append(u_base[sl] - tmps[it][:c])
            qs_lists[p].append(tmps[it][c:])
            u_full = cat(u_lists[p] + ([u_base[(j + 1) * c:]] if j + 1 < nblk else []), 0)
            kt_j = jnp.where(cblk_row == j, kts[p], 0.0).astype(BF16)
            delta_ref[seq, h] = (s_olds[it] * jnp.exp(glasts[st][j][:, 4 + h:5 + h])
                                 + _dot(kt_j, u_full.astype(BF16)))
    o_raw = {}
    for p in pairs:
        o_raw[p] = cat(qs_lists[p], 0) + _dot(qkds[p], cat(u_lists[p], 0).astype(BF16))
    o_tiles = []
    for st in range(nsub):
        o_heads = []
        for h in range(GDN_HEADS):
            zg = z[st * sub:(st + 1) * sub, D_QKV + h * GDN_DK:D_QKV + (h + 1) * GDN_DK]
            o_heads.append(_rms(o_raw[(st, h)], normo_ref[...]) * _silu(zg))
        o_tiles.append(jnp.concatenate(o_heads, axis=1))
    o_a = cat(o_tiles, 0)

    zu = z[:, D_QKV + D_A:D_QKV + D_A + D_B]
    zv = z[:, D_QKV + D_A + D_B:]
    u_act = _gelu_tanh(zu)
    gv = _gelu_tanh(zv)
    vc = gv - jnp.mean(gv, axis=-1, keepdims=True)
    var = jnp.mean(vc * vc, axis=-1, keepdims=True)
    v_ln = vc * lax.rsqrt(var + EPS) * lng_ref[...] + lnb_ref[...]
    rv = vrows_ref.shape[1]
    vrows_ref[...] = v_ln.reshape(nseq, tq, D_B)[:, tq - rv:, :]
    log2cs = _log2(cs)
    srow = lax.broadcasted_iota(jnp.int32, (SGU_CHUNK, SGU_CHUNK), 0)
    scol = lax.broadcasted_iota(jnp.int32, (SGU_CHUNK, SGU_CHUNK), 1)
    smask = (lax.shift_right_logical(srow, log2cs) == lax.shift_right_logical(scol, log2cs)) & (srow >= scol)
    v_bf = v_ln.astype(BF16)
    mixed_groups = []
    for g in range(SGU_GROUPS):
        wg = jnp.where(smask, wsp_ref[g], 0.0).astype(BF16)
        bg = bsp_ref[g]
        parts = []
        for r0 in range(0, R, SGU_CHUNK):
            parts.append(_dot(wg, v_bf[r0:r0 + SGU_CHUNK, g * LANES:(g + 1) * LANES]) + bg)
        mixed_groups.append(parts[0] if len(parts) == 1 else jnp.concatenate(parts, axis=0))
    o_b = u_act * jnp.concatenate(mixed_groups, axis=1)

    y_in = jnp.concatenate([o_a, o_b], axis=1).astype(BF16)
    for i in range(R // pr):
        seqs = slice(i * spp, (i + 1) * spp)
        proj = _dot(y_in[i * pr:(i + 1) * pr], wout_ref[...])
        hout_ref[seqs] = h_ref[seqs] + proj.reshape(spp, tq, D_MODEL)


def _ab_layer(h, states, w, *, nseq, tq, c, cs):
    nb, t, _ = h.shape
    has_state = states is not None
    grid = (nb // nseq, t // tq)
    rv = min(SGU_CHUNK, tq)
    seq_blk = lambda shape: pl.BlockSpec(shape, lambda b, s: (b,) + (0,) * (len(shape) - 1))
    full = lambda a: pl.BlockSpec(a.shape, lambda b, s: (0,) * a.ndim)
    params = [w['gmix0'], w['win_ab'], w['wba'], w['convw'], w['gpar'], w['normo'], w['lng'], w['lnb'],
              w['wsp'], w['bsp'], w['wout_ab']]
    out_shapes = (
        jax.ShapeDtypeStruct(h.shape, F32),
        jax.ShapeDtypeStruct((nb, SUBLANES, D_QKV), F32),
        jax.ShapeDtypeStruct((nb, GDN_HEADS, GDN_DK, GDN_DK), F32),
        jax.ShapeDtypeStruct((nb, rv, D_B), F32),
    )
    return pl.pallas_call(
        functools.partial(_ab_body, nseq=nseq, tq=tq, c=c, cs=cs, has_state=has_state),
        grid=grid,
        in_specs=[pl.BlockSpec((nseq, tq, D_MODEL), lambda b, s: (b, s, 0))]
        + ([seq_blk((nseq, GDN_CONV - 1, D_QKV)), seq_blk((nseq, GDN_HEADS, GDN_DK, GDN_DK))]
           if has_state else []) + [full(a) for a in params],
        out_specs=(pl.BlockSpec((nseq, tq, D_MODEL), lambda b, s: (b, s, 0)),
                   seq_blk((nseq, SUBLANES, D_QKV)),
                   seq_blk((nseq, GDN_HEADS, GDN_DK, GDN_DK)),
                   seq_blk((nseq, rv, D_B))),
        out_shape=out_shapes,
        compiler_params=pltpu.CompilerParams(
            dimension_semantics=("arbitrary", "arbitrary"), vmem_limit_bytes=VMEM_LIMIT_BYTES),
        name="ab_mixer",
    )(h, *(states if has_state else ()), *params)


def _cd_body(*refs, nseq, tq, pos0, has_state):
    h_ref, refs = refs[0], refs[1:]
    if has_state:
        (pool0_ref, s5re0_ref, s5im0_ref), refs = refs[:3], refs[3:]
    (gmix_ref, win_ref, wpool_ref, pscale_ref, wb_ref, lam_re_ref, lam_im_ref, wcre_ref, wcim_ref,
     dskip_ref, wglu_ref, bglu_ref, wout_ref, hout_ref, pool_ref, s5re_ref, s5im_ref, bu_ref) = refs
    R = nseq * tq
    t_idx = pl.program_id(1)

    @pl.when(t_idx == 0)
    def _init():
        if has_state:
            pool_ref[...] = pool0_ref[...]
            s5re_ref[...] = s5re0_ref[...]
            s5im_ref[...] = s5im0_ref[...]
        else:
            pool_ref[...] = jnp.zeros(pool_ref.shape, F32)
            s5re_ref[...] = jnp.zeros(s5re_ref.shape, F32)
            s5im_ref[...] = jnp.zeros(s5im_ref.shape, F32)

    x3 = h_ref[...]
    xn3 = _rms(x3, gmix_ref[...]).astype(BF16)
    nper = R // CD_PERM_ROWS
    tsub = tq // nper
    log2n = _log2(nseq)
    ri = lax.broadcasted_iota(jnp.int32, (CD_PERM_ROWS, CD_PERM_ROWS), 0)
    ci = lax.broadcasted_iota(jnp.int32, (CD_PERM_ROWS, CD_PERM_ROWS), 1)
    to_tm = (ci == (ri & (nseq - 1)) * tsub + lax.shift_right_logical(ri, log2n)).astype(BF16)
    to_sm = (ri == (ci & (nseq - 1)) * tsub + lax.shift_right_logical(ci, log2n)).astype(BF16)
    xt = jnp.concatenate(
        [_dot(to_tm, xn3[:, i * tsub:(i + 1) * tsub, :].reshape(CD_PERM_ROWS, D_MODEL)).astype(BF16)
         for i in range(nper)], axis=0)
    z = _dot(xt, win_ref[...])
    xc = z[:, :D_C].reshape(tq, nseq, D_C)
    xd = z[:, D_C:]

    xext = jnp.concatenate([pool_ref[...], xc], axis=0)
    pool_ref[...] = xext[tq:]
    pos = pos0 + t_idx * tq + lax.broadcasted_iota(jnp.int32, (tq, 1, 1), 0)
    sums = xext
    outs = []
    for gi, win in enumerate(POOL_WINDOWS):
        sums = sums[win // 2:] + sums[:-(win // 2)]
        cnt = jnp.minimum(win, pos + 1).astype(F32)
        sl = slice(gi * LANES, (gi + 1) * LANES)
        m = sums[sums.shape[0] - tq:, :, sl] / cnt - xc[:, :, sl]
        outs.append(_dot(m.reshape(R, LANES).astype(BF16), wpool_ref[gi]))
    y_c = jnp.concatenate(outs, axis=1) * pscale_ref[...]

    xd_bf = xd.astype(BF16)
    for half in range(2):
        res = _dot(xd_bf[:, half * 256:(half + 1) * 256], wb_ref[half])
        bu_ref[:, half * S5_HALF:(half + 1) * S5_HALF] = res[:, :S5_HALF]
        bu_ref[:, S5_MODES + half * S5_HALF:S5_MODES + (half + 1) * S5_HALF] = res[:, S5_HALF:]
    piece = S5_SCAN_VREGS * SUBLANES * LANES // (2 * nseq)
    pieces = range(0, S5_MODES, piece)
    lam = {p0: (jnp.broadcast_to(lam_re_ref[:, p0:p0 + piece], (nseq, piece)),
                jnp.broadcast_to(lam_im_ref[:, p0:p0 + piece], (nseq, piece))) for p0 in pieces}
    carry = {p0: (s5re_ref[:, p0:p0 + piece], s5im_ref[:, p0:p0 + piece]) for p0 in pieces}
    kn = wcre_ref.shape[1]
    for i in range(nper):
        rows = slice(i * CD_PERM_ROWS, (i + 1) * CD_PERM_ROWS)
        for p0 in pieces:
            lr, li = lam[p0]
            re, im = carry[p0]
            for t in range(i * tsub, (i + 1) * tsub):
                trows = slice(t * nseq, (t + 1) * nseq)
                re, im = (lr * re - li * im + bu_ref[trows, p0:p0 + piece],
                          lr * im + li * re + bu_ref[trows, S5_MODES + p0:S5_MODES + p0 + piece])
                bu_ref[trows, p0:p0 + piece] = re
                bu_ref[trows, S5_MODES + p0:S5_MODES + p0 + piece] = im
            carry[p0] = (re, im)
        ys = []
        for n in range(wcre_ref.shape[0]):
            s_re = bu_ref[rows, n * kn:(n + 1) * kn].astype(BF16)
            s_im = bu_ref[rows, S5_MODES + n * kn:S5_MODES + (n + 1) * kn].astype(BF16)
            ys.append(_dot(s_re, wcre_ref[n]) - _dot(s_im, wcim_ref[n]))
        y = jnp.concatenate(ys, axis=1) + dskip_ref[...] * xd[rows]
        y = _gelu_tanh(y)
        y_d = y * _sigmoid(_dot(y.astype(BF16), wglu_ref[...]) + bglu_ref[...])
        y_in = jnp.concatenate([y_c[rows], y_d], axis=1).astype(BF16)
        proj = _dot(_dot(to_sm, y_in).astype(BF16), wout_ref[...])
        ts = slice(i * tsub, (i + 1) * tsub)
        hout_ref[:, ts, :] = x3[:, ts, :] + proj.reshape(nseq, tsub, D_MODEL)
    for p0 in pieces:
        s5re_ref[:, p0:p0 + piece], s5im_ref[:, p0:p0 + piece] = carry[p0]


def _cd_layer(h, states, w, *, nseq, tq, pos0):
    nb, t, _ = h.shape
    grid = (nb // nseq, t // tq)
    full = lambda a: pl.BlockSpec(a.shape, lambda b, s: (0,) * a.ndim)
    params = [w['gmix1'], w['win_cd'], w['wpool'], w['pscale'], w['wb'], w['lam_re'], w['lam_im'],
              w['wcre'], w['wcim'], w['dskip'], w['wglu'], w['bglu'], w['wout_cd']]
    h_spec = pl.BlockSpec((nseq, tq, D_MODEL), lambda b, s: (b, s, 0))
    pool_spec = pl.BlockSpec((2 * SUBLANES, nseq, D_C), lambda b, s: (0, b, 0))
    s5_spec = pl.BlockSpec((nseq, S5_MODES), lambda b, s: (b, 0))
    state_specs = [pool_spec, s5_spec, s5_spec]
    has_state = states is not None
    out_shapes = (
        jax.ShapeDtypeStruct(h.shape, F32),
        jax.ShapeDtypeStruct((2 * SUBLANES, nb, D_C), F32),
        jax.ShapeDtypeStruct((nb, S5_MODES), F32),
        jax.ShapeDtypeStruct((nb, S5_MODES), F32),
    )
    return pl.pallas_call(
        functools.partial(_cd_body, nseq=nseq, tq=tq, pos0=pos0, has_state=has_state),
        grid=grid,
        in_specs=[h_spec] + (state_specs if has_state else []) + [full(a) for a in params],
        out_specs=tuple([h_spec] + state_specs),
        out_shape=out_shapes,
        scratch_shapes=[pltpu.VMEM((nseq * tq, 2 * S5_MODES), F32)],
        compiler_params=pltpu.CompilerParams(
            dimension_semantics=("arbitrary", "arbitrary"), vmem_limit_bytes=VMEM_LIMIT_BYTES),
        name="cd_mixer",
    )(h, *(states if has_state else ()), *params)


def _ffn_pe_body(h_ref, p_ref, gffn_ref, wup_ref, wdn_ref, gpe_ref, wgate_ref, wproj_ref, gfin_ref,
                 out_ref, *, final):
    x = h_ref[...]
    xn = _rms(x, gffn_ref[...]).astype(BF16)
    h = x
    ffc = D_FF // 4
    for ci in range(4):
        a = jnp.maximum(_dot(xn, wup_ref[:, ci * ffc:(ci + 1) * ffc]), 0.0)
        h = h + _dot((a * a).astype(BF16), wdn_ref[ci * ffc:(ci + 1) * ffc, :])
    hn = _rms(h, gpe_ref[...]).astype(BF16)
    gate = _sigmoid(_dot(hn, wgate_ref[...]))
    h = h + _dot(p_ref[...].astype(BF16), wproj_ref[...]) * gate
    if final:
        h = _rms(h, gfin_ref[...])
    out_ref[...] = h


def _ffn_pe(h2, p3, w, layer, *, tm, final):
    n = h2.shape[0]
    of_layer = lambda a: pl.BlockSpec((None,) + a.shape[1:], lambda i: (layer,) + (0,) * (a.ndim - 1),
                                      pipeline_mode=pl.Buffered(1))
    params = [w['gffn'], w['wup'], w['wdn'], w['gpe'], w['wgate'], w['wproj']]
    return pl.pallas_call(
        functools.partial(_ffn_pe_body, final=final),
        grid=(n // tm,),
        in_specs=[pl.BlockSpec((tm, D_MODEL), lambda i: (i, 0)),
                  pl.BlockSpec((None, tm, D_PLE), lambda i: (layer, i, 0))]
        + [of_layer(a) for a in params] + [pl.BlockSpec(w['gfin'].shape, lambda i: (0, 0))],
        out_specs=pl.BlockSpec((tm, D_MODEL), lambda i: (i, 0)),
        out_shape=jax.ShapeDtypeStruct(h2.shape, F32),
        compiler_params=pltpu.CompilerParams(
            dimension_semantics=("arbitrary",), vmem_limit_bytes=VMEM_LIMIT_BYTES),
        name="ffn_pe",
    )(h2, p3, *params, w['gfin'])


def _s5_prep_body(lre_ref, lim_ref, ldt_ref, bre_ref, bim_ref, lbre_ref, lbim_ref, bbre_ref, bbim_ref):
    lr = lre_ref[...]
    li = lim_ref[...]
    dt = jnp.exp(ldt_ref[...])
    mag = jnp.exp(lr * dt)
    ang = li * dt
    lb_re = mag * jnp.cos(ang)
    lb_im = mag * jnp.sin(ang)
    lbre_ref[...] = lb_re
    lbim_ref[...] = lb_im
    a = lb_re - 1.0
    b = lb_im
    den = lr * lr + li * li
    cr = (a * lr + b * li) / den
    ci = (b * lr - a * li) / den
    bre = bre_ref[...]
    bim = bim_ref[...]
    bbre_ref[...] = cr * bre - ci * bim
    bbim_ref[...] = cr * bim + ci * bre


def _s5_prep(lam_re, lam_im, log_dt, b_re, b_im):
    ldt = jnp.broadcast_to(log_dt[:, None], (S5_GROUPS, S5_STATE))
    rows = [a.reshape(1, S5_MODES) for a in (lam_re, lam_im, ldt)]
    chan_major = lambda a: a.reshape(S5_MODES, S5_GW).T
    shp = jax.ShapeDtypeStruct((S5_GW, S5_MODES), F32)
    row_shp = jax.ShapeDtypeStruct((1, S5_MODES), F32)
    lb_re, lb_im, bb_re, bb_im = pl.pallas_call(
        _s5_prep_body, out_shape=(row_shp, row_shp, shp, shp), name="s5_prep",
    )(*rows, chan_major(b_re), chan_major(b_im))
    to_gcn = lambda a: jnp.swapaxes(a.reshape(S5_GW, S5_GROUPS, S5_STATE), 0, 1)
    return lb_re, lb_im, to_gcn(bb_re), to_gcn(bb_im)


def _block_diag(a, ngrp):
    p, q = a.shape[1], a.shape[2]
    eye = jnp.eye(ngrp, dtype=a.dtype)
    a = a.reshape(S5_GROUPS // ngrp, ngrp, p, q)
    return jnp.einsum('hgpq,gk->hgpkq', a, eye).reshape(S5_GROUPS // ngrp, ngrp * p, ngrp * q)


def _prepare_weights(norm_mix, norm_ffn, norm_pe, norm_final, w_in_ab, conv_qkv, a_log, dt_bias, norm_o,
                     ln_v_gain, ln_v_bias, w_spatial, b_spatial, w_out_ab, w_in_cd, w_pool, pool_scale,
                     lam_re, lam_im, log_dt, b_re, b_im, c_re, c_im, d_skip, w_glu, b_glu, w_out_cd,
                     w_ffn_up, w_ffn_down, w_pe_proj, w_pe_gate):
    row = lambda a: a.reshape(1, -1)
    w = {}
    w['gmix0'] = row(norm_mix[0])
    w['gmix1'] = row(norm_mix[1])
    wi = w_in_ab[0]
    cut0 = D_QKV + D_A
    cut1 = cut0 + 2 * GDN_HEADS
    w['win_ab'] = jnp.concatenate([wi[:, :cut0], wi[:, cut1:]], axis=1).astype(BF16)
    w['wba'] = jnp.pad(wi[:, cut0:cut1], ((0, 0), (0, LANES - 2 * GDN_HEADS))).astype(BF16)
    w['convw'] = jnp.pad(conv_qkv[0], ((0, SUBLANES - 4), (0, 0)))
    gp = jnp.zeros((SUBLANES, LANES), F32)
    gp = gp.at[0, GDN_HEADS:2 * GDN_HEADS].set(a_log[0]).at[1, GDN_HEADS:2 * GDN_HEADS].set(dt_bias[0])
    w['gpar'] = gp
    w['normo'] = row(norm_o[0])
    w['lng'] = row(ln_v_gain[0])
    w['lnb'] = row(ln_v_bias[0])
    w['wout_ab'] = w_out_ab[0].astype(BF16)
    w['w_spatial'] = w_spatial[0]
    w['b_spatial'] = b_spatial[0]
    w['win_cd'] = w_in_cd[0].astype(BF16)
    w['wpool'] = w_pool[0].astype(BF16)
    w['pscale'] = row(pool_scale[0])
    lb_re, lb_im, bb_re, bb_im = _s5_prep(lam_re[0], lam_im[0], log_dt[0], b_re[0], b_im[0])
    w['lam_re'] = lb_re
    w['lam_im'] = lb_im
    half = S5_GROUPS // 2
    w['wb'] = jnp.concatenate([_block_diag(bb_re, half), _block_diag(bb_im, half)],
                              axis=2).astype(BF16)
    per_tile = LANES // S5_GW
    w['wcre'] = _block_diag(jnp.swapaxes(c_re[0], 1, 2), per_tile).astype(BF16)
    w['wcim'] = _block_diag(jnp.swapaxes(c_im[0], 1, 2), per_tile).astype(BF16)
    w['dskip'] = row(d_skip[0])
    w['wglu'] = w_glu[0].astype(BF16)
    w['bglu'] = row(b_glu[0])
    w['wout_cd'] = w_out_cd[0].astype(BF16)
    w['gffn'] = norm_ffn.reshape(2, 1, D_MODEL)
    w['gpe'] = norm_pe.reshape(2, 1, D_MODEL)
    w['gfin'] = row(norm_final)
    w['wup'] = w_ffn_up.astype(BF16)
    w['wdn'] = w_ffn_down.astype(BF16)
    w['wgate'] = w_pe_gate.astype(BF16)
    w['wproj'] = w_pe_proj.astype(BF16)
    return w


def _trunk(x, p, states, pos0, w, *, ab_tile, cd_tile, tm):
    nb, t, _ = x.shape
    nseq, tq = ab_tile
    c = min(64, tq)
    cs = min(SGU_CHUNK, tq)
    reps = SGU_CHUNK // cs
    wl = dict(w)
    wl['wsp'] = jnp.tile(w['w_spatial'][:, :cs, :cs], (1, reps, reps))
    wl['bsp'] = jnp.tile(w['b_spatial'][:, :cs], (1, reps))[:, :, None]
    ab_states = cd_states = None
    if states is not None:
        conv0, delta0, pool0, s5re0, s5im0 = states
        ab_states = (conv0, delta0)
        pool_in = jnp.swapaxes(jnp.pad(pool0, ((0, 0), (2 * SUBLANES - POOL_BUF, 0), (0, 0))), 0, 1)
        cd_states = (pool_in, s5re0.reshape(nb, S5_MODES), s5im0.reshape(nb, S5_MODES))
    h, conv_o, delta_o, vrows = _ab_layer(x, ab_states, wl, nseq=nseq, tq=tq, c=c, cs=cs)
    p3 = p.reshape(p.shape[0], nb * t, D_PLE)
    h = _ffn_pe(h.reshape(nb * t, D_MODEL), p3, w, 0, tm=tm, final=False)
    h, pool_o, s5re, s5im = _cd_layer(h.reshape(nb, t, D_MODEL), cd_states, w,
                                      nseq=cd_tile[0], tq=cd_tile[1], pos0=pos0)
    y = _ffn_pe(h.reshape(nb * t, D_MODEL), p3, w, 1, tm=tm, final=True)
    pool_o = jnp.swapaxes(pool_o, 0, 1)[:, 2 * SUBLANES - POOL_BUF:, :]
    return (y.reshape(nb, t, D_MODEL), conv_o[None, :, SUBLANES - (GDN_CONV - 1):, :], delta_o[None],
            vrows[None], pool_o[None], s5re.reshape(1, nb, S5_GROUPS, S5_STATE),
            s5im.reshape(1, nb, S5_GROUPS, S5_STATE))


def kernel(x_prompt, x_sample, state_conv, state_delta, state_pool, state_s5_re, state_s5_im, p_prompt, p_sample, norm_mix, norm_ffn, norm_pe, norm_final, w_in_ab, conv_qkv, a_log, dt_bias, norm_o, ln_v_gain, ln_v_bias, w_spatial, b_spatial, w_out_ab, w_in_cd, w_pool, pool_scale, lam_re, lam_im, log_dt, b_re, b_im, c_re, c_im, d_skip, w_glu, b_glu, w_out_cd, w_ffn_up, w_ffn_down, w_pe_proj, w_pe_gate):
    w = _prepare_weights(norm_mix, norm_ffn, norm_pe, norm_final, w_in_ab, conv_qkv, a_log, dt_bias, norm_o,
                         ln_v_gain, ln_v_bias, w_spatial, b_spatial, w_out_ab, w_in_cd, w_pool, pool_scale,
                         lam_re, lam_im, log_dt, b_re, b_im, c_re, c_im, d_skip, w_glu, b_glu, w_out_cd,
                         w_ffn_up, w_ffn_down, w_pe_proj, w_pe_gate)
    outs_p = _trunk(x_prompt, p_prompt, None, 0, w, ab_tile=(2, 256), cd_tile=(8, 64), tm=1024)
    states = (state_conv[0], state_delta[0], state_pool[0], state_s5_re[0], state_s5_im[0])
    outs_s = _trunk(x_sample, p_sample, states, PAST_LEN, w, ab_tile=(16, 8), cd_tile=(32, 8), tm=512)
    return (outs_p[0], outs_s[0]) + outs_p[1:] + outs_s[1:]
```

```python
import functools
import math

import jax
import jax.numpy as jnp
from jax import lax
from jax.experimental import pallas as pl
from jax.experimental.pallas import tpu as pltpu

F32 = jnp.float32
BF16 = jnp.bfloat16
HIGHEST = lax.Precision.HIGHEST

EPS = 1e-6
D_MODEL = 1024
PAST_LEN = 16384
GDN_HEADS = 4
GDN_DK = 128
GDN_CONV = 4
D_A = 512
D_QKV = 1536
D_B = 512
D_IN_AB_MAIN = D_QKV + D_A + 2 * D_B
SGU_GROUPS = 4
SGU_CHUNK = 128
POOL_WINDOWS = (2, 4, 8, 16)
POOL_BUF = 15
D_C = 512
D_D = 512
S5_GROUPS = 32
S5_STATE = 64
S5_GW = 16
D_FF = 4096
D_PLE = 256
S5_MODES = S5_GROUPS * S5_STATE
S5_HALF = S5_MODES // 2
LANES = 128
SUBLANES = 8
GDN_SUB = 128
S5_SCAN_VREGS = 16
CD_PERM_ROWS = 256
AB_PROJ_ROWS = 256
VMEM_LIMIT_BYTES = 56 * 1024 * 1024


def _dot(a, b):
    return jnp.dot(a, b, preferred_element_type=F32)


def _dot_nt(a, b):
    return lax.dot_general(a, b, (((1,), (1,)), ((), ())), preferred_element_type=F32)


def _dot_exact(a, b):
    return jnp.dot(a, b, precision=HIGHEST, preferred_element_type=F32)


def _rms(x, g):
    return x * lax.rsqrt(jnp.mean(x * x, axis=-1, keepdims=True) + EPS) * g


def _sigmoid(x):
    return 0.5 * jnp.tanh(0.5 * x) + 0.5


def _silu(x):
    return x * _sigmoid(x)


def _gelu_tanh(x):
    return 0.5 * x * (1.0 + jnp.tanh(math.sqrt(2.0 / math.pi) * (x + 0.044715 * (x * x * x))))


def _softplus(x):
    return jnp.maximum(x, 0.0) + jnp.log(1.0 + jnp.exp(-jnp.abs(x)))


def _log2(n):
    l = n.bit_length() - 1
    assert (1 << l) == n, n
    return l


def _ab_body(*refs, nseq, tq, c, cs, has_state):
    h_ref, refs = refs[0], refs[1:]
    if has_state:
        (conv0_ref, delta0_ref), refs = refs[:2], refs[2:]
    (gmix_ref, win_ref, wba_ref, convw_ref, gpar_ref, normo_ref, lng_ref, lnb_ref, wsp_ref, bsp_ref,
     wout_ref, hout_ref, conv_ref, delta_ref, vrows_ref) = refs
    R = nseq * tq
    t_idx = pl.program_id(1)

    @pl.when(t_idx == 0)
    def _init():
        conv_ref[...] = jnp.zeros(conv_ref.shape, F32)
        if has_state:
            conv_ref[:, SUBLANES - (GDN_CONV - 1):, :] = conv0_ref[...]
            delta_ref[...] = delta0_ref[...]
        else:
            delta_ref[...] = jnp.zeros(delta_ref.shape, F32)

    cat = lambda parts, axis: parts[0] if len(parts) == 1 else jnp.concatenate(parts, axis=axis)
    pr = min(R, AB_PROJ_ROWS)
    spp = pr // tq
    cw = convw_ref[...]
    z_parts, zba_parts, qkv_parts = [], [], []
    for i in range(R // pr):
        xn = _rms(h_ref[i * spp:(i + 1) * spp].reshape(pr, D_MODEL), gmix_ref[...]).astype(BF16)
        zi = _dot(xn, win_ref[...])
        z_parts.append(zi)
        zba_parts.append(_dot(xn, wba_ref[...]))
        pre = zi[:, :D_QKV].reshape(spp, tq, D_QKV)
        xext = jnp.concatenate([conv_ref[i * spp:(i + 1) * spp], pre], axis=1)
        acc = pre * cw[3:4, :]
        for k in range(1, 4):
            acc = acc + pltpu.roll(xext, k, axis=1)[:, SUBLANES:, :] * cw[3 - k:4 - k, :]
        conv_ref[i * spp:(i + 1) * spp] = xext[:, tq:, :]
        qkv_parts.append(_silu(acc).reshape(pr, D_QKV))
    z = cat(z_parts, 0)
    zba = cat(zba_parts, 0)
    qkv = cat(qkv_parts, 0)

    gpar = gpar_ref[...]
    beta_all = _sigmoid(zba)
    g_all = -jnp.exp(gpar[0:1, :]) * _softplus(zba + gpar[1:2, :])

    sub = GDN_SUB
    log2c = _log2(c)
    row = lax.broadcasted_iota(jnp.int32, (sub, sub), 0)
    col = lax.broadcasted_iota(jnp.int32, (sub, sub), 1)
    cblk = lax.shift_right_logical(col, log2c)
    same = lax.shift_right_logical(row, log2c) == cblk
    tril = same & (row >= col)
    strict = same & (row > col)
    tril_f = tril.astype(F32)
    eye = (row == col).astype(F32)
    cblk_row = cblk[0:1, :]
    nblk = sub // c
    blk_per_seq = tq // c
    nsub = R // sub
    pairs = [(st, h) for st in range(nsub) for h in range(GDN_HEADS)]
    gc_alls, glasts, grems, gc_ts = [], [], [], []
    for st in range(nsub):
        gc_all = _dot_exact(tril_f, g_all[st * sub:(st + 1) * sub])
        glast = [gc_all[(j + 1) * c - 1:(j + 1) * c, :] for j in range(nblk)]
        gc_alls.append(gc_all)
        glasts.append(glast)
        grems.append(cat([jnp.broadcast_to(gl, (c, LANES)) for gl in glast], 0) - gc_all)
        gc_ts.append(gc_all.T)
    qs, ks, vs, betas, egs, decays, qkks = {}, {}, {}, {}, {}, {}, {}
    for p in pairs:
        st, h = p
        r0, lo = st * sub, h * GDN_DK
        q = qkv[r0:r0 + sub, lo:lo + GDN_DK]
        k = qkv[r0:r0 + sub, D_A + lo:D_A + lo + GDN_DK]
        qs[p] = q * lax.rsqrt(jnp.sum(q * q, axis=-1, keepdims=True) + EPS) * (GDN_DK ** -0.5)
        ks[p] = k * lax.rsqrt(jnp.sum(k * k, axis=-1, keepdims=True) + EPS)
        vs[p] = qkv[r0:r0 + sub, 2 * D_A + lo:2 * D_A + lo + GDN_DK]
        betas[p] = beta_all[r0:r0 + sub, h:h + 1]
        gc = gc_alls[st][:, 4 + h:5 + h]
        egs[p] = jnp.exp(gc)
        decays[p] = jnp.where(tril, jnp.exp(jnp.where(tril, gc - gc_ts[st][4 + h:5 + h, :], 0.0)), 0.0)
    for p in pairs:
        kb = ks[p].astype(BF16)
        qkks[p] = _dot_nt(jnp.concatenate([qs[p].astype(BF16), kb], axis=0), kb)
    xks, tinvs, qkds = {}, {}, {}
    for p in pairs:
        qkds[p] = (qkks[p][:sub] * decays[p]).astype(BF16)
        xks[p] = jnp.where(strict, -(betas[p] * qkks[p][sub:] * decays[p]), 0.0)
        tinvs[p] = eye + xks[p]
    if log2c > 1:
        for p in pairs:
            xb = xks[p].astype(BF16)
            xks[p] = _dot(xb, xb)
    for lev in range(1, log2c):
        res = {}
        for p in pairs:
            xb = xks[p].astype(BF16)
            if lev < log2c - 1:
                res[p] = _dot(jnp.concatenate([xb, tinvs[p].astype(BF16)], axis=0), xb)
            else:
                res[p] = _dot(tinvs[p].astype(BF16), xb)
        for p in pairs:
            if lev < log2c - 1:
                xks[p] = res[p][:sub]
                tinvs[p] = tinvs[p] + res[p][sub:]
            else:
                tinvs[p] = tinvs[p] + res[p]
    uws, q_heads, kts = {}, {}, {}
    for p in pairs:
        rhs = jnp.concatenate([(vs[p] * betas[p]).astype(BF16),
                               (ks[p] * (betas[p] * egs[p])).astype(BF16)], axis=1)
        uws[p] = _dot(tinvs[p].astype(BF16), rhs)
        q_heads[p] = qs[p] * egs[p]
        st, h = p
        kts[p] = (ks[p] * jnp.exp(grems[st][:, 4 + h:5 + h])).T
    u_lists = {p: [] for p in pairs}
    qs_lists = {p: [] for p in pairs}
    for kstep in range(blk_per_seq):
        items = []
        for seq in range(nseq):
            st, j = divmod((seq * tq + kstep * c) // c, nblk)
            items += [(seq, st, j, h) for h in range(GDN_HEADS)]
        s_olds, tmps = {}, {}
        for it in items:
            seq, st, j, h = it
            p = (st, h)
            sl = slice(j * c, (j + 1) * c)
            s_olds[it] = delta_ref[seq, h]
            both = jnp.concatenate([uws[p][sl, GDN_DK:], q_heads[p][sl]], axis=0).astype(BF16)
            tmps[it] = _dot(both, s_olds[it].astype(BF16))
        for it in items:
            seq, st, j, h = it
            p = (st, h)
            sl = slice(j * c, (j + 1) * c)
            u_base = uws[p][:, :GDN_DK]
            u_lists[p].append(u_base[sl] - tmps[it][:c])
            qs_lists[p].append(tmps[it][c:])
            u_full = cat(u_lists[p] + ([u_base[(j + 1) * c:]] if j + 1 < nblk else []), 0)
            kt_j = jnp.where(cblk_row == j, kts[p], 0.0).astype(BF16)
            delta_ref[seq, h] = (s_olds[it] * jnp.exp(glasts[st][j][:, 4 + h:5 + h])
                                 + _dot(kt_j, u_full.astype(BF16)))
    o_raw = {}
    for p in pairs:
        o_raw[p] = cat(qs_lists[p], 0) + _dot(qkds[p], cat(u_lists[p], 0).astype(BF16))
    o_tiles = []
    for st in range(nsub):
        o_heads = []
        for h in range(GDN_HEADS):
            zg = z[st * sub:(st + 1) * sub, D_QKV + h * GDN_DK:D_QKV + (h + 1) * GDN_DK]
            o_heads.append(_rms(o_raw[(st, h)], normo_ref[...]) * _silu(zg))
        o_tiles.append(jnp.concatenate(o_heads, axis=1))
    o_a = cat(o_tiles, 0)

    zu = z[:, D_QKV + D_A:D_QKV + D_A + D_B]
    zv = z[:, D_QKV + D_A + D_B:]
    u_act = _gelu_tanh(zu)
    gv = _gelu_tanh(zv)
    vc = gv - jnp.mean(gv, axis=-1, keepdims=True)
    var = jnp.mean(vc * vc, axis=-1, keepdims=True)
    v_ln = vc * lax.rsqrt(var + EPS) * lng_ref[...] + lnb_ref[...]
    rv = vrows_ref.shape[1]
    vrows_ref[...] = v_ln.reshape(nseq, tq, D_B)[:, tq - rv:, :]
    log2cs = _log2(cs)
    srow = lax.broadcasted_iota(jnp.int32, (SGU_CHUNK, SGU_CHUNK), 0)
    scol = lax.broadcasted_iota(jnp.int32, (SGU_CHUNK, SGU_CHUNK), 1)
    smask = (lax.shift_right_logical(srow, log2cs) == lax.shift_right_logical(scol, log2cs)) & (srow >= scol)
    v_bf = v_ln.astype(BF16)
    mixed_groups = []
    for g in range(SGU_GROUPS):
        wg = jnp.where(smask, wsp_ref[g], 0.0).astype(BF16)
        bg = bsp_ref[g]
        parts = []
        for r0 in range(0, R, SGU_CHUNK):
            parts.append(_dot(wg, v_bf[r0:r0 + SGU_CHUNK, g * LANES:(g + 1) * LANES]) + bg)
        mixed_groups.append(parts[0] if len(parts) == 1 else jnp.concatenate(parts, axis=0))
    o_b = u_act * jnp.concatenate(mixed_groups, axis=1)

    y_in = jnp.concatenate([o_a, o_b], axis=1).astype(BF16)
    for i in range(R // pr):
        seqs = slice(i * spp, (i + 1) * spp)
        proj = _dot(y_in[i * pr:(i + 1) * pr], wout_ref[...])
        hout_ref[seqs] = h_ref[seqs] + proj.reshape(spp, tq, D_MODEL)


def _ab_layer(h, states, w, *, nseq, tq, c, cs):
    nb, t, _ = h.shape
    has_state = states is not None
    grid = (nb // nseq, t // tq)
    rv = min(SGU_CHUNK, tq)
    seq_blk = lambda shape: pl.BlockSpec(shape, lambda b, s: (b,) + (0,) * (len(shape) - 1))
    full = lambda a: pl.BlockSpec(a.shape, lambda b, s: (0,) * a.ndim)
    params = [w['gmix0'], w['win_ab'], w['wba'], w['convw'], w['gpar'], w['normo'], w['lng'], w['lnb'],
              w['wsp'], w['bsp'], w['wout_ab']]
    out_shapes = (
        jax.ShapeDtypeStruct(h.shape, F32),
        jax.ShapeDtypeStruct((nb, SUBLANES, D_QKV), F32),
        jax.ShapeDtypeStruct((nb, GDN_HEADS, GDN_DK, GDN_DK), F32),
        jax.ShapeDtypeStruct((nb, rv, D_B), F32),
    )
    return pl.pallas_call(
        functools.partial(_ab_body, nseq=nseq, tq=tq, c=c, cs=cs, has_state=has_state),
        grid=grid,
        in_specs=[pl.BlockSpec((nseq, tq, D_MODEL), lambda b, s: (b, s, 0))]
        + ([seq_blk((nseq, GDN_CONV - 1, D_QKV)), seq_blk((nseq, GDN_HEADS, GDN_DK, GDN_DK))]
           if has_state else []) + [full(a) for a in params],
        out_specs=(pl.BlockSpec((nseq, tq, D_MODEL), lambda b, s: (b, s, 0)),
                   seq_blk((nseq, SUBLANES, D_QKV)),
                   seq_blk((nseq, GDN_HEADS, GDN_DK, GDN_DK)),
                   seq_blk((nseq, rv, D_B))),
        out_shape=out_shapes,
        compiler_params=pltpu.CompilerParams(
            dimension_semantics=("arbitrary", "arbitrary"), vmem_limit_bytes=VMEM_LIMIT_BYTES),
        name="ab_mixer",
    )(h, *(states if has_state else ()), *params)


def _cd_body(*refs, nseq, tq, pos0, has_state):
    h_ref, refs = refs[0], refs[1:]
    if has_state:
        (pool0_ref, s5re0_ref, s5im0_ref), refs = refs[:3], refs[3:]
    (gmix_ref, win_ref, wpool_ref, pscale_ref, wb_ref, lam_re_ref, lam_im_ref, wcre_ref, wcim_ref,
     dskip_ref, wglu_ref, bglu_ref, wout_ref, hout_ref, pool_ref, s5re_ref, s5im_ref, bu_ref) = refs
    R = nseq * tq
    t_idx = pl.program_id(1)

    @pl.when(t_idx == 0)
    def _init():
        if has_state:
            pool_ref[...] = pool0_ref[...]
            s5re_ref[...] = s5re0_ref[...]
            s5im_ref[...] = s5im0_ref[...]
        else:
            pool_ref[...] = jnp.zeros(pool_ref.shape, F32)
            s5re_ref[...] = jnp.zeros(s5re_ref.shape, F32)
            s5im_ref[...] = jnp.zeros(s5im_ref.shape, F32)

    x3 = h_ref[...]
    xn3 = _rms(x3, gmix_ref[...]).astype(BF16)
    nper = R // CD_PERM_ROWS
    tsub = tq // nper
    log2n = _log2(nseq)
    ri = lax.broadcasted_iota(jnp.int32, (CD_PERM_ROWS, CD_PERM_ROWS), 0)
    ci = lax.broadcasted_iota(jnp.int32, (CD_PERM_ROWS, CD_PERM_ROWS), 1)
    to_tm = (ci == (ri & (nseq - 1)) * tsub + lax.shift_right_logical(ri, log2n)).astype(BF16)
    to_sm = (ri == (ci & (nseq - 1)) * tsub + lax.shift_right_logical(ci, log2n)).astype(BF16)
    xt = jnp.concatenate(
        [_dot(to_tm, xn3[:, i * tsub:(i + 1) * tsub, :].reshape(CD_PERM_ROWS, D_MODEL)).astype(BF16)
         for i in range(nper)], axis=0)
    z = _dot(xt, win_ref[...])
    xc = z[:, :D_C].reshape(tq, nseq, D_C)
    xd = z[:, D_C:]

    xext = jnp.concatenate([pool_ref[...], xc], axis=0)
    pool_ref[...] = xext[tq:]
    pos = pos0 + t_idx * tq + lax.broadcasted_iota(jnp.int32, (tq, 1, 1), 0)
    sums = xext
    outs = []
    for gi, win in enumerate(POOL_WINDOWS):
        sums = sums[win // 2:] + sums[:-(win // 2)]
        cnt = jnp.minimum(win, pos + 1).astype(F32)
        sl = slice(gi * LANES, (gi + 1) * LANES)
        m = sums[sums.shape[0] - tq:, :, sl] / cnt - xc[:, :, sl]
        outs.append(_dot(m.reshape(R, LANES).astype(BF16), wpool_ref[gi]))
    y_c = jnp.concatenate(outs, axis=1) * pscale_ref[...]

    xd_bf = xd.astype(BF16)
    for half in range(2):
        res = _dot(xd_bf[:, half * 256:(half + 1) * 256], wb_ref[half])
        bu_ref[:, half * S5_HALF:(half + 1) * S5_HALF] = res[:, :S5_HALF]
        bu_ref[:, S5_MODES + half * S5_HALF:S5_MODES + (half + 1) * S5_HALF] = res[:, S5_HALF:]
    piece = S5_SCAN_VREGS * SUBLANES * LANES // (2 * nseq)
    pieces = range(0, S5_MODES, piece)
    lam = {p0: (jnp.broadcast_to(lam_re_ref[:, p0:p0 + piece], (nseq, piece)),
                jnp.broadcast_to(lam_im_ref[:, p0:p0 + piece], (nseq, piece))) for p0 in pieces}
    carry = {p0: (s5re_ref[:, p0:p0 + piece], s5im_ref[:, p0:p0 + piece]) for p0 in pieces}
    kn = wcre_ref.shape[1]
    for i in range(nper):
        rows = slice(i * CD_PERM_ROWS, (i + 1) * CD_PERM_ROWS)
        for p0 in pieces:
            lr, li = lam[p0]
            re, im = carry[p0]
            for t in range(i * tsub, (i + 1) * tsub):
                trows = slice(t * nseq, (t + 1) * nseq)
                re, im = (lr * re - li * im + bu_ref[trows, p0:p0 + piece],
                          lr * im + li * re + bu_ref[trows, S5_MODES + p0:S5_MODES + p0 + piece])
                bu_ref[trows, p0:p0 + piece] = re
                bu_ref[trows, S5_MODES + p0:S5_MODES + p0 + piece] = im
            carry[p0] = (re, im)
        ys = []
        for n in range(wcre_ref.shape[0]):
            s_re = bu_ref[rows, n * kn:(n + 1) * kn].astype(BF16)
            s_im = bu_ref[rows, S5_MODES + n * kn:S5_MODES + (n + 1) * kn].astype(BF16)
            ys.append(_dot(s_re, wcre_ref[n]) - _dot(s_im, wcim_ref[n]))
        y = jnp.concatenate(ys, axis=1) + dskip_ref[...] * xd[rows]
        y = _gelu_tanh(y)
        y_d = y * _sigmoid(_dot(y.astype(BF16), wglu_ref[...]) + bglu_ref[...])
        y_in = jnp.concatenate([y_c[rows], y_d], axis=1).astype(BF16)
        proj = _dot(_dot(to_sm, y_in).astype(BF16), wout_ref[...])
        ts = slice(i * tsub, (i + 1) * tsub)
        hout_ref[:, ts, :] = x3[:, ts, :] + proj.reshape(nseq, tsub, D_MODEL)
    for p0 in pieces:
        s5re_ref[:, p0:p0 + piece], s5im_ref[:, p0:p0 + piece] = carry[p0]


def _cd_layer(h, states, w, *, nseq, tq, pos0):
    nb, t, _ = h.shape
    grid = (nb // nseq, t // tq)
    full = lambda a: pl.BlockSpec(a.shape, lambda b, s: (0,) * a.ndim)
    params = [w['gmix1'], w['win_cd'], w['wpool'], w['pscale'], w['wb'], w['lam_re'], w['lam_im'],
              w['wcre'], w['wcim'], w['dskip'], w['wglu'], w['bglu'], w['wout_cd']]
    h_spec = pl.BlockSpec((nseq, tq, D_MODEL), lambda b, s: (b, s, 0))
    pool_spec = pl.BlockSpec((2 * SUBLANES, nseq, D_C), lambda b, s: (0, b, 0))
    s5_spec = pl.BlockSpec((nseq, S5_MODES), lambda b, s: (b, 0))
    state_specs = [pool_spec, s5_spec, s5_spec]
    has_state = states is not None
    out_shapes = (
        jax.ShapeDtypeStruct(h.shape, F32),
        jax.ShapeDtypeStruct((2 * SUBLANES, nb, D_C), F32),
        jax.ShapeDtypeStruct((nb, S5_MODES), F32),
        jax.ShapeDtypeStruct((nb, S5_MODES), F32),
    )
    return pl.pallas_call(
        functools.partial(_cd_body, nseq=nseq, tq=tq, pos0=pos0, has_state=has_state),
        grid=grid,
        in_specs=[h_spec] + (state_specs if has_state else []) + [full(a) for a in params],
        out_specs=tuple([h_spec] + state_specs),
        out_shape=out_shapes,
        scratch_shapes=[pltpu.VMEM((nseq * tq, 2 * S5_MODES), F32)],
        compiler_params=pltpu.CompilerParams(
            dimension_semantics=("arbitrary", "arbitrary"), vmem_limit_bytes=VMEM_LIMIT_BYTES),
        name="cd_mixer",
    )(h, *(states if has_state else ()), *params)


def _ffn_pe_body(h_ref, p_ref, gffn_ref, wup_ref, wdn_ref, gpe_ref, wgate_ref, wproj_ref, gfin_ref,
                 out_ref, *, final):
    x = h_ref[...]
    xn = _rms(x, gffn_ref[...]).astype(BF16)
    h = x
    ffc = D_FF // 4
    for ci in range(4):
        a = jnp.maximum(_dot(xn, wup_ref[:, ci * ffc:(ci + 1) * ffc]), 0.0)
        h = h + _dot((a * a).astype(BF16), wdn_ref[ci * ffc:(ci + 1) * ffc, :])
    hn = _rms(h, gpe_ref[...]).astype(BF16)
    gate = _sigmoid(_dot(hn, wgate_ref[...]))
    h = h + _dot(p_ref[...].astype(BF16), wproj_ref[...]) * gate
    if final:
        h = _rms(h, gfin_ref[...])
    out_ref[...] = h


def _ffn_pe(h2, p3, w, layer, *, tm, final):
    n = h2.shape[0]
    of_layer = lambda a: pl.BlockSpec((None,) + a.shape[1:], lambda i: (layer,) + (0,) * (a.ndim - 1),
                                      pipeline_mode=pl.Buffered(1))
    params = [w['gffn'], w['wup'], w['wdn'], w['gpe'], w['wgate'], w['wproj']]
    return pl.pallas_call(
        functools.partial(_ffn_pe_body, final=final),
        grid=(n // tm,),
        in_specs=[pl.BlockSpec((tm, D_MODEL), lambda i: (i, 0)),
                  pl.BlockSpec((None, tm, D_PLE), lambda i: (layer, i, 0))]
        + [of_layer(a) for a in params] + [pl.BlockSpec(w['gfin'].shape, lambda i: (0, 0))],
        out_specs=pl.BlockSpec((tm, D_MODEL), lambda i: (i, 0)),
        out_shape=jax.ShapeDtypeStruct(h2.shape, F32),
        compiler_params=pltpu.CompilerParams(
            dimension_semantics=("arbitrary",), vmem_limit_bytes=VMEM_LIMIT_BYTES),
        name="ffn_pe",
    )(h2, p3, *params, w['gfin'])


def _s5_prep_body(lre_ref, lim_ref, ldt_ref, bre_ref, bim_ref, lbre_ref, lbim_ref, bbre_ref, bbim_ref):
    lr = lre_ref[...]
    li = lim_ref[...]
    dt = jnp.exp(ldt_ref[...])
    mag = jnp.exp(lr * dt)
    ang = li * dt
    lb_re = mag * jnp.cos(ang)
    lb_im = mag * jnp.sin(ang)
    lbre_ref[...] = lb_re
    lbim_ref[...] = lb_im
    a = lb_re - 1.0
    b = lb_im
    den = lr * lr + li * li
    cr = (a * lr + b * li) / den
    ci = (b * lr - a * li) / den
    bre = bre_ref[...]
    bim = bim_ref[...]
    bbre_ref[...] = cr * bre - ci * bim
    bbim_ref[...] = cr * bim + ci * bre


def _s5_prep(lam_re, lam_im, log_dt, b_re, b_im):
    ldt = jnp.broadcast_to(log_dt[:, None], (S5_GROUPS, S5_STATE))
    rows = [a.reshape(1, S5_MODES) for a in (lam_re, lam_im, ldt)]
    chan_major = lambda a: a.reshape(S5_MODES, S5_GW).T
    shp = jax.ShapeDtypeStruct((S5_GW, S5_MODES), F32)
    row_shp = jax.ShapeDtypeStruct((1, S5_MODES), F32)
    lb_re, lb_im, bb_re, bb_im = pl.pallas_call(
        _s5_prep_body, out_shape=(row_shp, row_shp, shp, shp), name="s5_prep",
    )(*rows, chan_major(b_re), chan_major(b_im))
    to_gcn = lambda a: jnp.swapaxes(a.reshape(S5_GW, S5_GROUPS, S5_STATE), 0, 1)
    return lb_re, lb_im, to_gcn(bb_re), to_gcn(bb_im)


def _win_ab_prep_body(w_ref, main_ref, ba_ref):
    a = w_ref[...]
    cut0 = D_QKV + D_A
    cut1 = cut0 + 2 * GDN_HEADS
    main_ref[:, :cut0] = a[:, :cut0].astype(BF16)
    main_ref[:, cut0:] = a[:, cut1:].astype(BF16)
    pad = jnp.zeros((a.shape[0], LANES - 2 * GDN_HEADS), F32)
    ba_ref[...] = jnp.concatenate([a[:, cut0:cut1], pad], axis=1).astype(BF16)


def _win_ab_prep(wi, rows=256):
    k, n = wi.shape
    return pl.pallas_call(
        _win_ab_prep_body,
        grid=(k // rows,),
        in_specs=[pl.BlockSpec((rows, n), lambda i: (i, 0))],
        out_specs=(pl.BlockSpec((rows, D_IN_AB_MAIN), lambda i: (i, 0)),
                   pl.BlockSpec((rows, LANES), lambda i: (i, 0))),
        out_shape=(jax.ShapeDtypeStruct((k, D_IN_AB_MAIN), BF16), jax.ShapeDtypeStruct((k, LANES), BF16)),
        name="win_ab_prep",
    )(wi)


def _block_diag(a, ngrp):
    p, q = a.shape[1], a.shape[2]
    eye = jnp.eye(ngrp, dtype=a.dtype)
    a = a.reshape(S5_GROUPS // ngrp, ngrp, p, q)
    return jnp.einsum('hgpq,gk->hgpkq', a, eye).reshape(S5_GROUPS // ngrp, ngrp * p, ngrp * q)


def _prepare_weights(norm_mix, norm_ffn, norm_pe, norm_final, w_in_ab, conv_qkv, a_log, dt_bias, norm_o,
                     ln_v_gain, ln_v_bias, w_spatial, b_spatial, w_out_ab, w_in_cd, w_pool, pool_scale,
                     lam_re, lam_im, log_dt, b_re, b_im, c_re, c_im, d_skip, w_glu, b_glu, w_out_cd,
                     w_ffn_up, w_ffn_down, w_pe_proj, w_pe_gate):
    row = lambda a: a.reshape(1, -1)
    w = {}
    w['gmix0'] = row(norm_mix[0])
    w['gmix1'] = row(norm_mix[1])
    w['win_ab'], w['wba'] = _win_ab_prep(w_in_ab[0])
    w['convw'] = jnp.pad(conv_qkv[0], ((0, SUBLANES - 4), (0, 0)))
    w['gpar'] = jnp.pad(jnp.concatenate([a_log, dt_bias], axis=0),
                        ((0, 0), (GDN_HEADS, LANES - 2 * GDN_HEADS)))
    w['normo'] = row(norm_o[0])
    w['lng'] = row(ln_v_gain[0])
    w['lnb'] = row(ln_v_bias[0])
    w['wout_ab'] = w_out_ab[0].astype(BF16)
    w['w_spatial'] = w_spatial[0]
    w['b_spatial'] = b_spatial[0]
    w['win_cd'] = w_in_cd[0].astype(BF16)
    w['wpool'] = w_pool[0].astype(BF16)
    w['pscale'] = row(pool_scale[0])
    lb_re, lb_im, bb_re, bb_im = _s5_prep(lam_re[0], lam_im[0], log_dt[0], b_re[0], b_im[0])
    w['lam_re'] = lb_re
    w['lam_im'] = lb_im
    half = S5_GROUPS // 2
    w['wb'] = jnp.concatenate([_block_diag(bb_re, half), _block_diag(bb_im, half)],
                              axis=2).astype(BF16)
    per_tile = LANES // S5_GW
    w['wcre'] = _block_diag(jnp.swapaxes(c_re[0], 1, 2), per_tile).astype(BF16)
    w['wcim'] = _block_diag(jnp.swapaxes(c_im[0], 1, 2), per_tile).astype(BF16)
    w['dskip'] = row(d_skip[0])
    w['wglu'] = w_glu[0].astype(BF16)
    w['bglu'] = row(b_glu[0])
    w['wout_cd'] = w_out_cd[0].astype(BF16)
    w['gffn'] = norm_ffn.reshape(2, 1, D_MODEL)
    w['gpe'] = norm_pe.reshape(2, 1, D_MODEL)
    w['gfin'] = row(norm_final)
    w['wup'] = w_ffn_up.astype(BF16)
    w['wdn'] = w_ffn_down.astype(BF16)
    w['wgate'] = w_pe_gate.astype(BF16)
    w['wproj'] = w_pe_proj.astype(BF16)
    return w


def _trunk(x, p, states, pos0, w, *, ab_tile, cd_tile, tm):
    nb, t, _ = x.shape
    nseq, tq = ab_tile
    c = min(64, tq)
    cs = min(SGU_CHUNK, tq)
    reps = SGU_CHUNK // cs
    wl = dict(w)
    wl['wsp'] = jnp.tile(w['w_spatial'][:, :cs, :cs], (1, reps, reps))
    wl['bsp'] = jnp.tile(w['b_spatial'][:, :cs], (1, reps))[:, :, None]
    ab_states = cd_states = None
    if states is not None:
        conv0, delta0, pool0, s5re0, s5im0 = states
        ab_states = (conv0, delta0)
        pool_in = jnp.swapaxes(jnp.pad(pool0, ((0, 0), (2 * SUBLANES - POOL_BUF, 0), (0, 0))), 0, 1)
        cd_states = (pool_in, s5re0.reshape(nb, S5_MODES), s5im0.reshape(nb, S5_MODES))
    h, conv_o, delta_o, vrows = _ab_layer(x, ab_states, wl, nseq=nseq, tq=tq, c=c, cs=cs)
    p3 = p.reshape(p.shape[0], nb * t, D_PLE)
    h = _ffn_pe(h.reshape(nb * t, D_MODEL), p3, w, 0, tm=tm, final=False)
    h, pool_o, s5re, s5im = _cd_layer(h.reshape(nb, t, D_MODEL), cd_states, w,
                                      nseq=cd_tile[0], tq=cd_tile[1], pos0=pos0)
    y = _ffn_pe(h.reshape(nb * t, D_MODEL), p3, w, 1, tm=tm, final=True)
    pool_o = jnp.swapaxes(pool_o, 0, 1)[:, 2 * SUBLANES - POOL_BUF:, :]
    return (y.reshape(nb, t, D_MODEL), conv_o[None, :, SUBLANES - (GDN_CONV - 1):, :], delta_o[None],
            vrows[None], pool_o[None], s5re.reshape(1, nb, S5_GROUPS, S5_STATE),
            s5im.reshape(1, nb, S5_GROUPS, S5_STATE))


def kernel(x_prompt, x_sample, state_conv, state_delta, state_pool, state_s5_re, state_s5_im, p_prompt, p_sample, norm_mix, norm_ffn, norm_pe, norm_final, w_in_ab, conv_qkv, a_log, dt_bias, norm_o, ln_v_gain, ln_v_bias, w_spatial, b_spatial, w_out_ab, w_in_cd, w_pool, pool_scale, lam_re, lam_im, log_dt, b_re, b_im, c_re, c_im, d_skip, w_glu, b_glu, w_out_cd, w_ffn_up, w_ffn_down, w_pe_proj, w_pe_gate):
    w = _prepare_weights(norm_mix, norm_ffn, norm_pe, norm_final, w_in_ab, conv_qkv, a_log, dt_bias, norm_o,
                         ln_v_gain, ln_v_bias, w_spatial, b_spatial, w_out_ab, w_in_cd, w_pool, pool_scale,
                         lam_re, lam_im, log_dt, b_re, b_im, c_re, c_im, d_skip, w_glu, b_glu, w_out_cd,
                         w_ffn_up, w_ffn_down, w_pe_proj, w_pe_gate)
    outs_p = _trunk(x_prompt, p_prompt, None, 0, w, ab_tile=(4, 128), cd_tile=(8, 64), tm=1024)
    states = (state_conv[0], state_delta[0], state_pool[0], state_s5_re[0], state_s5_im[0])
    outs_s = _trunk(x_sample, p_sample, states, PAST_LEN, w, ab_tile=(16, 8), cd_tile=(32, 8), tm=512)
    return (outs_p[0], outs_s[0]) + outs_p[1:] + outs_s[1:]
```

```python
import functools
import math

import jax
import jax.numpy as jnp
from jax import lax
from jax.experimental import pallas as pl
from jax.experimental.pallas import tpu as pltpu

F32 = jnp.float32
BF16 = jnp.bfloat16
HIGHEST = lax.Precision.HIGHEST

EPS = 1e-6
D_MODEL = 1024
PAST_LEN = 16384
GDN_HEADS = 4
GDN_DK = 128
GDN_CONV = 4
D_A = 512
D_QKV = 1536
D_B = 512
SGU_GROUPS = 4
SGU_CHUNK = 128
POOL_WINDOWS = (2, 4, 8, 16)
POOL_BUF = 15
D_C = 512
D_D = 512
S5_GROUPS = 32
S5_STATE = 64
S5_GW = 16
D_FF = 4096
D_PLE = 256
S5_MODES = S5_GROUPS * S5_STATE
S5_HALF = S5_MODES // 2
LANES = 128
SUBLANES = 8
GDN_SUB = 128
S5_SCAN_VREGS = 16
CD_PERM_ROWS = 256
AB_PROJ_ROWS = 256
FFN_ROWS = 1024
VMEM_LIMIT_BYTES = 56 * 1024 * 1024


def _dot(a, b):
    return jnp.dot(a, b, preferred_element_type=F32)


def _dot_nt(a, b):
    return lax.dot_general(a, b, (((1,), (1,)), ((), ())), preferred_element_type=F32)


def _dot_exact(a, b):
    return jnp.dot(a, b, precision=HIGHEST, preferred_element_type=F32)


def _rms(x, g):
    return x * lax.rsqrt(jnp.mean(x * x, axis=-1, keepdims=True) + EPS) * g


def _sigmoid(x):
    return 0.5 * jnp.tanh(0.5 * x) + 0.5


def _silu(x):
    return x * _sigmoid(x)


def _gelu_tanh(x):
    return 0.5 * x * (1.0 + jnp.tanh(math.sqrt(2.0 / math.pi) * (x + 0.044715 * (x * x * x))))


def _softplus(x):
    return jnp.maximum(x, 0.0) + jnp.log(1.0 + jnp.exp(-jnp.abs(x)))


def _log2(n):
    l = n.bit_length() - 1
    assert (1 << l) == n, n
    return l


def _ab_body(*refs, nseq, tq, c, cs, has_state, n_cast):
    h_ref, refs = refs[0], refs[1:]
    if has_state:
        (conv0_ref, delta0_ref), refs = refs[:2], refs[2:]
    (gmix_ref, win_qkvg_ref, win_uv_ref, wba_ref, convw_ref, gpar_ref, normo_ref, lng_ref, lnb_ref, wsp_ref,
     bsp_ref, wout_ref), refs = refs[:12], refs[12:]
    cast_in, refs = refs[:n_cast], refs[n_cast:]
    (hout_ref, conv_ref, delta_ref, vrows_ref), cast_out = refs[:4], refs[4:]
    for src_ref, dst_ref in zip(cast_in, cast_out):
        dst_ref[...] = src_ref[...].astype(BF16)
    R = nseq * tq
    t_idx = pl.program_id(1)

    @pl.when(t_idx == 0)
    def _init():
        conv_ref[...] = jnp.zeros(conv_ref.shape, F32)
        if has_state:
            conv_ref[:, SUBLANES - (GDN_CONV - 1):, :] = conv0_ref[...]
            delta_ref[...] = delta0_ref[...]
        else:
            delta_ref[...] = jnp.zeros(delta_ref.shape, F32)

    cat = lambda parts, axis: parts[0] if len(parts) == 1 else jnp.concatenate(parts, axis=axis)
    pr = min(R, AB_PROJ_ROWS)
    spp = pr // tq
    cw = convw_ref[...]
    z_parts, zba_parts, qkv_parts = [], [], []
    for i in range(R // pr):
        xn = _rms(h_ref[i * spp:(i + 1) * spp].reshape(pr, D_MODEL), gmix_ref[...]).astype(BF16)
        zi = jnp.concatenate([_dot(xn, win_qkvg_ref[...]), _dot(xn, win_uv_ref[...])], axis=1)
        z_parts.append(zi)
        zba_parts.append(_dot(xn, wba_ref[...]))
        pre = zi[:, :D_QKV].reshape(spp, tq, D_QKV)
        xext = jnp.concatenate([conv_ref[i * spp:(i + 1) * spp], pre], axis=1)
        acc = pre * cw[3:4, :]
        for k in range(1, GDN_CONV):
            acc = acc + pltpu.roll(xext, k, axis=1)[:, SUBLANES:, :] * cw[3 - k:4 - k, :]
        conv_ref[i * spp:(i + 1) * spp] = xext[:, tq:, :]
        qkv_parts.append(_silu(acc).reshape(pr, D_QKV))
    z = cat(z_parts, 0)
    zba = cat(zba_parts, 0)
    qkv = cat(qkv_parts, 0)

    gpar = gpar_ref[...]
    beta_all = _sigmoid(zba)
    g_all = -jnp.exp(gpar[0:1, :]) * _softplus(zba + gpar[1:2, :])

    sub = GDN_SUB
    log2c = _log2(c)
    row = lax.broadcasted_iota(jnp.int32, (sub, sub), 0)
    col = lax.broadcasted_iota(jnp.int32, (sub, sub), 1)
    cblk = lax.shift_right_logical(col, log2c)
    same = lax.shift_right_logical(row, log2c) == cblk
    tril = same & (row >= col)
    strict = same & (row > col)
    tril_f = tril.astype(F32)
    eye = (row == col).astype(F32)
    cblk_row = cblk[0:1, :]
    nblk = sub // c
    blk_per_seq = tq // c
    nsub = R // sub
    pairs = [(st, h) for st in range(nsub) for h in range(GDN_HEADS)]
    gc_alls, glasts, grems, gc_ts = [], [], [], []
    for st in range(nsub):
        gc_all = _dot_exact(tril_f, g_all[st * sub:(st + 1) * sub])
        glast = [gc_all[(j + 1) * c - 1:(j + 1) * c, :] for j in range(nblk)]
        gc_alls.append(gc_all)
        glasts.append(glast)
        grems.append(cat([jnp.broadcast_to(gl, (c, LANES)) for gl in glast], 0) - gc_all)
        gc_ts.append(gc_all.T)
    qs, ks, vs, betas, egs, decays, qkks = {}, {}, {}, {}, {}, {}, {}
    for p in pairs:
        st, h = p
        r0, lo = st * sub, h * GDN_DK
        q = qkv[r0:r0 + sub, lo:lo + GDN_DK]
        k = qkv[r0:r0 + sub, D_A + lo:D_A + lo + GDN_DK]
        qs[p] = q * lax.rsqrt(jnp.sum(q * q, axis=-1, keepdims=True) + EPS) * (GDN_DK ** -0.5)
        ks[p] = k * lax.rsqrt(jnp.sum(k * k, axis=-1, keepdims=True) + EPS)
        vs[p] = qkv[r0:r0 + sub, 2 * D_A + lo:2 * D_A + lo + GDN_DK]
        betas[p] = beta_all[r0:r0 + sub, h:h + 1]
        gc = gc_alls[st][:, 4 + h:5 + h]
        egs[p] = jnp.exp(gc)
        decays[p] = jnp.where(tril, jnp.exp(jnp.where(tril, gc - gc_ts[st][4 + h:5 + h, :], 0.0)), 0.0)
    for p in pairs:
        kb = ks[p].astype(BF16)
        qkks[p] = _dot_nt(jnp.concatenate([qs[p].astype(BF16), kb], axis=0), kb)
    xks, tinvs, qkds = {}, {}, {}
    for p in pairs:
        qkds[p] = (qkks[p][:sub] * decays[p]).astype(BF16)
        xks[p] = jnp.where(strict, -(betas[p] * qkks[p][sub:] * decays[p]), 0.0)
        tinvs[p] = eye + xks[p]
    if log2c > 1:
        for p in pairs:
            xb = xks[p].astype(BF16)
            xks[p] = _dot(xb, xb)
    for lev in range(1, log2c):
        res = {}
        for p in pairs:
            xb = xks[p].astype(BF16)
            if lev < log2c - 1:
                res[p] = _dot(jnp.concatenate([xb, tinvs[p].astype(BF16)], axis=0), xb)
            else:
                res[p] = _dot(tinvs[p].astype(BF16), xb)
        for p in pairs:
            if lev < log2c - 1:
                xks[p] = res[p][:sub]
                tinvs[p] = tinvs[p] + res[p][sub:]
            else:
                tinvs[p] = tinvs[p] + res[p]
    uws, q_heads, kts = {}, {}, {}
    for p in pairs:
        rhs = jnp.concatenate([(vs[p] * betas[p]).astype(BF16),
                               (ks[p] * (betas[p] * egs[p])).astype(BF16)], axis=1)
        uws[p] = _dot(tinvs[p].astype(BF16), rhs)
        q_heads[p] = qs[p] * egs[p]
        st, h = p
        kts[p] = (ks[p] * jnp.exp(grems[st][:, 4 + h:5 + h])).T
    u_lists = {p: [] for p in pairs}
    qs_lists = {p: [] for p in pairs}
    for kstep in range(blk_per_seq):
        items = []
        for seq in range(nseq):
            st, j = divmod((seq * tq + kstep * c) // c, nblk)
            items += [(seq, st, j, h) for h in range(GDN_HEADS)]
        s_olds, tmps = {}, {}
        for it in items:
            seq, st, j, h = it
            p = (st, h)
            sl = slice(j * c, (j + 1) * c)
            s_olds[it] = delta_ref[seq, h]
            both = jnp.concatenate([uws[p][sl, GDN_DK:], q_heads[p][sl]], axis=0).astype(BF16)
            tmps[it] = _dot(both, s_olds[it].astype(BF16))
        for it in items:
            seq, st, j, h = it
            p = (st, h)
            sl = slice(j * c, (j + 1) * c)
            u_base = uws[p][:, :GDN_DK]
            u_lists[p].append(u_base[sl] - tmps[it][:c])
            qs_lists[p].append(tmps[it][c:])
            u_full = cat(u_lists[p] + ([u_base[(j + 1) * c:]] if j + 1 < nblk else []), 0)
            kt_j = jnp.where(cblk_row == j, kts[p], 0.0).astype(BF16)
            delta_ref[seq, h] = (s_olds[it] * jnp.exp(glasts[st][j][:, 4 + h:5 + h])
                                 + _dot(kt_j, u_full.astype(BF16)))
    o_raw = {}
    for p in pairs:
        o_raw[p] = cat(qs_lists[p], 0) + _dot(qkds[p], cat(u_lists[p], 0).astype(BF16))
    o_tiles = []
    for st in range(nsub):
        o_heads = []
        for h in range(GDN_HEADS):
            zg = z[st * sub:(st + 1) * sub, D_QKV + h * GDN_DK:D_QKV + (h + 1) * GDN_DK]
            o_heads.append(_rms(o_raw[(st, h)], normo_ref[...]) * _silu(zg))
        o_tiles.append(jnp.concatenate(o_heads, axis=1))
    o_a = cat(o_tiles, 0)

    zu = z[:, D_QKV + D_A:D_QKV + D_A + D_B]
    zv = z[:, D_QKV + D_A + D_B:]
    u_act = _gelu_tanh(zu)
    gv = _gelu_tanh(zv)
    vc = gv - jnp.mean(gv, axis=-1, keepdims=True)
    var = jnp.mean(vc * vc, axis=-1, keepdims=True)
    v_ln = vc * lax.rsqrt(var + EPS) * lng_ref[...] + lnb_ref[...]
    rv = vrows_ref.shape[1]
    vrows_ref[...] = v_ln.reshape(nseq, tq, D_B)[:, tq - rv:, :]
    log2cs = _log2(cs)
    srow = lax.broadcasted_iota(jnp.int32, (SGU_CHUNK, SGU_CHUNK), 0)
    scol = lax.broadcasted_iota(jnp.int32, (SGU_CHUNK, SGU_CHUNK), 1)
    smask = (lax.shift_right_logical(srow, log2cs) == lax.shift_right_logical(scol, log2cs)) & (srow >= scol)
    v_bf = v_ln.astype(BF16)
    mixed_groups = []
    for g in range(SGU_GROUPS):
        wg = jnp.where(smask, wsp_ref[g], 0.0).astype(BF16)
        bg = bsp_ref[g]
        parts = []
        for r0 in range(0, R, SGU_CHUNK):
            parts.append(_dot(wg, v_bf[r0:r0 + SGU_CHUNK, g * LANES:(g + 1) * LANES]) + bg)
        mixed_groups.append(parts[0] if len(parts) == 1 else jnp.concatenate(parts, axis=0))
    o_b = u_act * jnp.concatenate(mixed_groups, axis=1)

    y_in = jnp.concatenate([o_a, o_b], axis=1).astype(BF16)
    for i in range(R // pr):
        seqs = slice(i * spp, (i + 1) * spp)
        proj = _dot(y_in[i * pr:(i + 1) * pr], wout_ref[...])
        hout_ref[seqs] = h_ref[seqs] + proj.reshape(spp, tq, D_MODEL)


def _ab_layer(h, states, w, *, nseq, tq, c, cs, casts=()):
    nb, t, _ = h.shape
    has_state = states is not None
    grid = (nb // nseq, t // tq)
    nsteps = grid[0] * grid[1]

    def slab(a):
        rows = a.shape[-2] // nsteps
        assert rows * nsteps == a.shape[-2] and rows % (2 * SUBLANES) == 0, a.shape
        lead = a.ndim - 2
        return pl.BlockSpec(a.shape[:lead] + (rows, a.shape[-1]),
                            lambda b, s: (0,) * lead + (b * grid[1] + s, 0))
    rv = min(SGU_CHUNK, tq)
    seq_blk = lambda shape: pl.BlockSpec(shape, lambda b, s: (b,) + (0,) * (len(shape) - 1))
    full = lambda a: pl.BlockSpec(a.shape, lambda b, s: (0,) * a.ndim)
    params = [w['gmix0'], w['win_qkvg'], w['win_uv'], w['wba'], w['convw'], w['gpar'], w['normo'], w['lng'], w['lnb'],
              w['wsp'], w['bsp'], w['wout_ab']]
    out_shapes = (
        jax.ShapeDtypeStruct(h.shape, F32),
        jax.ShapeDtypeStruct((nb, SUBLANES, D_QKV), F32),
        jax.ShapeDtypeStruct((nb, GDN_HEADS, GDN_DK, GDN_DK), F32),
        jax.ShapeDtypeStruct((nb, rv, D_B), F32),
    ) + tuple(jax.ShapeDtypeStruct(a.shape, BF16) for a in casts)
    outs = pl.pallas_call(
        functools.partial(_ab_body, nseq=nseq, tq=tq, c=c, cs=cs, has_state=has_state, n_cast=len(casts)),
        grid=grid,
        in_specs=[pl.BlockSpec((nseq, tq, D_MODEL), lambda b, s: (b, s, 0))]
        + ([seq_blk((nseq, GDN_CONV - 1, D_QKV)), seq_blk((nseq, GDN_HEADS, GDN_DK, GDN_DK))]
           if has_state else []) + [full(a) for a in params] + [slab(a) for a in casts],
        out_specs=(pl.BlockSpec((nseq, tq, D_MODEL), lambda b, s: (b, s, 0)),
                   seq_blk((nseq, SUBLANES, D_QKV)),
                   seq_blk((nseq, GDN_HEADS, GDN_DK, GDN_DK)),
                   seq_blk((nseq, rv, D_B))) + tuple(slab(a) for a in casts),
        out_shape=out_shapes,
        compiler_params=pltpu.CompilerParams(
            dimension_semantics=("arbitrary", "arbitrary"), vmem_limit_bytes=VMEM_LIMIT_BYTES),
        name="ab_mixer",
    )(h, *(states if has_state else ()), *params, *casts)
    return outs[:4], outs[4:]


def _cd_body(*refs, nseq, tq, pos0, has_state):
    h_ref, refs = refs[0], refs[1:]
    if has_state:
        (pool0_ref, s5re0_ref, s5im0_ref), refs = refs[:3], refs[3:]
    (gmix_ref, win_ref, wpool_ref, pscale_ref, wb_ref, lam_re_ref, lam_im_ref, wcre_ref, wcim_ref,
     dskip_ref, wglu_ref, bglu_ref, wout_ref, hout_ref, pool_ref, s5re_ref, s5im_ref, bu_ref) = refs
    R = nseq * tq
    t_idx = pl.program_id(1)

    @pl.when(t_idx == 0)
    def _init():
        if has_state:
            pool_ref[...] = pool0_ref[...]
            s5re_ref[...] = s5re0_ref[...]
            s5im_ref[...] = s5im0_ref[...]
        else:
            pool_ref[...] = jnp.zeros(pool_ref.shape, F32)
            s5re_ref[...] = jnp.zeros(s5re_ref.shape, F32)
            s5im_ref[...] = jnp.zeros(s5im_ref.shape, F32)

    x3 = h_ref[...]
    xn3 = _rms(x3, gmix_ref[...]).astype(BF16)
    nper = R // CD_PERM_ROWS
    tsub = tq // nper
    log2n = _log2(nseq)
    ri = lax.broadcasted_iota(jnp.int32, (CD_PERM_ROWS, CD_PERM_ROWS), 0)
    ci = lax.broadcasted_iota(jnp.int32, (CD_PERM_ROWS, CD_PERM_ROWS), 1)
    to_tm = (ci == (ri & (nseq - 1)) * tsub + lax.shift_right_logical(ri, log2n)).astype(BF16)
    to_sm = (ri == (ci & (nseq - 1)) * tsub + lax.shift_right_logical(ci, log2n)).astype(BF16)
    xt = jnp.concatenate(
        [_dot(to_tm, xn3[:, i * tsub:(i + 1) * tsub, :].reshape(CD_PERM_ROWS, D_MODEL)).astype(BF16)
         for i in range(nper)], axis=0)
    z = _dot(xt, win_ref[...])
    xc = z[:, :D_C].reshape(tq, nseq, D_C)
    xd = z[:, D_C:]

    xext = jnp.concatenate([pool_ref[...], xc], axis=0)
    pool_ref[...] = xext[tq:]
    pos = pos0 + t_idx * tq + lax.broadcasted_iota(jnp.int32, (tq, 1, 1), 0)
    sums = xext
    outs = []
    for gi, win in enumerate(POOL_WINDOWS):
        sums = sums[win // 2:] + sums[:-(win // 2)]
        cnt = jnp.minimum(win, pos + 1).astype(F32)
        sl = slice(gi * LANES, (gi + 1) * LANES)
        m = sums[sums.shape[0] - tq:, :, sl] / cnt - xc[:, :, sl]
        outs.append(_dot(m.reshape(R, LANES).astype(BF16), wpool_ref[gi]))
    y_c = jnp.concatenate(outs, axis=1) * pscale_ref[...]

    xd_bf = xd.astype(BF16)
    for half in range(2):
        res = _dot(xd_bf[:, half * 256:(half + 1) * 256], wb_ref[half])
        bu_ref[:, half * S5_HALF:(half + 1) * S5_HALF] = res[:, :S5_HALF]
        bu_ref[:, S5_MODES + half * S5_HALF:S5_MODES + (half + 1) * S5_HALF] = res[:, S5_HALF:]
    piece = S5_SCAN_VREGS * SUBLANES * LANES // (2 * nseq)
    pieces = range(0, S5_MODES, piece)
    lam = {p0: (jnp.broadcast_to(lam_re_ref[:, p0:p0 + piece], (nseq, piece)),
                jnp.broadcast_to(lam_im_ref[:, p0:p0 + piece], (nseq, piece))) for p0 in pieces}
    carry = {p0: (s5re_ref[:, p0:p0 + piece], s5im_ref[:, p0:p0 + piece]) for p0 in pieces}
    kn = wcre_ref.shape[1]
    for i in range(nper):
        rows = slice(i * CD_PERM_ROWS, (i + 1) * CD_PERM_ROWS)
        for p0 in pieces:
            lr, li = lam[p0]
            re, im = carry[p0]
            for t in range(i * tsub, (i + 1) * tsub):
                trows = slice(t * nseq, (t + 1) * nseq)
                re, im = (lr * re - li * im + bu_ref[trows, p0:p0 + piece],
                          lr * im + li * re + bu_ref[trows, S5_MODES + p0:S5_MODES + p0 + piece])
                bu_ref[trows, p0:p0 + piece] = re
                bu_ref[trows, S5_MODES + p0:S5_MODES + p0 + piece] = im
            carry[p0] = (re, im)
        ys = []
        for n in range(wcre_ref.shape[0]):
            s_re = bu_ref[rows, n * kn:(n + 1) * kn].astype(BF16)
            s_im = bu_ref[rows, S5_MODES + n * kn:S5_MODES + (n + 1) * kn].astype(BF16)
            ys.append(_dot(s_re, wcre_ref[n]) - _dot(s_im, wcim_ref[n]))
        y = jnp.concatenate(ys, axis=1) + dskip_ref[...] * xd[rows]
        y = _gelu_tanh(y)
        y_d = y * _sigmoid(_dot(y.astype(BF16), wglu_ref[...]) + bglu_ref[...])
        y_in = jnp.concatenate([y_c[rows], y_d], axis=1).astype(BF16)
        proj = _dot(_dot(to_sm, y_in).astype(BF16), wout_ref[...])
        ts = slice(i * tsub, (i + 1) * tsub)
        hout_ref[:, ts, :] = x3[:, ts, :] + proj.reshape(nseq, tsub, D_MODEL)
    for p0 in pieces:
        s5re_ref[:, p0:p0 + piece], s5im_ref[:, p0:p0 + piece] = carry[p0]


def _cd_layer(h, states, w, *, nseq, tq, pos0):
    nb, t, _ = h.shape
    grid = (nb // nseq, t // tq)
    full = lambda a: pl.BlockSpec(a.shape, lambda b, s: (0,) * a.ndim)
    params = [w['gmix1'], w['win_cd'], w['wpool'], w['pscale'], w['wb'], w['lam_re'], w['lam_im'],
              w['wcre'], w['wcim'], w['dskip'], w['wglu'], w['bglu'], w['wout_cd']]
    h_spec = pl.BlockSpec((nseq, tq, D_MODEL), lambda b, s: (b, s, 0))
    pool_spec = pl.BlockSpec((2 * SUBLANES, nseq, D_C), lambda b, s: (0, b, 0))
    s5_spec = pl.BlockSpec((nseq, S5_MODES), lambda b, s: (b, 0))
    state_specs = [pool_spec, s5_spec, s5_spec]
    has_state = states is not None
    out_shapes = (
        jax.ShapeDtypeStruct(h.shape, F32),
        jax.ShapeDtypeStruct((2 * SUBLANES, nb, D_C), F32),
        jax.ShapeDtypeStruct((nb, S5_MODES), F32),
        jax.ShapeDtypeStruct((nb, S5_MODES), F32),
    )
    return pl.pallas_call(
        functools.partial(_cd_body, nseq=nseq, tq=tq, pos0=pos0, has_state=has_state),
        grid=grid,
        in_specs=[h_spec] + (state_specs if has_state else []) + [full(a) for a in params],
        out_specs=tuple([h_spec] + state_specs),
        out_shape=out_shapes,
        scratch_shapes=[pltpu.VMEM((nseq * tq, 2 * S5_MODES), F32)],
        compiler_params=pltpu.CompilerParams(
            dimension_semantics=("arbitrary", "arbitrary"), vmem_limit_bytes=VMEM_LIMIT_BYTES),
        name="cd_mixer",
    )(h, *(states if has_state else ()), *params)


def _ffn_pe_body(h_ref, p_ref, gffn_ref, wup_ref, wdn_ref, gpe_ref, wgate_ref, wproj_ref, gfin_ref,
                 out_ref, *, final):
    x = h_ref[...]
    xn = _rms(x, gffn_ref[...]).astype(BF16)
    h = x
    ffc = D_FF // 4
    for ci in range(4):
        a = jnp.maximum(_dot(xn, wup_ref[:, ci * ffc:(ci + 1) * ffc]), 0.0)
        h = h + _dot((a * a).astype(BF16), wdn_ref[ci * ffc:(ci + 1) * ffc, :])
    hn = _rms(h, gpe_ref[...]).astype(BF16)
    gate = _sigmoid(_dot(hn, wgate_ref[...]))
    h = h + _dot(p_ref[...].astype(BF16), wproj_ref[...]) * gate
    if final:
        h = _rms(h, gfin_ref[...])
    out_ref[...] = h


def _ffn_pe(h2, p3, w, layer, *, tm, final):
    n = h2.shape[0]
    of_layer = lambda a: pl.BlockSpec((None,) + a.shape[1:], lambda i: (layer,) + (0,) * (a.ndim - 1),
                                      pipeline_mode=pl.Buffered(1))
    params = [w['gffn'], w['wup'], w['wdn'], w['gpe'], w['wgate'], w['wproj']]
    return pl.pallas_call(
        functools.partial(_ffn_pe_body, final=final),
        grid=(n // tm,),
        in_specs=[pl.BlockSpec((tm, D_MODEL), lambda i: (i, 0)),
                  pl.BlockSpec((None, tm, D_PLE), lambda i: (layer, i, 0))]
        + [of_layer(a) for a in params] + [pl.BlockSpec(w['gfin'].shape, lambda i: (0, 0))],
        out_specs=pl.BlockSpec((tm, D_MODEL), lambda i: (i, 0)),
        out_shape=jax.ShapeDtypeStruct(h2.shape, F32),
        compiler_params=pltpu.CompilerParams(
            dimension_semantics=("arbitrary",), vmem_limit_bytes=VMEM_LIMIT_BYTES),
        name="ffn_pe",
    )(h2, p3, *params, w['gfin'])


def _s5_prep_body(lre_ref, lim_ref, ldt_ref, bre_ref, bim_ref, lbre_ref, lbim_ref, bbre_ref, bbim_ref):
    lr = lre_ref[...]
    li = lim_ref[...]
    dt = jnp.exp(ldt_ref[...])
    mag = jnp.exp(lr * dt)
    ang = li * dt
    lb_re = mag * jnp.cos(ang)
    lb_im = mag * jnp.sin(ang)
    lbre_ref[...] = lb_re
    lbim_ref[...] = lb_im
    a = lb_re - 1.0
    b = lb_im
    den = lr * lr + li * li
    cr = (a * lr + b * li) / den
    ci = (b * lr - a * li) / den
    bre = bre_ref[...]
    bim = bim_ref[...]
    bbre_ref[...] = cr * bre - ci * bim
    bbim_ref[...] = cr * bim + ci * bre


def _s5_prep(lam_re, lam_im, log_dt, b_re, b_im):
    ldt = jnp.broadcast_to(log_dt[:, None], (S5_GROUPS, S5_STATE))
    rows = [a.reshape(1, S5_MODES) for a in (lam_re, lam_im, ldt)]
    chan_major = lambda a: a.reshape(S5_MODES, S5_GW).T
    shp = jax.ShapeDtypeStruct((S5_GW, S5_MODES), F32)
    row_shp = jax.ShapeDtypeStruct((1, S5_MODES), F32)
    lb_re, lb_im, bb_re, bb_im = pl.pallas_call(
        _s5_prep_body, out_shape=(row_shp, row_shp, shp, shp), name="s5_prep",
    )(*rows, chan_major(b_re), chan_major(b_im))
    to_gcn = lambda a: jnp.swapaxes(a.reshape(S5_GW, S5_GROUPS, S5_STATE), 0, 1)
    return lb_re, lb_im, to_gcn(bb_re), to_gcn(bb_im)


def _block_diag(a, ngrp):
    p, q = a.shape[1], a.shape[2]
    eye = jnp.eye(ngrp, dtype=a.dtype)
    a = a.reshape(S5_GROUPS // ngrp, ngrp, p, q)
    return jnp.einsum('hgpq,gk->hgpkq', a, eye).reshape(S5_GROUPS // ngrp, ngrp * p, ngrp * q)


def _prepare_weights(norm_mix, norm_ffn, norm_pe, norm_final, w_in_ab, conv_qkv, a_log, dt_bias, norm_o,
                     ln_v_gain, ln_v_bias, w_spatial, b_spatial, w_out_ab, w_in_cd, w_pool, pool_scale,
                     lam_re, lam_im, log_dt, b_re, b_im, c_re, c_im, d_skip, w_glu, b_glu, w_out_cd,
                     w_ffn_up, w_ffn_down, w_pe_proj, w_pe_gate):
    row = lambda a: a.reshape(1, -1)
    w = {}
    w['gmix0'] = row(norm_mix[0])
    w['gmix1'] = row(norm_mix[1])
    wi = w_in_ab[0]
    cut0 = D_QKV + D_A
    cut1 = cut0 + 2 * GDN_HEADS
    w['win_qkvg'] = wi[:, :cut0].astype(BF16)
    w['win_uv'] = wi[:, cut1:].astype(BF16)
    w['wba'] = jnp.pad(wi[:, cut0:cut1], ((0, 0), (0, LANES - 2 * GDN_HEADS))).astype(BF16)
    w['convw'] = jnp.pad(conv_qkv[0], ((0, SUBLANES - 4), (0, 0)))
    w['gpar'] = jnp.pad(jnp.concatenate([a_log, dt_bias], axis=0),
                        ((0, 0), (GDN_HEADS, LANES - 2 * GDN_HEADS)))
    w['normo'] = row(norm_o[0])
    w['lng'] = row(ln_v_gain[0])
    w['lnb'] = row(ln_v_bias[0])
    w['wout_ab'] = w_out_ab[0].astype(BF16)
    w['w_spatial'] = w_spatial[0]
    w['b_spatial'] = b_spatial[0]
    w['wpool'] = w_pool[0].astype(BF16)
    w['pscale'] = row(pool_scale[0])
    lb_re, lb_im, bb_re, bb_im = _s5_prep(lam_re[0], lam_im[0], log_dt[0], b_re[0], b_im[0])
    w['lam_re'] = lb_re
    w['lam_im'] = lb_im
    half = S5_GROUPS // 2
    w['wb'] = jnp.concatenate([_block_diag(bb_re, half), _block_diag(bb_im, half)],
                              axis=2).astype(BF16)
    per_tile = LANES // S5_GW
    w['wcre'] = _block_diag(jnp.swapaxes(c_re[0], 1, 2), per_tile).astype(BF16)
    w['wcim'] = _block_diag(jnp.swapaxes(c_im[0], 1, 2), per_tile).astype(BF16)
    w['dskip'] = row(d_skip[0])
    w['bglu'] = row(b_glu[0])
    w['gffn'] = norm_ffn.reshape(2, 1, D_MODEL)
    w['gpe'] = norm_pe.reshape(2, 1, D_MODEL)
    w['gfin'] = row(norm_final)
    w['wproj'] = w_pe_proj.astype(BF16)
    return w


def _mix_ab(x, states, w, *, tile, casts=()):
    nseq, tq = tile
    c = min(64, tq)
    cs = min(SGU_CHUNK, tq)
    reps = SGU_CHUNK // cs
    wl = dict(w)
    wl['wsp'] = jnp.tile(w['w_spatial'][:, :cs, :cs], (1, reps, reps))
    wl['bsp'] = jnp.tile(w['b_spatial'][:, :cs], (1, reps))[:, :, None]
    (h, conv_o, delta_o, vrows), cast_out = _ab_layer(x, states, wl, nseq=nseq, tq=tq, c=c, cs=cs, casts=casts)
    return h, (conv_o[None, :, SUBLANES - (GDN_CONV - 1):, :], delta_o[None], vrows[None]), cast_out


def _mix_cd(h, states, pos0, w, *, tile):
    nb = h.shape[0]
    if states is not None:
        pool0, s5re0, s5im0 = states
        pool_in = jnp.swapaxes(jnp.pad(pool0, ((0, 0), (2 * SUBLANES - POOL_BUF, 0), (0, 0))), 0, 1)
        states = (pool_in, s5re0.reshape(nb, S5_MODES), s5im0.reshape(nb, S5_MODES))
    h, pool_o, s5re, s5im = _cd_layer(h, states, w, nseq=tile[0], tq=tile[1], pos0=pos0)
    pool_o = jnp.swapaxes(pool_o, 0, 1)[:, 2 * SUBLANES - POOL_BUF:, :]
    return h, (pool_o[None], s5re.reshape(1, nb, S5_GROUPS, S5_STATE), s5im.reshape(1, nb, S5_GROUPS, S5_STATE))


def kernel(x_prompt, x_sample, state_conv, state_delta, state_pool, state_s5_re, state_s5_im, p_prompt, p_sample, norm_mix, norm_ffn, norm_pe, norm_final, w_in_ab, conv_qkv, a_log, dt_bias, norm_o, ln_v_gain, ln_v_bias, w_spatial, b_spatial, w_out_ab, w_in_cd, w_pool, pool_scale, lam_re, lam_im, log_dt, b_re, b_im, c_re, c_im, d_skip, w_glu, b_glu, w_out_cd, w_ffn_up, w_ffn_down, w_pe_proj, w_pe_gate):
    w = _prepare_weights(norm_mix, norm_ffn, norm_pe, norm_final, w_in_ab, conv_qkv, a_log, dt_bias, norm_o,
                         ln_v_gain, ln_v_bias, w_spatial, b_spatial, w_out_ab, w_in_cd, w_pool, pool_scale,
                         lam_re, lam_im, log_dt, b_re, b_im, c_re, c_im, d_skip, w_glu, b_glu, w_out_cd,
                         w_ffn_up, w_ffn_down, w_pe_proj, w_pe_gate)
    bp, tp, _ = x_prompt.shape
    bs, ts, _ = x_sample.shape
    flat = lambda a: a.reshape(-1, a.shape[-1])
    pp3 = p_prompt.reshape(p_prompt.shape[0], bp * tp, D_PLE)
    ps3 = p_sample.reshape(p_sample.shape[0], bs * ts, D_PLE)
    later = (w_ffn_up, w_ffn_down, w_pe_gate, w_in_cd[0], w_out_cd[0], w_glu[0])
    hp, ab_p, later_bf16 = _mix_ab(x_prompt, None, w, tile=(4, 128), casts=later)
    w['wup'], w['wdn'], w['wgate'], w['win_cd'], w['wout_cd'], w['wglu'] = later_bf16
    hs, ab_s, _ = _mix_ab(x_sample, (state_conv[0], state_delta[0]), w, tile=(16, 8))
    hp = _ffn_pe(flat(hp), pp3, w, 0, tm=FFN_ROWS, final=False)
    hs = _ffn_pe(flat(hs), ps3, w, 0, tm=FFN_ROWS // 2, final=False)
    hp, cd_p = _mix_cd(hp.reshape(bp, tp, D_MODEL), None, 0, w, tile=(8, 64))
    hs, cd_s = _mix_cd(hs.reshape(bs, ts, D_MODEL), (state_pool[0], state_s5_re[0], state_s5_im[0]),
                       PAST_LEN, w, tile=(32, 8))
    yp = _ffn_pe(flat(hp), pp3, w, 1, tm=FFN_ROWS, final=True)
    ys = _ffn_pe(flat(hs), ps3, w, 1, tm=FFN_ROWS // 2, final=True)
    return (yp.reshape(bp, tp, D_MODEL), ys.reshape(bs, ts, D_MODEL)) + ab_p + cd_p + ab_s + cd_s
```

```python
import functools
import math

import jax
import jax.numpy as jnp
from jax import lax
from jax.experimental import pallas as pl
from jax.experimental.pallas import tpu as pltpu

F32 = jnp.float32
BF16 = jnp.bfloat16
HIGHEST = lax.Precision.HIGHEST

EPS = 1e-6
D_MODEL = 1024
PAST_LEN = 16384
GDN_HEADS = 4
GDN_DK = 128
GDN_CONV = 4
D_A = 512
D_QKV = 1536
D_B = 512
SGU_GROUPS = 4
SGU_CHUNK = 128
POOL_WINDOWS = (2, 4, 8, 16)
POOL_BUF = 15
D_C = 512
D_D = 512
S5_GROUPS = 32
S5_STATE = 64
S5_GW = 16
D_FF = 4096
D_PLE = 256
S5_MODES = S5_GROUPS * S5_STATE
S5_HALF = S5_MODES // 2
LANES = 128
SUBLANES = 8
GDN_SUB = 128
S5_SCAN_VREGS = 16
CD_PERM_ROWS = 256
AB_PROJ_ROWS = 256
FFN_ROWS = 1024
VMEM_LIMIT_BYTES = 56 * 1024 * 1024


def _dot(a, b):
    return jnp.dot(a, b, preferred_element_type=F32)


def _dot_nt(a, b):
    return lax.dot_general(a, b, (((1,), (1,)), ((), ())), preferred_element_type=F32)


def _dot_exact(a, b):
    return jnp.dot(a, b, precision=HIGHEST, preferred_element_type=F32)


def _rms(x, g):
    return x * lax.rsqrt(jnp.mean(x * x, axis=-1, keepdims=True) + EPS) * g


def _sigmoid(x):
    return 0.5 * jnp.tanh(0.5 * x) + 0.5


def _silu(x):
    return x * _sigmoid(x)


def _gelu_tanh(x):
    return 0.5 * x * (1.0 + jnp.tanh(math.sqrt(2.0 / math.pi) * (x + 0.044715 * (x * x * x))))


def _softplus(x):
    return jnp.maximum(x, 0.0) + jnp.log(1.0 + jnp.exp(-jnp.abs(x)))


def _log2(n):
    l = n.bit_length() - 1
    assert (1 << l) == n, n
    return l


def _ab_body(*refs, nseq, tq, c, cs, has_state, n_cast):
    h_ref, refs = refs[0], refs[1:]
    if has_state:
        (conv0_ref, delta0_ref), refs = refs[:2], refs[2:]
    (gmix_ref, win_qkvg_ref, win_uv_ref, wba_ref, convw_ref, gpar_ref, normo_ref, lng_ref, lnb_ref, wsp_ref,
     bsp_ref, wout_ref), refs = refs[:12], refs[12:]
    cast_in, refs = refs[:n_cast], refs[n_cast:]
    (hout_ref, conv_ref, delta_ref, vrows_ref), cast_out = refs[:4], refs[4:]
    for src_ref, dst_ref in zip(cast_in, cast_out):
        dst_ref[...] = src_ref[...].astype(BF16)
    R = nseq * tq
    t_idx = pl.program_id(1)

    @pl.when(t_idx == 0)
    def _init():
        conv_ref[...] = jnp.zeros(conv_ref.shape, F32)
        if has_state:
            conv_ref[:, SUBLANES - (GDN_CONV - 1):, :] = conv0_ref[...]
            delta_ref[...] = delta0_ref[...]
        else:
            delta_ref[...] = jnp.zeros(delta_ref.shape, F32)

    cat = lambda parts, axis: parts[0] if len(parts) == 1 else jnp.concatenate(parts, axis=axis)
    pr = min(R, AB_PROJ_ROWS)
    spp = pr // tq
    cw = convw_ref[...]
    z_parts, zba_parts, qkv_parts = [], [], []
    for i in range(R // pr):
        xn = _rms(h_ref[i * spp:(i + 1) * spp].reshape(pr, D_MODEL), gmix_ref[...]).astype(BF16)
        zi = jnp.concatenate([_dot(xn, win_qkvg_ref[...]), _dot(xn, win_uv_ref[...])], axis=1)
        z_parts.append(zi)
        zba_parts.append(_dot(xn, wba_ref[...]))
        pre = zi[:, :D_QKV].reshape(spp, tq, D_QKV)
        xext = jnp.concatenate([conv_ref[i * spp:(i + 1) * spp], pre], axis=1)
        acc = pre * cw[3:4, :]
        for k in range(1, GDN_CONV):
            acc = acc + pltpu.roll(xext, k, axis=1)[:, SUBLANES:, :] * cw[3 - k:4 - k, :]
        conv_ref[i * spp:(i + 1) * spp] = xext[:, tq:, :]
        qkv_parts.append(_silu(acc).reshape(pr, D_QKV))
    z = cat(z_parts, 0)
    zba = cat(zba_parts, 0)
    qkv = cat(qkv_parts, 0)

    gpar = gpar_ref[...]
    beta_all = _sigmoid(zba)
    g_all = -jnp.exp(gpar[0:1, :]) * _softplus(zba + gpar[1:2, :])

    sub = GDN_SUB
    log2c = _log2(c)
    row = lax.broadcasted_iota(jnp.int32, (sub, sub), 0)
    col = lax.broadcasted_iota(jnp.int32, (sub, sub), 1)
    cblk = lax.shift_right_logical(col, log2c)
    same = lax.shift_right_logical(row, log2c) == cblk
    tril = same & (row >= col)
    strict = same & (row > col)
    tril_f = tril.astype(F32)
    eye = (row == col).astype(F32)
    cblk_row = cblk[0:1, :]
    nblk = sub // c
    blk_per_seq = tq // c
    nsub = R // sub
    pairs = [(st, h) for st in range(nsub) for h in range(GDN_HEADS)]
    gl = lambda h: slice(GDN_HEADS + h, GDN_HEADS + h + 1)
    gc_alls, glasts, grems, gc_ts = [], [], [], []
    for st in range(nsub):
        gc_all = _dot_exact(tril_f, g_all[st * sub:(st + 1) * sub])
        glast = [gc_all[(j + 1) * c - 1:(j + 1) * c, :] for j in range(nblk)]
        gc_alls.append(gc_all)
        glasts.append(glast)
        grems.append(cat([jnp.broadcast_to(gl, (c, LANES)) for gl in glast], 0) - gc_all)
        gc_ts.append(gc_all.T)
    qs, ks, vs, betas, egs, decays, qkks = {}, {}, {}, {}, {}, {}, {}
    for p in pairs:
        st, h = p
        r0, lo = st * sub, h * GDN_DK
        q = qkv[r0:r0 + sub, lo:lo + GDN_DK]
        k = qkv[r0:r0 + sub, D_A + lo:D_A + lo + GDN_DK]
        qs[p] = q * lax.rsqrt(jnp.sum(q * q, axis=-1, keepdims=True) + EPS) * (GDN_DK ** -0.5)
        ks[p] = k * lax.rsqrt(jnp.sum(k * k, axis=-1, keepdims=True) + EPS)
        vs[p] = qkv[r0:r0 + sub, 2 * D_A + lo:2 * D_A + lo + GDN_DK]
        betas[p] = beta_all[r0:r0 + sub, h:h + 1]
        gc = gc_alls[st][:, gl(h)]
        egs[p] = jnp.exp(gc)
        decays[p] = jnp.where(tril, jnp.exp(jnp.where(tril, gc - gc_ts[st][gl(h), :], 0.0)), 0.0)
    for p in pairs:
        kb = ks[p].astype(BF16)
        qkks[p] = _dot_nt(jnp.concatenate([qs[p].astype(BF16), kb], axis=0), kb)
    xks, tinvs, qkds = {}, {}, {}
    for p in pairs:
        qkds[p] = (qkks[p][:sub] * decays[p]).astype(BF16)
        xks[p] = jnp.where(strict, -(betas[p] * qkks[p][sub:] * decays[p]), 0.0)
        tinvs[p] = eye + xks[p]
    if log2c > 1:
        for p in pairs:
            xb = xks[p].astype(BF16)
            xks[p] = _dot(xb, xb)
    for lev in range(1, log2c):
        res = {}
        for p in pairs:
            xb = xks[p].astype(BF16)
            if lev < log2c - 1:
                res[p] = _dot(jnp.concatenate([xb, tinvs[p].astype(BF16)], axis=0), xb)
            else:
                res[p] = _dot(tinvs[p].astype(BF16), xb)
        for p in pairs:
            if lev < log2c - 1:
                xks[p] = res[p][:sub]
                tinvs[p] = tinvs[p] + res[p][sub:]
            else:
                tinvs[p] = tinvs[p] + res[p]
    uws, q_heads, kts = {}, {}, {}
    for p in pairs:
        rhs = jnp.concatenate([(vs[p] * betas[p]).astype(BF16),
                               (ks[p] * (betas[p] * egs[p])).astype(BF16)], axis=1)
        uws[p] = _dot(tinvs[p].astype(BF16), rhs)
        q_heads[p] = qs[p] * egs[p]
        st, h = p
        kts[p] = (ks[p] * jnp.exp(grems[st][:, gl(h)])).T
    u_lists = {p: [] for p in pairs}
    qs_lists = {p: [] for p in pairs}
    for kstep in range(blk_per_seq):
        items = []
        for seq in range(nseq):
            st, j = divmod((seq * tq + kstep * c) // c, nblk)
            items += [(seq, st, j, h) for h in range(GDN_HEADS)]
        s_olds, tmps = {}, {}
        for it in items:
            seq, st, j, h = it
            p = (st, h)
            sl = slice(j * c, (j + 1) * c)
            s_olds[it] = delta_ref[seq, h]
            both = jnp.concatenate([uws[p][sl, GDN_DK:], q_heads[p][sl]], axis=0).astype(BF16)
            tmps[it] = _dot(both, s_olds[it].astype(BF16))
        for it in items:
            seq, st, j, h = it
            p = (st, h)
            sl = slice(j * c, (j + 1) * c)
            u_base = uws[p][:, :GDN_DK]
            u_lists[p].append(u_base[sl] - tmps[it][:c])
            qs_lists[p].append(tmps[it][c:])
            u_full = cat(u_lists[p] + ([u_base[(j + 1) * c:]] if j + 1 < nblk else []), 0)
            kt_j = jnp.where(cblk_row == j, kts[p], 0.0).astype(BF16)
            delta_ref[seq, h] = (s_olds[it] * jnp.exp(glasts[st][j][:, gl(h)])
                                 + _dot(kt_j, u_full.astype(BF16)))
    o_raw = {}
    for p in pairs:
        o_raw[p] = cat(qs_lists[p], 0) + _dot(qkds[p], cat(u_lists[p], 0).astype(BF16))
    o_tiles = []
    for st in range(nsub):
        o_heads = []
        for h in range(GDN_HEADS):
            zg = z[st * sub:(st + 1) * sub, D_QKV + h * GDN_DK:D_QKV + (h + 1) * GDN_DK]
            o_heads.append(_rms(o_raw[(st, h)], normo_ref[...]) * _silu(zg))
        o_tiles.append(jnp.concatenate(o_heads, axis=1))
    o_a = cat(o_tiles, 0)

    zu = z[:, D_QKV + D_A:D_QKV + D_A + D_B]
    zv = z[:, D_QKV + D_A + D_B:]
    u_act = _gelu_tanh(zu)
    gv = _gelu_tanh(zv)
    vc = gv - jnp.mean(gv, axis=-1, keepdims=True)
    var = jnp.mean(vc * vc, axis=-1, keepdims=True)
    v_ln = vc * lax.rsqrt(var + EPS) * lng_ref[...] + lnb_ref[...]
    rv = vrows_ref.shape[1]
    vrows_ref[...] = v_ln.reshape(nseq, tq, D_B)[:, tq - rv:, :]
    log2cs = _log2(cs)
    srow = lax.broadcasted_iota(jnp.int32, (SGU_CHUNK, SGU_CHUNK), 0)
    scol = lax.broadcasted_iota(jnp.int32, (SGU_CHUNK, SGU_CHUNK), 1)
    smask = (lax.shift_right_logical(srow, log2cs) == lax.shift_right_logical(scol, log2cs)) & (srow >= scol)
    v_bf = v_ln.astype(BF16)
    mixed_groups = []
    for g in range(SGU_GROUPS):
        wg = jnp.where(smask, wsp_ref[g], 0.0).astype(BF16)
        bg = bsp_ref[g]
        parts = []
        for r0 in range(0, R, SGU_CHUNK):
            parts.append(_dot(wg, v_bf[r0:r0 + SGU_CHUNK, g * LANES:(g + 1) * LANES]) + bg)
        mixed_groups.append(parts[0] if len(parts) == 1 else jnp.concatenate(parts, axis=0))
    o_b = u_act * jnp.concatenate(mixed_groups, axis=1)

    y_in = jnp.concatenate([o_a, o_b], axis=1).astype(BF16)
    for i in range(R // pr):
        seqs = slice(i * spp, (i + 1) * spp)
        proj = _dot(y_in[i * pr:(i + 1) * pr], wout_ref[...])
        hout_ref[seqs] = h_ref[seqs] + proj.reshape(spp, tq, D_MODEL)


def _ab_layer(h, states, w, *, nseq, tq, c, cs, casts=()):
    nb, t, _ = h.shape
    has_state = states is not None
    grid = (nb // nseq, t // tq)
    nsteps = grid[0] * grid[1]

    def slab(a):
        rows = a.shape[-2] // nsteps
        assert rows * nsteps == a.shape[-2] and rows % (2 * SUBLANES) == 0, a.shape
        lead = a.ndim - 2
        return pl.BlockSpec(a.shape[:lead] + (rows, a.shape[-1]),
                            lambda b, s: (0,) * lead + (b * grid[1] + s, 0))
    rv = min(SGU_CHUNK, tq)
    seq_blk = lambda shape: pl.BlockSpec(shape, lambda b, s: (b,) + (0,) * (len(shape) - 1))
    full = lambda a: pl.BlockSpec(a.shape, lambda b, s: (0,) * a.ndim)
    params = [w['gmix0'], w['win_qkvg'], w['win_uv'], w['wba'], w['convw'], w['gpar'], w['normo'], w['lng'], w['lnb'],
              w['wsp'], w['bsp'], w['wout_ab']]
    out_shapes = (
        jax.ShapeDtypeStruct(h.shape, F32),
        jax.ShapeDtypeStruct((nb, SUBLANES, D_QKV), F32),
        jax.ShapeDtypeStruct((nb, GDN_HEADS, GDN_DK, GDN_DK), F32),
        jax.ShapeDtypeStruct((nb, rv, D_B), F32),
    ) + tuple(jax.ShapeDtypeStruct(a.shape, BF16) for a in casts)
    outs = pl.pallas_call(
        functools.partial(_ab_body, nseq=nseq, tq=tq, c=c, cs=cs, has_state=has_state, n_cast=len(casts)),
        grid=grid,
        in_specs=[pl.BlockSpec((nseq, tq, D_MODEL), lambda b, s: (b, s, 0))]
        + ([seq_blk((nseq, GDN_CONV - 1, D_QKV)), seq_blk((nseq, GDN_HEADS, GDN_DK, GDN_DK))]
           if has_state else []) + [full(a) for a in params] + [slab(a) for a in casts],
        out_specs=(pl.BlockSpec((nseq, tq, D_MODEL), lambda b, s: (b, s, 0)),
                   seq_blk((nseq, SUBLANES, D_QKV)),
                   seq_blk((nseq, GDN_HEADS, GDN_DK, GDN_DK)),
                   seq_blk((nseq, rv, D_B))) + tuple(slab(a) for a in casts),
        out_shape=out_shapes,
        compiler_params=pltpu.CompilerParams(
            dimension_semantics=("arbitrary", "arbitrary"), vmem_limit_bytes=VMEM_LIMIT_BYTES),
        name="ab_mixer",
    )(h, *(states if has_state else ()), *params, *casts)
    return outs[:4], outs[4:]


def _cd_body(*refs, nseq, tq, pos0, has_state):
    h_ref, refs = refs[0], refs[1:]
    if has_state:
        (pool0_ref, s5re0_ref, s5im0_ref), refs = refs[:3], refs[3:]
    (gmix_ref, win_ref, wpool_ref, pscale_ref, wb_ref, lam_re_ref, lam_im_ref, wcre_ref, wcim_ref,
     dskip_ref, wglu_ref, bglu_ref, wout_ref, hout_ref, pool_ref, s5re_ref, s5im_ref, bu_ref) = refs
    R = nseq * tq
    t_idx = pl.program_id(1)

    @pl.when(t_idx == 0)
    def _init():
        if has_state:
            pool_ref[...] = pool0_ref[...]
            s5re_ref[...] = s5re0_ref[...]
            s5im_ref[...] = s5im0_ref[...]
        else:
            pool_ref[...] = jnp.zeros(pool_ref.shape, F32)
            s5re_ref[...] = jnp.zeros(s5re_ref.shape, F32)
            s5im_ref[...] = jnp.zeros(s5im_ref.shape, F32)

    x3 = h_ref[...]
    xn3 = _rms(x3, gmix_ref[...]).astype(BF16)
    nper = R // CD_PERM_ROWS
    tsub = tq // nper
    log2n = _log2(nseq)
    ri = lax.broadcasted_iota(jnp.int32, (CD_PERM_ROWS, CD_PERM_ROWS), 0)
    ci = lax.broadcasted_iota(jnp.int32, (CD_PERM_ROWS, CD_PERM_ROWS), 1)
    to_tm = (ci == (ri & (nseq - 1)) * tsub + lax.shift_right_logical(ri, log2n)).astype(BF16)
    to_sm = (ri == (ci & (nseq - 1)) * tsub + lax.shift_right_logical(ci, log2n)).astype(BF16)
    xt = jnp.concatenate(
        [_dot(to_tm, xn3[:, i * tsub:(i + 1) * tsub, :].reshape(CD_PERM_ROWS, D_MODEL)).astype(BF16)
         for i in range(nper)], axis=0)
    z = _dot(xt, win_ref[...])
    xc = z[:, :D_C].reshape(tq, nseq, D_C)
    xd = z[:, D_C:]

    xext = jnp.concatenate([pool_ref[...], xc], axis=0)
    pool_ref[...] = xext[tq:]
    pos = pos0 + t_idx * tq + lax.broadcasted_iota(jnp.int32, (tq, 1, 1), 0)
    sums = xext
    outs = []
    for gi, win in enumerate(POOL_WINDOWS):
        sums = sums[win // 2:] + sums[:-(win // 2)]
        cnt = jnp.minimum(win, pos + 1).astype(F32)
        sl = slice(gi * LANES, (gi + 1) * LANES)
        m = sums[sums.shape[0] - tq:, :, sl] / cnt - xc[:, :, sl]
        outs.append(_dot(m.reshape(R, LANES).astype(BF16), wpool_ref[gi]))
    y_c = jnp.concatenate(outs, axis=1) * pscale_ref[...]

    xd_bf = xd.astype(BF16)
    for half in range(2):
        res = _dot(xd_bf[:, half * D_D // 2:(half + 1) * D_D // 2], wb_ref[half])
        bu_ref[:, half * S5_HALF:(half + 1) * S5_HALF] = res[:, :S5_HALF]
        bu_ref[:, S5_MODES + half * S5_HALF:S5_MODES + (half + 1) * S5_HALF] = res[:, S5_HALF:]
    piece = S5_SCAN_VREGS * SUBLANES * LANES // (2 * nseq)
    pieces = range(0, S5_MODES, piece)
    lam = {p0: (jnp.broadcast_to(lam_re_ref[:, p0:p0 + piece], (nseq, piece)),
                jnp.broadcast_to(lam_im_ref[:, p0:p0 + piece], (nseq, piece))) for p0 in pieces}
    carry = {p0: (s5re_ref[:, p0:p0 + piece], s5im_ref[:, p0:p0 + piece]) for p0 in pieces}
    kn = wcre_ref.shape[1]
    for i in range(nper):
        rows = slice(i * CD_PERM_ROWS, (i + 1) * CD_PERM_ROWS)
        for p0 in pieces:
            lr, li = lam[p0]
            re, im = carry[p0]
            for t in range(i * tsub, (i + 1) * tsub):
                trows = slice(t * nseq, (t + 1) * nseq)
                re, im = (lr * re - li * im + bu_ref[trows, p0:p0 + piece],
                          lr * im + li * re + bu_ref[trows, S5_MODES + p0:S5_MODES + p0 + piece])
                bu_ref[trows, p0:p0 + piece] = re
                bu_ref[trows, S5_MODES + p0:S5_MODES + p0 + piece] = im
            carry[p0] = (re, im)
        ys = []
        for n in range(wcre_ref.shape[0]):
            s_re = bu_ref[rows, n * kn:(n + 1) * kn].astype(BF16)
            s_im = bu_ref[rows, S5_MODES + n * kn:S5_MODES + (n + 1) * kn].astype(BF16)
            ys.append(_dot(s_re, wcre_ref[n]) - _dot(s_im, wcim_ref[n]))
        y = jnp.concatenate(ys, axis=1) + dskip_ref[...] * xd[rows]
        y = _gelu_tanh(y)
        y_d = y * _sigmoid(_dot(y.astype(BF16), wglu_ref[...]) + bglu_ref[...])
        y_in = jnp.concatenate([y_c[rows], y_d], axis=1).astype(BF16)
        proj = _dot(_dot(to_sm, y_in).astype(BF16), wout_ref[...])
        ts = slice(i * tsub, (i + 1) * tsub)
        hout_ref[:, ts, :] = x3[:, ts, :] + proj.reshape(nseq, tsub, D_MODEL)
    for p0 in pieces:
        s5re_ref[:, p0:p0 + piece], s5im_ref[:, p0:p0 + piece] = carry[p0]


def _cd_layer(h, states, w, *, nseq, tq, pos0):
    nb, t, _ = h.shape
    grid = (nb // nseq, t // tq)
    full = lambda a: pl.BlockSpec(a.shape, lambda b, s: (0,) * a.ndim)
    params = [w['gmix1'], w['win_cd'], w['wpool'], w['pscale'], w['wb'], w['lam_re'], w['lam_im'],
              w['wcre'], w['wcim'], w['dskip'], w['wglu'], w['bglu'], w['wout_cd']]
    h_spec = pl.BlockSpec((nseq, tq, D_MODEL), lambda b, s: (b, s, 0))
    pool_spec = pl.BlockSpec((2 * SUBLANES, nseq, D_C), lambda b, s: (0, b, 0))
    s5_spec = pl.BlockSpec((nseq, S5_MODES), lambda b, s: (b, 0))
    state_specs = [pool_spec, s5_spec, s5_spec]
    has_state = states is not None
    out_shapes = (
        jax.ShapeDtypeStruct(h.shape, F32),
        jax.ShapeDtypeStruct((2 * SUBLANES, nb, D_C), F32),
        jax.ShapeDtypeStruct((nb, S5_MODES), F32),
        jax.ShapeDtypeStruct((nb, S5_MODES), F32),
    )
    return pl.pallas_call(
        functools.partial(_cd_body, nseq=nseq, tq=tq, pos0=pos0, has_state=has_state),
        grid=grid,
        in_specs=[h_spec] + (state_specs if has_state else []) + [full(a) for a in params],
        out_specs=tuple([h_spec] + state_specs),
        out_shape=out_shapes,
        scratch_shapes=[pltpu.VMEM((nseq * tq, 2 * S5_MODES), F32)],
        compiler_params=pltpu.CompilerParams(
            dimension_semantics=("arbitrary", "arbitrary"), vmem_limit_bytes=VMEM_LIMIT_BYTES),
        name="cd_mixer",
    )(h, *(states if has_state else ()), *params)


def _ffn_pe_body(h_ref, p_ref, gffn_ref, wup_ref, wdn_ref, gpe_ref, wgate_ref, wproj_ref, gfin_ref,
                 out_ref, *, final):
    x = h_ref[...]
    xn = _rms(x, gffn_ref[...]).astype(BF16)
    h = x
    ffc = D_FF // 4
    for ci in range(4):
        a = jnp.maximum(_dot(xn, wup_ref[:, ci * ffc:(ci + 1) * ffc]), 0.0)
        h = h + _dot((a * a).astype(BF16), wdn_ref[ci * ffc:(ci + 1) * ffc, :])
    hn = _rms(h, gpe_ref[...]).astype(BF16)
    gate = _sigmoid(_dot(hn, wgate_ref[...]))
    h = h + _dot(p_ref[...].astype(BF16), wproj_ref[...]) * gate
    if final:
        h = _rms(h, gfin_ref[...])
    out_ref[...] = h


def _ffn_pe(h2, p3, w, layer, *, tm, final):
    n = h2.shape[0]
    of_layer = lambda a: pl.BlockSpec((None,) + a.shape[1:], lambda i: (layer,) + (0,) * (a.ndim - 1),
                                      pipeline_mode=pl.Buffered(1))
    params = [w['gffn'], w['wup'], w['wdn'], w['gpe'], w['wgate'], w['wproj']]
    return pl.pallas_call(
        functools.partial(_ffn_pe_body, final=final),
        grid=(n // tm,),
        in_specs=[pl.BlockSpec((tm, D_MODEL), lambda i: (i, 0)),
                  pl.BlockSpec((None, tm, D_PLE), lambda i: (layer, i, 0))]
        + [of_layer(a) for a in params] + [pl.BlockSpec(w['gfin'].shape, lambda i: (0, 0))],
        out_specs=pl.BlockSpec((tm, D_MODEL), lambda i: (i, 0)),
        out_shape=jax.ShapeDtypeStruct(h2.shape, F32),
        compiler_params=pltpu.CompilerParams(
            dimension_semantics=("arbitrary",), vmem_limit_bytes=VMEM_LIMIT_BYTES),
        name="ffn_pe",
    )(h2, p3, *params, w['gfin'])


def _s5_prep_body(lre_ref, lim_ref, ldt_ref, bre_ref, bim_ref, lbre_ref, lbim_ref, bbre_ref, bbim_ref):
    lr = lre_ref[...]
    li = lim_ref[...]
    dt = jnp.exp(ldt_ref[...])
    mag = jnp.exp(lr * dt)
    ang = li * dt
    lb_re = mag * jnp.cos(ang)
    lb_im = mag * jnp.sin(ang)
    lbre_ref[...] = lb_re
    lbim_ref[...] = lb_im
    a = lb_re - 1.0
    b = lb_im
    den = lr * lr + li * li
    cr = (a * lr + b * li) / den
    ci = (b * lr - a * li) / den
    bre = bre_ref[...]
    bim = bim_ref[...]
    bbre_ref[...] = cr * bre - ci * bim
    bbim_ref[...] = cr * bim + ci * bre


def _s5_prep(lam_re, lam_im, log_dt, b_re, b_im):
    ldt = jnp.broadcast_to(log_dt[:, None], (S5_GROUPS, S5_STATE))
    rows = [a.reshape(1, S5_MODES) for a in (lam_re, lam_im, ldt)]
    chan_major = lambda a: a.reshape(S5_MODES, S5_GW).T
    shp = jax.ShapeDtypeStruct((S5_GW, S5_MODES), F32)
    row_shp = jax.ShapeDtypeStruct((1, S5_MODES), F32)
    lb_re, lb_im, bb_re, bb_im = pl.pallas_call(
        _s5_prep_body, out_shape=(row_shp, row_shp, shp, shp), name="s5_prep",
    )(*rows, chan_major(b_re), chan_major(b_im))
    to_gcn = lambda a: jnp.swapaxes(a.reshape(S5_GW, S5_GROUPS, S5_STATE), 0, 1)
    return lb_re, lb_im, to_gcn(bb_re), to_gcn(bb_im)


def _block_diag(a, ngrp):
    p, q = a.shape[1], a.shape[2]
    eye = jnp.eye(ngrp, dtype=a.dtype)
    a = a.reshape(S5_GROUPS // ngrp, ngrp, p, q)
    return jnp.einsum('hgpq,gk->hgpkq', a, eye).reshape(S5_GROUPS // ngrp, ngrp * p, ngrp * q)


def _prepare_weights(norm_mix, norm_ffn, norm_pe, norm_final, w_in_ab, conv_qkv, a_log, dt_bias, norm_o,
                     ln_v_gain, ln_v_bias, w_spatial, b_spatial, w_out_ab, w_in_cd, w_pool, pool_scale,
                     lam_re, lam_im, log_dt, b_re, b_im, c_re, c_im, d_skip, w_glu, b_glu, w_out_cd,
                     w_ffn_up, w_ffn_down, w_pe_proj, w_pe_gate):
    row = lambda a: a.reshape(1, -1)
    w = {}
    w['gmix0'] = row(norm_mix[0])
    w['gmix1'] = row(norm_mix[1])
    wi = w_in_ab[0]
    cut0 = D_QKV + D_A
    cut1 = cut0 + 2 * GDN_HEADS
    w['win_qkvg'] = wi[:, :cut0].astype(BF16)
    w['win_uv'] = wi[:, cut1:].astype(BF16)
    w['wba'] = jnp.pad(wi[:, cut0:cut1], ((0, 0), (0, LANES - 2 * GDN_HEADS))).astype(BF16)
    w['convw'] = jnp.pad(conv_qkv[0], ((0, SUBLANES - 4), (0, 0)))
    w['gpar'] = jnp.pad(jnp.concatenate([a_log, dt_bias], axis=0),
                        ((0, 0), (GDN_HEADS, LANES - 2 * GDN_HEADS)))
    w['normo'] = row(norm_o[0])
    w['lng'] = row(ln_v_gain[0])
    w['lnb'] = row(ln_v_bias[0])
    w['wout_ab'] = w_out_ab[0].astype(BF16)
    w['w_spatial'] = w_spatial[0]
    w['b_spatial'] = b_spatial[0]
    w['wpool'] = w_pool[0].astype(BF16)
    w['pscale'] = row(pool_scale[0])
    lb_re, lb_im, bb_re, bb_im = _s5_prep(lam_re[0], lam_im[0], log_dt[0], b_re[0], b_im[0])
    w['lam_re'] = lb_re
    w['lam_im'] = lb_im
    half = S5_GROUPS // 2
    w['wb'] = jnp.concatenate([_block_diag(bb_re, half), _block_diag(bb_im, half)],
                              axis=2).astype(BF16)
    per_tile = LANES // S5_GW
    w['wcre'] = _block_diag(jnp.swapaxes(c_re[0], 1, 2), per_tile).astype(BF16)
    w['wcim'] = _block_diag(jnp.swapaxes(c_im[0], 1, 2), per_tile).astype(BF16)
    w['dskip'] = row(d_skip[0])
    w['bglu'] = row(b_glu[0])
    w['gffn'] = norm_ffn.reshape(2, 1, D_MODEL)
    w['gpe'] = norm_pe.reshape(2, 1, D_MODEL)
    w['gfin'] = row(norm_final)
    w['wproj'] = w_pe_proj.astype(BF16)
    return w


def _mix_ab(x, states, w, *, tile, casts=()):
    nseq, tq = tile
    assert tq % SGU_CHUNK == 0 or (tq == x.shape[1] and tq < SGU_CHUNK), (tq, x.shape)
    c = min(64, tq)
    cs = min(SGU_CHUNK, tq)
    reps = SGU_CHUNK // cs
    wl = dict(w)
    wl['wsp'] = jnp.tile(w['w_spatial'][:, :cs, :cs], (1, reps, reps))
    wl['bsp'] = jnp.tile(w['b_spatial'][:, :cs], (1, reps))[:, :, None]
    (h, conv_o, delta_o, vrows), cast_out = _ab_layer(x, states, wl, nseq=nseq, tq=tq, c=c, cs=cs, casts=casts)
    return h, (conv_o[None, :, SUBLANES - (GDN_CONV - 1):, :], delta_o[None], vrows[None]), cast_out


def _mix_cd(h, states, pos0, w, *, tile):
    nb = h.shape[0]
    if states is not None:
        pool0, s5re0, s5im0 = states
        pool_in = jnp.swapaxes(jnp.pad(pool0, ((0, 0), (2 * SUBLANES - POOL_BUF, 0), (0, 0))), 0, 1)
        states = (pool_in, s5re0.reshape(nb, S5_MODES), s5im0.reshape(nb, S5_MODES))
    h, pool_o, s5re, s5im = _cd_layer(h, states, w, nseq=tile[0], tq=tile[1], pos0=pos0)
    pool_o = jnp.swapaxes(pool_o, 0, 1)[:, 2 * SUBLANES - POOL_BUF:, :]
    return h, (pool_o[None], s5re.reshape(1, nb, S5_GROUPS, S5_STATE), s5im.reshape(1, nb, S5_GROUPS, S5_STATE))


def kernel(x_prompt, x_sample, state_conv, state_delta, state_pool, state_s5_re, state_s5_im, p_prompt, p_sample, norm_mix, norm_ffn, norm_pe, norm_final, w_in_ab, conv_qkv, a_log, dt_bias, norm_o, ln_v_gain, ln_v_bias, w_spatial, b_spatial, w_out_ab, w_in_cd, w_pool, pool_scale, lam_re, lam_im, log_dt, b_re, b_im, c_re, c_im, d_skip, w_glu, b_glu, w_out_cd, w_ffn_up, w_ffn_down, w_pe_proj, w_pe_gate):
    w = _prepare_weights(norm_mix, norm_ffn, norm_pe, norm_final, w_in_ab, conv_qkv, a_log, dt_bias, norm_o,
                         ln_v_gain, ln_v_bias, w_spatial, b_spatial, w_out_ab, w_in_cd, w_pool, pool_scale,
                         lam_re, lam_im, log_dt, b_re, b_im, c_re, c_im, d_skip, w_glu, b_glu, w_out_cd,
                         w_ffn_up, w_ffn_down, w_pe_proj, w_pe_gate)
    bp, tp, _ = x_prompt.shape
    bs, ts, _ = x_sample.shape
    flat = lambda a: a.reshape(-1, a.shape[-1])
    pp3 = p_prompt.reshape(p_prompt.shape[0], bp * tp, D_PLE)
    ps3 = p_sample.reshape(p_sample.shape[0], bs * ts, D_PLE)
    later = (w_ffn_up, w_ffn_down, w_pe_gate, w_in_cd[0], w_out_cd[0], w_glu[0])
    hp, ab_p, later_bf16 = _mix_ab(x_prompt, None, w, tile=(4, 128), casts=later)
    w['wup'], w['wdn'], w['wgate'], w['win_cd'], w['wout_cd'], w['wglu'] = later_bf16
    hs, ab_s, _ = _mix_ab(x_sample, (state_conv[0], state_delta[0]), w, tile=(16, 8))
    hp = _ffn_pe(flat(hp), pp3, w, 0, tm=FFN_ROWS, final=False)
    hs = _ffn_pe(flat(hs), ps3, w, 0, tm=FFN_ROWS // 2, final=False)
    hp, cd_p = _mix_cd(hp.reshape(bp, tp, D_MODEL), None, 0, w, tile=(8, 128))
    hs, cd_s = _mix_cd(hs.reshape(bs, ts, D_MODEL), (state_pool[0], state_s5_re[0], state_s5_im[0]),
                       PAST_LEN, w, tile=(32, 8))
    yp = _ffn_pe(flat(hp), pp3, w, 1, tm=FFN_ROWS, final=True)
    ys = _ffn_pe(flat(hs), ps3, w, 1, tm=FFN_ROWS // 2, final=True)
    return (yp.reshape(bp, tp, D_MODEL), ys.reshape(bs, ts, D_MODEL)) + ab_p + cd_p + ab_s + cd_s
```

```python
import functools
import math

import jax
import jax.numpy as jnp
from jax import lax
from jax.experimental import pallas as pl
from jax.experimental.pallas import tpu as pltpu

F32 = jnp.float32
BF16 = jnp.bfloat16
HIGHEST = lax.Precision.HIGHEST

EPS = 1e-6
D_MODEL = 1024
PAST_LEN = 16384
GDN_HEADS = 4
GDN_DK = 128
GDN_CONV = 4
D_A = 512
D_QKV = 1536
D_B = 512
SGU_GROUPS = 4
SGU_CHUNK = 128
POOL_WINDOWS = (2, 4, 8, 16)
POOL_BUF = 15
D_C = 512
D_D = 512
S5_GROUPS = 32
S5_STATE = 64
S5_GW = 16
D_FF = 4096
D_PLE = 256
S5_MODES = S5_GROUPS * S5_STATE
S5_HALF = S5_MODES // 2
LANES = 128
SUBLANES = 8
GDN_SUB = 128
S5_SCAN_VREGS = 16
CD_PERM_ROWS = 256
AB_PROJ_ROWS = 256
FFN_ROWS = 1024
VMEM_LIMIT_BYTES = 56 * 1024 * 1024


def _dot(a, b):
    return jnp.dot(a, b, preferred_element_type=F32)


def _dot_nt(a, b):
    return lax.dot_general(a, b, (((1,), (1,)), ((), ())), preferred_element_type=F32)


def _dot_exact(a, b):
    return jnp.dot(a, b, precision=HIGHEST, preferred_element_type=F32)


def _rms(x, g):
    return x * lax.rsqrt(jnp.mean(x * x, axis=-1, keepdims=True) + EPS) * g


def _sigmoid(x):
    return 0.5 * jnp.tanh(0.5 * x) + 0.5


def _silu(x):
    return x * _sigmoid(x)


def _gelu_tanh(x):
    return 0.5 * x * (1.0 + jnp.tanh(math.sqrt(2.0 / math.pi) * (x + 0.044715 * (x * x * x))))


def _softplus(x):
    return jnp.maximum(x, 0.0) + jnp.log(1.0 + jnp.exp(-jnp.abs(x)))


def _log2(n):
    l = n.bit_length() - 1
    assert (1 << l) == n, n
    return l


def _ab_body(*refs, nseq, tq, c, cs, has_state, n_cast):
    h_ref, refs = refs[0], refs[1:]
    if has_state:
        (conv0_ref, delta0_ref), refs = refs[:2], refs[2:]
    (gmix_ref, win_qkvg_ref, win_uv_ref, wba_ref, convw_ref, gpar_ref, normo_ref, lng_ref, lnb_ref, wsp_ref,
     bsp_ref, wout_ref), refs = refs[:12], refs[12:]
    cast_in, refs = refs[:n_cast], refs[n_cast:]
    (hout_ref, conv_ref, delta_ref, vrows_ref), cast_out = refs[:4], refs[4:]
    for src_ref, dst_ref in zip(cast_in, cast_out):
        dst_ref[...] = src_ref[...].astype(BF16)
    R = nseq * tq
    t_idx = pl.program_id(1)

    @pl.when(t_idx == 0)
    def _init():
        conv_ref[...] = jnp.zeros(conv_ref.shape, F32)
        if has_state:
            conv_ref[:, SUBLANES - (GDN_CONV - 1):, :] = conv0_ref[...]
            delta_ref[...] = delta0_ref[...]
        else:
            delta_ref[...] = jnp.zeros(delta_ref.shape, F32)

    cat = lambda parts, axis: parts[0] if len(parts) == 1 else jnp.concatenate(parts, axis=axis)
    pr = min(R, AB_PROJ_ROWS)
    spp = pr // tq
    cw = convw_ref[...]
    z_parts, zba_parts, qkv_parts = [], [], []
    for i in range(R // pr):
        xn = _rms(h_ref[i * spp:(i + 1) * spp].reshape(pr, D_MODEL), gmix_ref[...]).astype(BF16)
        zi = jnp.concatenate([_dot_nt(xn, win_qkvg_ref[...]), _dot_nt(xn, win_uv_ref[...])], axis=1)
        z_parts.append(zi)
        zba_parts.append(_dot_nt(xn, wba_ref[...]))
        pre = zi[:, :D_QKV].reshape(spp, tq, D_QKV)
        xext = jnp.concatenate([conv_ref[i * spp:(i + 1) * spp], pre], axis=1)
        acc = pre * cw[3:4, :]
        for k in range(1, GDN_CONV):
            acc = acc + pltpu.roll(xext, k, axis=1)[:, SUBLANES:, :] * cw[3 - k:4 - k, :]
        conv_ref[i * spp:(i + 1) * spp] = xext[:, tq:, :]
        qkv_parts.append(_silu(acc).reshape(pr, D_QKV))
    z = cat(z_parts, 0)
    zba = cat(zba_parts, 0)
    qkv = cat(qkv_parts, 0)

    gpar = gpar_ref[...]
    beta_all = _sigmoid(zba)
    g_all = -jnp.exp(gpar[0:1, :]) * _softplus(zba + gpar[1:2, :])

    sub = GDN_SUB
    log2c = _log2(c)
    row = lax.broadcasted_iota(jnp.int32, (sub, sub), 0)
    col = lax.broadcasted_iota(jnp.int32, (sub, sub), 1)
    cblk = lax.shift_right_logical(col, log2c)
    same = lax.shift_right_logical(row, log2c) == cblk
    tril = same & (row >= col)
    strict = same & (row > col)
    tril_f = tril.astype(F32)
    eye = (row == col).astype(F32)
    cblk_row = cblk[0:1, :]
    nblk = sub // c
    blk_per_seq = tq // c
    nsub = R // sub
    pairs = [(st, h) for st in range(nsub) for h in range(GDN_HEADS)]
    gl = lambda h: slice(GDN_HEADS + h, GDN_HEADS + h + 1)
    gc_alls, glasts, grems, gc_ts = [], [], [], []
    for st in range(nsub):
        gc_all = _dot_exact(tril_f, g_all[st * sub:(st + 1) * sub])
        glast = [gc_all[(j + 1) * c - 1:(j + 1) * c, :] for j in range(nblk)]
        gc_alls.append(gc_all)
        glasts.append(glast)
        grems.append(cat([jnp.broadcast_to(gl, (c, LANES)) for gl in glast], 0) - gc_all)
        gc_ts.append(gc_all.T)
    qs, ks, vs, betas, egs, decays, qkks = {}, {}, {}, {}, {}, {}, {}
    for p in pairs:
        st, h = p
        r0, lo = st * sub, h * GDN_DK
        q = qkv[r0:r0 + sub, lo:lo + GDN_DK]
        k = qkv[r0:r0 + sub, D_A + lo:D_A + lo + GDN_DK]
        qs[p] = q * lax.rsqrt(jnp.sum(q * q, axis=-1, keepdims=True) + EPS) * (GDN_DK ** -0.5)
        ks[p] = k * lax.rsqrt(jnp.sum(k * k, axis=-1, keepdims=True) + EPS)
        vs[p] = qkv[r0:r0 + sub, 2 * D_A + lo:2 * D_A + lo + GDN_DK]
        betas[p] = beta_all[r0:r0 + sub, h:h + 1]
        gc = gc_alls[st][:, gl(h)]
        egs[p] = jnp.exp(gc)
        decays[p] = jnp.where(tril, jnp.exp(jnp.where(tril, gc - gc_ts[st][gl(h), :], 0.0)), 0.0)
    for p in pairs:
        kb = ks[p].astype(BF16)
        qkks[p] = _dot_nt(jnp.concatenate([qs[p].astype(BF16), kb], axis=0), kb)
    xks, tinvs, qkds = {}, {}, {}
    for p in pairs:
        qkds[p] = (qkks[p][:sub] * decays[p]).astype(BF16)
        xks[p] = jnp.where(strict, -(betas[p] * qkks[p][sub:] * decays[p]), 0.0)
        tinvs[p] = eye + xks[p]
    if log2c > 1:
        for p in pairs:
            xb = xks[p].astype(BF16)
            xks[p] = _dot(xb, xb)
    for lev in range(1, log2c):
        res = {}
        for p in pairs:
            xb = xks[p].astype(BF16)
            if lev < log2c - 1:
                res[p] = _dot(jnp.concatenate([xb, tinvs[p].astype(BF16)], axis=0), xb)
            else:
                res[p] = _dot(tinvs[p].astype(BF16), xb)
        for p in pairs:
            if lev < log2c - 1:
                xks[p] = res[p][:sub]
                tinvs[p] = tinvs[p] + res[p][sub:]
            else:
                tinvs[p] = tinvs[p] + res[p]
    uws, q_heads, kts = {}, {}, {}
    for p in pairs:
        rhs = jnp.concatenate([(vs[p] * betas[p]).astype(BF16),
                               (ks[p] * (betas[p] * egs[p])).astype(BF16)], axis=1)
        uws[p] = _dot(tinvs[p].astype(BF16), rhs)
        q_heads[p] = qs[p] * egs[p]
        st, h = p
        kts[p] = (ks[p] * jnp.exp(grems[st][:, gl(h)])).T
    u_lists = {p: [] for p in pairs}
    qs_lists = {p: [] for p in pairs}
    for kstep in range(blk_per_seq):
        items = []
        for seq in range(nseq):
            st, j = divmod((seq * tq + kstep * c) // c, nblk)
            items += [(seq, st, j, h) for h in range(GDN_HEADS)]
        s_olds, tmps = {}, {}
        for it in items:
            seq, st, j, h = it
            p = (st, h)
            sl = slice(j * c, (j + 1) * c)
            s_olds[it] = delta_ref[seq, h]
            both = jnp.concatenate([uws[p][sl, GDN_DK:], q_heads[p][sl]], axis=0).astype(BF16)
            tmps[it] = _dot(both, s_olds[it].astype(BF16))
        for it in items:
            seq, st, j, h = it
            p = (st, h)
            sl = slice(j * c, (j + 1) * c)
            u_base = uws[p][:, :GDN_DK]
            u_lists[p].append(u_base[sl] - tmps[it][:c])
            qs_lists[p].append(tmps[it][c:])
            u_full = cat(u_lists[p] + ([u_base[(j + 1) * c:]] if j + 1 < nblk else []), 0)
            kt_j = jnp.where(cblk_row == j, kts[p], 0.0).astype(BF16)
            delta_ref[seq, h] = (s_olds[it] * jnp.exp(glasts[st][j][:, gl(h)])
                                 + _dot(kt_j, u_full.astype(BF16)))
    o_raw = {}
    for p in pairs:
        o_raw[p] = cat(qs_lists[p], 0) + _dot(qkds[p], cat(u_lists[p], 0).astype(BF16))
    o_tiles = []
    for st in range(nsub):
        o_heads = []
        for h in range(GDN_HEADS):
            zg = z[st * sub:(st + 1) * sub, D_QKV + h * GDN_DK:D_QKV + (h + 1) * GDN_DK]
            o_heads.append(_rms(o_raw[(st, h)], normo_ref[...]) * _silu(zg))
        o_tiles.append(jnp.concatenate(o_heads, axis=1))
    o_a = cat(o_tiles, 0)

    zu = z[:, D_QKV + D_A:D_QKV + D_A + D_B]
    zv = z[:, D_QKV + D_A + D_B:]
    u_act = _gelu_tanh(zu)
    gv = _gelu_tanh(zv)
    vc = gv - jnp.mean(gv, axis=-1, keepdims=True)
    var = jnp.mean(vc * vc, axis=-1, keepdims=True)
    v_ln = vc * lax.rsqrt(var + EPS) * lng_ref[...] + lnb_ref[...]
    rv = vrows_ref.shape[1]
    vrows_ref[...] = v_ln.reshape(nseq, tq, D_B)[:, tq - rv:, :]
    log2cs = _log2(cs)
    srow = lax.broadcasted_iota(jnp.int32, (SGU_CHUNK, SGU_CHUNK), 0)
    scol = lax.broadcasted_iota(jnp.int32, (SGU_CHUNK, SGU_CHUNK), 1)
    smask = (lax.shift_right_logical(srow, log2cs) == lax.shift_right_logical(scol, log2cs)) & (srow >= scol)
    v_bf = v_ln.astype(BF16)
    mixed_groups = []
    for g in range(SGU_GROUPS):
        wg = jnp.where(smask, wsp_ref[g], 0.0).astype(BF16)
        bg = bsp_ref[g]
        parts = []
        for r0 in range(0, R, SGU_CHUNK):
            parts.append(_dot(wg, v_bf[r0:r0 + SGU_CHUNK, g * LANES:(g + 1) * LANES]) + bg)
        mixed_groups.append(parts[0] if len(parts) == 1 else jnp.concatenate(parts, axis=0))
    o_b = u_act * jnp.concatenate(mixed_groups, axis=1)

    y_in = jnp.concatenate([o_a, o_b], axis=1).astype(BF16)
    for i in range(R // pr):
        seqs = slice(i * spp, (i + 1) * spp)
        proj = _dot(y_in[i * pr:(i + 1) * pr], wout_ref[...])
        hout_ref[seqs] = h_ref[seqs] + proj.reshape(spp, tq, D_MODEL)


def _ab_layer(h, states, w, *, nseq, tq, c, cs, casts=()):
    nb, t, _ = h.shape
    has_state = states is not None
    grid = (nb // nseq, t // tq)
    nsteps = grid[0] * grid[1]

    def slab(a):
        rows = a.shape[-2] // nsteps
        assert rows * nsteps == a.shape[-2] and rows % (2 * SUBLANES) == 0, a.shape
        lead = a.ndim - 2
        return pl.BlockSpec(a.shape[:lead] + (rows, a.shape[-1]),
                            lambda b, s: (0,) * lead + (b * grid[1] + s, 0))
    rv = min(SGU_CHUNK, tq)
    seq_blk = lambda shape: pl.BlockSpec(shape, lambda b, s: (b,) + (0,) * (len(shape) - 1))
    full = lambda a: pl.BlockSpec(a.shape, lambda b, s: (0,) * a.ndim)
    params = [w['gmix0'], w['win_qkvg'], w['win_uv'], w['wba'], w['convw'], w['gpar'], w['normo'], w['lng'], w['lnb'],
              w['wsp'], w['bsp'], w['wout_ab']]
    out_shapes = (
        jax.ShapeDtypeStruct(h.shape, F32),
        jax.ShapeDtypeStruct((nb, SUBLANES, D_QKV), F32),
        jax.ShapeDtypeStruct((nb, GDN_HEADS, GDN_DK, GDN_DK), F32),
        jax.ShapeDtypeStruct((nb, rv, D_B), F32),
    ) + tuple(jax.ShapeDtypeStruct(a.shape, BF16) for a in casts)
    outs = pl.pallas_call(
        functools.partial(_ab_body, nseq=nseq, tq=tq, c=c, cs=cs, has_state=has_state, n_cast=len(casts)),
        grid=grid,
        in_specs=[pl.BlockSpec((nseq, tq, D_MODEL), lambda b, s: (b, s, 0))]
        + ([seq_blk((nseq, GDN_CONV - 1, D_QKV)), seq_blk((nseq, GDN_HEADS, GDN_DK, GDN_DK))]
           if has_state else []) + [full(a) for a in params] + [slab(a) for a in casts],
        out_specs=(pl.BlockSpec((nseq, tq, D_MODEL), lambda b, s: (b, s, 0)),
                   seq_blk((nseq, SUBLANES, D_QKV)),
                   seq_blk((nseq, GDN_HEADS, GDN_DK, GDN_DK)),
                   seq_blk((nseq, rv, D_B))) + tuple(slab(a) for a in casts),
        out_shape=out_shapes,
        compiler_params=pltpu.CompilerParams(
            dimension_semantics=("arbitrary", "arbitrary"), vmem_limit_bytes=VMEM_LIMIT_BYTES),
        name="ab_mixer",
    )(h, *(states if has_state else ()), *params, *casts)
    return outs[:4], outs[4:]


def _cd_body(*refs, nseq, tq, pos0, has_state):
    h_ref, refs = refs[0], refs[1:]
    if has_state:
        (pool0_ref, s5re0_ref, s5im0_ref), refs = refs[:3], refs[3:]
    (gmix_ref, win_ref, wpool_ref, pscale_ref, wb_ref, lam_re_ref, lam_im_ref, wcre_ref, wcim_ref,
     dskip_ref, wglu_ref, bglu_ref, wout_ref, hout_ref, pool_ref, s5re_ref, s5im_ref, bu_ref) = refs
    R = nseq * tq
    t_idx = pl.program_id(1)

    @pl.when(t_idx == 0)
    def _init():
        if has_state:
            pool_ref[...] = pool0_ref[...]
            s5re_ref[...] = s5re0_ref[...]
            s5im_ref[...] = s5im0_ref[...]
        else:
            pool_ref[...] = jnp.zeros(pool_ref.shape, F32)
            s5re_ref[...] = jnp.zeros(s5re_ref.shape, F32)
            s5im_ref[...] = jnp.zeros(s5im_ref.shape, F32)

    x3 = h_ref[...]
    xn3 = _rms(x3, gmix_ref[...]).astype(BF16)
    nper = R // CD_PERM_ROWS
    tsub = tq // nper
    log2n = _log2(nseq)
    ri = lax.broadcasted_iota(jnp.int32, (CD_PERM_ROWS, CD_PERM_ROWS), 0)
    ci = lax.broadcasted_iota(jnp.int32, (CD_PERM_ROWS, CD_PERM_ROWS), 1)
    to_tm = (ci == (ri & (nseq - 1)) * tsub + lax.shift_right_logical(ri, log2n)).astype(BF16)
    to_sm = (ri == (ci & (nseq - 1)) * tsub + lax.shift_right_logical(ci, log2n)).astype(BF16)
    xt = jnp.concatenate(
        [_dot(to_tm, xn3[:, i * tsub:(i + 1) * tsub, :].reshape(CD_PERM_ROWS, D_MODEL)).astype(BF16)
         for i in range(nper)], axis=0)
    z = _dot(xt, win_ref[...])
    xc = z[:, :D_C].reshape(tq, nseq, D_C)
    xd = z[:, D_C:]

    xext = jnp.concatenate([pool_ref[...], xc], axis=0)
    pool_ref[...] = xext[tq:]
    pos = pos0 + t_idx * tq + lax.broadcasted_iota(jnp.int32, (tq, 1, 1), 0)
    sums = xext
    outs = []
    for gi, win in enumerate(POOL_WINDOWS):
        sums = sums[win // 2:] + sums[:-(win // 2)]
        cnt = jnp.minimum(win, pos + 1).astype(F32)
        sl = slice(gi * LANES, (gi + 1) * LANES)
        m = sums[sums.shape[0] - tq:, :, sl] / cnt - xc[:, :, sl]
        outs.append(_dot(m.reshape(R, LANES).astype(BF16), wpool_ref[gi]))
    y_c = jnp.concatenate(outs, axis=1) * pscale_ref[...]

    xd_bf = xd.astype(BF16)
    for half in range(2):
        res = _dot(xd_bf[:, half * D_D // 2:(half + 1) * D_D // 2], wb_ref[half])
        bu_ref[:, half * S5_HALF:(half + 1) * S5_HALF] = res[:, :S5_HALF]
        bu_ref[:, S5_MODES + half * S5_HALF:S5_MODES + (half + 1) * S5_HALF] = res[:, S5_HALF:]
    piece = S5_SCAN_VREGS * SUBLANES * LANES // (2 * nseq)
    pieces = range(0, S5_MODES, piece)
    lam = {p0: (jnp.broadcast_to(lam_re_ref[:, p0:p0 + piece], (nseq, piece)),
                jnp.broadcast_to(lam_im_ref[:, p0:p0 + piece], (nseq, piece))) for p0 in pieces}
    carry = {p0: (s5re_ref[:, p0:p0 + piece], s5im_ref[:, p0:p0 + piece]) for p0 in pieces}
    kn = wcre_ref.shape[1]
    for i in range(nper):
        rows = slice(i * CD_PERM_ROWS, (i + 1) * CD_PERM_ROWS)
        for p0 in pieces:
            lr, li = lam[p0]
            re, im = carry[p0]
            for t in range(i * tsub, (i + 1) * tsub):
                trows = slice(t * nseq, (t + 1) * nseq)
                re, im = (lr * re - li * im + bu_ref[trows, p0:p0 + piece],
                          lr * im + li * re + bu_ref[trows, S5_MODES + p0:S5_MODES + p0 + piece])
                bu_ref[trows, p0:p0 + piece] = re
                bu_ref[trows, S5_MODES + p0:S5_MODES + p0 + piece] = im
            carry[p0] = (re, im)
        ys = []
        for n in range(wcre_ref.shape[0]):
            s_re = bu_ref[rows, n * kn:(n + 1) * kn].astype(BF16)
            s_im = bu_ref[rows, S5_MODES + n * kn:S5_MODES + (n + 1) * kn].astype(BF16)
            ys.append(_dot(s_re, wcre_ref[n]) - _dot(s_im, wcim_ref[n]))
        y = jnp.concatenate(ys, axis=1) + dskip_ref[...] * xd[rows]
        y = _gelu_tanh(y)
        y_d = y * _sigmoid(_dot(y.astype(BF16), wglu_ref[...]) + bglu_ref[...])
        y_in = jnp.concatenate([y_c[rows], y_d], axis=1).astype(BF16)
        proj = _dot(_dot(to_sm, y_in).astype(BF16), wout_ref[...])
        ts = slice(i * tsub, (i + 1) * tsub)
        hout_ref[:, ts, :] = x3[:, ts, :] + proj.reshape(nseq, tsub, D_MODEL)
    for p0 in pieces:
        s5re_ref[:, p0:p0 + piece], s5im_ref[:, p0:p0 + piece] = carry[p0]


def _cd_layer(h, states, w, *, nseq, tq, pos0):
    nb, t, _ = h.shape
    grid = (nb // nseq, t // tq)
    full = lambda a: pl.BlockSpec(a.shape, lambda b, s: (0,) * a.ndim)
    params = [w['gmix1'], w['win_cd'], w['wpool'], w['pscale'], w['wb'], w['lam_re'], w['lam_im'],
              w['wcre'], w['wcim'], w['dskip'], w['wglu'], w['bglu'], w['wout_cd']]
    h_spec = pl.BlockSpec((nseq, tq, D_MODEL), lambda b, s: (b, s, 0))
    pool_spec = pl.BlockSpec((2 * SUBLANES, nseq, D_C), lambda b, s: (0, b, 0))
    s5_spec = pl.BlockSpec((nseq, S5_MODES), lambda b, s: (b, 0))
    state_specs = [pool_spec, s5_spec, s5_spec]
    has_state = states is not None
    out_shapes = (
        jax.ShapeDtypeStruct(h.shape, F32),
        jax.ShapeDtypeStruct((2 * SUBLANES, nb, D_C), F32),
        jax.ShapeDtypeStruct((nb, S5_MODES), F32),
        jax.ShapeDtypeStruct((nb, S5_MODES), F32),
    )
    return pl.pallas_call(
        functools.partial(_cd_body, nseq=nseq, tq=tq, pos0=pos0, has_state=has_state),
        grid=grid,
        in_specs=[h_spec] + (state_specs if has_state else []) + [full(a) for a in params],
        out_specs=tuple([h_spec] + state_specs),
        out_shape=out_shapes,
        scratch_shapes=[pltpu.VMEM((nseq * tq, 2 * S5_MODES), F32)],
        compiler_params=pltpu.CompilerParams(
            dimension_semantics=("arbitrary", "arbitrary"), vmem_limit_bytes=VMEM_LIMIT_BYTES),
        name="cd_mixer",
    )(h, *(states if has_state else ()), *params)


def _ffn_pe_body(h_ref, p_ref, gffn_ref, wup_ref, wdn_ref, gpe_ref, wgate_ref, wproj_ref, gfin_ref,
                 out_ref, *, final):
    x = h_ref[...]
    xn = _rms(x, gffn_ref[...]).astype(BF16)
    h = x
    ffc = D_FF // 4
    for ci in range(4):
        a = jnp.maximum(_dot(xn, wup_ref[:, ci * ffc:(ci + 1) * ffc]), 0.0)
        h = h + _dot((a * a).astype(BF16), wdn_ref[ci * ffc:(ci + 1) * ffc, :])
    hn = _rms(h, gpe_ref[...]).astype(BF16)
    gate = _sigmoid(_dot(hn, wgate_ref[...]))
    h = h + _dot(p_ref[...].astype(BF16), wproj_ref[...]) * gate
    if final:
        h = _rms(h, gfin_ref[...])
    out_ref[...] = h


def _ffn_pe(h2, p3, w, layer, *, tm, final):
    n = h2.shape[0]
    of_layer = lambda a: pl.BlockSpec((None,) + a.shape[1:], lambda i: (layer,) + (0,) * (a.ndim - 1),
                                      pipeline_mode=pl.Buffered(1))
    params = [w['gffn'], w['wup'], w['wdn'], w['gpe'], w['wgate'], w['wproj']]
    return pl.pallas_call(
        functools.partial(_ffn_pe_body, final=final),
        grid=(n // tm,),
        in_specs=[pl.BlockSpec((tm, D_MODEL), lambda i: (i, 0)),
                  pl.BlockSpec((None, tm, D_PLE), lambda i: (layer, i, 0))]
        + [of_layer(a) for a in params] + [pl.BlockSpec(w['gfin'].shape, lambda i: (0, 0))],
        out_specs=pl.BlockSpec((tm, D_MODEL), lambda i: (i, 0)),
        out_shape=jax.ShapeDtypeStruct(h2.shape, F32),
        compiler_params=pltpu.CompilerParams(
            dimension_semantics=("arbitrary",), vmem_limit_bytes=VMEM_LIMIT_BYTES),
        name="ffn_pe",
    )(h2, p3, *params, w['gfin'])


def _s5_prep_body(lre_ref, lim_ref, ldt_ref, bre_ref, bim_ref, lbre_ref, lbim_ref, bbre_ref, bbim_ref):
    lr = lre_ref[...]
    li = lim_ref[...]
    dt = jnp.exp(ldt_ref[...])
    mag = jnp.exp(lr * dt)
    ang = li * dt
    lb_re = mag * jnp.cos(ang)
    lb_im = mag * jnp.sin(ang)
    lbre_ref[...] = lb_re
    lbim_ref[...] = lb_im
    a = lb_re - 1.0
    b = lb_im
    den = lr * lr + li * li
    cr = (a * lr + b * li) / den
    ci = (b * lr - a * li) / den
    bre = bre_ref[...]
    bim = bim_ref[...]
    bbre_ref[...] = cr * bre - ci * bim
    bbim_ref[...] = cr * bim + ci * bre


def _s5_prep(lam_re, lam_im, log_dt, b_re, b_im):
    ldt = jnp.broadcast_to(log_dt[:, None], (S5_GROUPS, S5_STATE))
    rows = [a.reshape(1, S5_MODES) for a in (lam_re, lam_im, ldt)]
    chan_major = lambda a: a.reshape(S5_MODES, S5_GW).T
    shp = jax.ShapeDtypeStruct((S5_GW, S5_MODES), F32)
    row_shp = jax.ShapeDtypeStruct((1, S5_MODES), F32)
    lb_re, lb_im, bb_re, bb_im = pl.pallas_call(
        _s5_prep_body, out_shape=(row_shp, row_shp, shp, shp), name="s5_prep",
    )(*rows, chan_major(b_re), chan_major(b_im))
    to_gcn = lambda a: jnp.swapaxes(a.reshape(S5_GW, S5_GROUPS, S5_STATE), 0, 1)
    return lb_re, lb_im, to_gcn(bb_re), to_gcn(bb_im)


def _block_diag(a, ngrp):
    p, q = a.shape[1], a.shape[2]
    eye = jnp.eye(ngrp, dtype=a.dtype)
    a = a.reshape(S5_GROUPS // ngrp, ngrp, p, q)
    return jnp.einsum('hgpq,gk->hgpkq', a, eye).reshape(S5_GROUPS // ngrp, ngrp * p, ngrp * q)


def _prepare_weights(norm_mix, norm_ffn, norm_pe, norm_final, w_in_ab, conv_qkv, a_log, dt_bias, norm_o,
                     ln_v_gain, ln_v_bias, w_spatial, b_spatial, w_out_ab, w_in_cd, w_pool, pool_scale,
                     lam_re, lam_im, log_dt, b_re, b_im, c_re, c_im, d_skip, w_glu, b_glu, w_out_cd,
                     w_ffn_up, w_ffn_down, w_pe_proj, w_pe_gate):
    row = lambda a: a.reshape(1, -1)
    w = {}
    w['gmix0'] = row(norm_mix[0])
    w['gmix1'] = row(norm_mix[1])
    wt = jnp.swapaxes(w_in_ab[0], 0, 1)
    cut0 = D_QKV + D_A
    cut1 = cut0 + 2 * GDN_HEADS
    w['win_qkvg'] = wt[:cut0].astype(BF16)
    w['win_uv'] = wt[cut1:].astype(BF16)
    w['wba'] = jnp.pad(wt[cut0:cut1], ((0, LANES - 2 * GDN_HEADS), (0, 0))).astype(BF16)
    w['convw'] = jnp.pad(conv_qkv[0], ((0, SUBLANES - 4), (0, 0)))
    w['gpar'] = jnp.pad(jnp.concatenate([a_log, dt_bias], axis=0),
                        ((0, 0), (GDN_HEADS, LANES - 2 * GDN_HEADS)))
    w['normo'] = row(norm_o[0])
    w['lng'] = row(ln_v_gain[0])
    w['lnb'] = row(ln_v_bias[0])
    w['wout_ab'] = w_out_ab[0].astype(BF16)
    w['w_spatial'] = w_spatial[0]
    w['b_spatial'] = b_spatial[0]
    w['wpool'] = w_pool[0].astype(BF16)
    w['pscale'] = row(pool_scale[0])
    lb_re, lb_im, bb_re, bb_im = _s5_prep(lam_re[0], lam_im[0], log_dt[0], b_re[0], b_im[0])
    w['lam_re'] = lb_re
    w['lam_im'] = lb_im
    half = S5_GROUPS // 2
    w['wb'] = jnp.concatenate([_block_diag(bb_re, half), _block_diag(bb_im, half)],
                              axis=2).astype(BF16)
    per_tile = LANES // S5_GW
    w['wcre'] = _block_diag(jnp.swapaxes(c_re[0], 1, 2), per_tile).astype(BF16)
    w['wcim'] = _block_diag(jnp.swapaxes(c_im[0], 1, 2), per_tile).astype(BF16)
    w['dskip'] = row(d_skip[0])
    w['bglu'] = row(b_glu[0])
    w['gffn'] = norm_ffn.reshape(2, 1, D_MODEL)
    w['gpe'] = norm_pe.reshape(2, 1, D_MODEL)
    w['gfin'] = row(norm_final)
    w['wproj'] = w_pe_proj.astype(BF16)
    return w


def _mix_ab(x, states, w, *, tile, casts=()):
    nseq, tq = tile
    assert tq % SGU_CHUNK == 0 or (tq == x.shape[1] and tq < SGU_CHUNK), (tq, x.shape)
    c = min(64, tq)
    cs = min(SGU_CHUNK, tq)
    reps = SGU_CHUNK // cs
    wl = dict(w)
    wl['wsp'] = jnp.tile(w['w_spatial'][:, :cs, :cs], (1, reps, reps))
    wl['bsp'] = jnp.tile(w['b_spatial'][:, :cs], (1, reps))[:, :, None]
    (h, conv_o, delta_o, vrows), cast_out = _ab_layer(x, states, wl, nseq=nseq, tq=tq, c=c, cs=cs, casts=casts)
    return h, (conv_o[None, :, SUBLANES - (GDN_CONV - 1):, :], delta_o[None], vrows[None]), cast_out


def _mix_cd(h, states, pos0, w, *, tile):
    nb = h.shape[0]
    if states is not None:
        pool0, s5re0, s5im0 = states
        pool_in = jnp.swapaxes(jnp.pad(pool0, ((0, 0), (2 * SUBLANES - POOL_BUF, 0), (0, 0))), 0, 1)
        states = (pool_in, s5re0.reshape(nb, S5_MODES), s5im0.reshape(nb, S5_MODES))
    h, pool_o, s5re, s5im = _cd_layer(h, states, w, nseq=tile[0], tq=tile[1], pos0=pos0)
    pool_o = jnp.swapaxes(pool_o, 0, 1)[:, 2 * SUBLANES - POOL_BUF:, :]
    return h, (pool_o[None], s5re.reshape(1, nb, S5_GROUPS, S5_STATE), s5im.reshape(1, nb, S5_GROUPS, S5_STATE))


def kernel(x_prompt, x_sample, state_conv, state_delta, state_pool, state_s5_re, state_s5_im, p_prompt, p_sample, norm_mix, norm_ffn, norm_pe, norm_final, w_in_ab, conv_qkv, a_log, dt_bias, norm_o, ln_v_gain, ln_v_bias, w_spatial, b_spatial, w_out_ab, w_in_cd, w_pool, pool_scale, lam_re, lam_im, log_dt, b_re, b_im, c_re, c_im, d_skip, w_glu, b_glu, w_out_cd, w_ffn_up, w_ffn_down, w_pe_proj, w_pe_gate):
    w = _prepare_weights(norm_mix, norm_ffn, norm_pe, norm_final, w_in_ab, conv_qkv, a_log, dt_bias, norm_o,
                         ln_v_gain, ln_v_bias, w_spatial, b_spatial, w_out_ab, w_in_cd, w_pool, pool_scale,
                         lam_re, lam_im, log_dt, b_re, b_im, c_re, c_im, d_skip, w_glu, b_glu, w_out_cd,
                         w_ffn_up, w_ffn_down, w_pe_proj, w_pe_gate)
    bp, tp, _ = x_prompt.shape
    bs, ts, _ = x_sample.shape
    flat = lambda a: a.reshape(-1, a.shape[-1])
    pp3 = p_prompt.reshape(p_prompt.shape[0], bp * tp, D_PLE)
    ps3 = p_sample.reshape(p_sample.shape[0], bs * ts, D_PLE)
    later = (w_ffn_up, w_ffn_down, w_pe_gate, w_in_cd[0], w_out_cd[0], w_glu[0])
    hp, ab_p, later_bf16 = _mix_ab(x_prompt, None, w, tile=(4, 128), casts=later)
    w['wup'], w['wdn'], w['wgate'], w['win_cd'], w['wout_cd'], w['wglu'] = later_bf16
    hs, ab_s, _ = _mix_ab(x_sample, (state_conv[0], state_delta[0]), w, tile=(16, 8))
    hp = _ffn_pe(flat(hp), pp3, w, 0, tm=FFN_ROWS, final=False)
    hs = _ffn_pe(flat(hs), ps3, w, 0, tm=FFN_ROWS // 2, final=False)
    hp, cd_p = _mix_cd(hp.reshape(bp, tp, D_MODEL), None, 0, w, tile=(8, 128))
    hs, cd_s = _mix_cd(hs.reshape(bs, ts, D_MODEL), (state_pool[0], state_s5_re[0], state_s5_im[0]),
                       PAST_LEN, w, tile=(32, 8))
    yp = _ffn_pe(flat(hp), pp3, w, 1, tm=FFN_ROWS, final=True)
    ys = _ffn_pe(flat(hs), ps3, w, 1, tm=FFN_ROWS // 2, final=True)
    return (yp.reshape(bp, tp, D_MODEL), ys.reshape(bs, ts, D_MODEL)) + ab_p + cd_p + ab_s + cd_s
```

```python
import functools
import math

import jax
import jax.numpy as jnp
from jax import lax
from jax.experimental import pallas as pl
from jax.experimental.pallas import tpu as pltpu

F32 = jnp.float32
BF16 = jnp.bfloat16
HIGHEST = lax.Precision.HIGHEST

EPS = 1e-6
D_MODEL = 1024
PAST_LEN = 16384
GDN_HEADS = 4
GDN_DK = 128
GDN_CONV = 4
D_A = 512
D_QKV = 1536
D_B = 512
SGU_GROUPS = 4
SGU_CHUNK = 128
POOL_WINDOWS = (2, 4, 8, 16)
POOL_BUF = 15
D_C = 512
D_D = 512
S5_GROUPS = 32
S5_STATE = 64
S5_GW = 16
D_FF = 4096
D_PLE = 256
S5_MODES = S5_GROUPS * S5_STATE
S5_HALF = S5_MODES // 2
LANES = 128
SUBLANES = 8
GDN_SUB = 128
S5_SCAN_VREGS = 16
CD_PERM_ROWS = 256
AB_PROJ_ROWS = 256
FFN_ROWS = 1024
N_AB_PARAMS = 12
VMEM_LIMIT_BYTES = 56 * 1024 * 1024


def _dot(a, b):
    return jnp.dot(a, b, preferred_element_type=F32)


def _dot_nt(a, b):
    return lax.dot_general(a, b, (((1,), (1,)), ((), ())), preferred_element_type=F32)


def _dot_exact(a, b):
    return jnp.dot(a, b, precision=HIGHEST, preferred_element_type=F32)


def _rms(x, g):
    return x * lax.rsqrt(jnp.mean(x * x, axis=-1, keepdims=True) + EPS) * g


def _sigmoid(x):
    return 0.5 * jnp.tanh(0.5 * x) + 0.5


def _silu(x):
    h = 0.5 * x
    return h + h * jnp.tanh(h)


def _gelu_tanh(x):
    c = math.sqrt(2.0 / math.pi)
    h = 0.5 * x
    return h + h * jnp.tanh(x * (c + (c * 0.044715) * (x * x)))


def _softplus(x):
    return jnp.maximum(x, 0.0) + jnp.log(1.0 + jnp.exp(-jnp.abs(x)))


def _log2(n):
    l = n.bit_length() - 1
    assert (1 << l) == n, n
    return l


def _ab_body(*refs, nseq, tq, c, cs, has_state, n_cast):
    h_ref, refs = refs[0], refs[1:]
    if has_state:
        (conv0_ref, delta0_ref), refs = refs[:2], refs[2:]
    prm, refs = refs[:N_AB_PARAMS], refs[N_AB_PARAMS:]
    cast_in, refs = refs[:n_cast], refs[n_cast:]
    (hout_ref, conv_ref, delta_ref, vrows_ref), cast_out = refs[:4], refs[4:]
    for src_ref, dst_ref in zip(cast_in, cast_out):
        dst_ref[...] = src_ref[...].astype(BF16)

    @pl.when(pl.program_id(1) == 0)
    def _init():
        conv_ref[...] = jnp.zeros(conv_ref.shape, F32)
        if has_state:
            conv_ref[:, SUBLANES - (GDN_CONV - 1):, :] = conv0_ref[...]
            delta_ref[...] = delta0_ref[...]
        else:
            delta_ref[...] = jnp.zeros(delta_ref.shape, F32)

    def store(seqs, value):
        hout_ref[seqs] = value

    _ab_core(h_ref, prm, conv_ref, delta_ref, vrows_ref, store, nseq=nseq, tq=tq, c=c, cs=cs)


def _ab_core(h_ref, prm, conv_ref, delta_ref, vrows_ref, store, *, nseq, tq, c, cs, hook=lambda: None):
    (gmix_ref, win_qkvg_ref, win_uv_ref, wba_ref, convw_ref, gpar_ref, normo_ref, lng_ref, lnb_ref, wsp_ref,
     bsp_ref, wout_ref) = prm
    R = nseq * tq
    cat = lambda parts, axis: parts[0] if len(parts) == 1 else jnp.concatenate(parts, axis=axis)
    pr = min(R, AB_PROJ_ROWS)
    spp = pr // tq
    cw = convw_ref[...]
    z_parts, zba_parts, qkv_parts = [], [], []
    for i in range(R // pr):
        xn = _rms(h_ref[i * spp:(i + 1) * spp].reshape(pr, D_MODEL), gmix_ref[...]).astype(BF16)
        zi = jnp.concatenate([_dot(xn, win_qkvg_ref[...]), _dot(xn, win_uv_ref[...])], axis=1)
        z_parts.append(zi)
        zba_parts.append(_dot(xn, wba_ref[...]))
        pre = zi[:, :D_QKV].reshape(spp, tq, D_QKV)
        xext = jnp.concatenate([conv_ref[i * spp:(i + 1) * spp], pre], axis=1)
        acc = pre * cw[3:4, :]
        for k in range(1, GDN_CONV):
            acc = acc + pltpu.roll(xext, k, axis=1)[:, SUBLANES:, :] * cw[3 - k:4 - k, :]
        conv_ref[i * spp:(i + 1) * spp] = xext[:, tq:, :]
        qkv_parts.append(_silu(acc).reshape(pr, D_QKV))
        hook()
    z = cat(z_parts, 0)
    zba = cat(zba_parts, 0)
    qkv = cat(qkv_parts, 0)

    gpar = gpar_ref[...]
    beta_all = _sigmoid(zba)
    g_all = -jnp.exp(gpar[0:1, :]) * _softplus(zba + gpar[1:2, :])

    sub = GDN_SUB
    log2c = _log2(c)
    row = lax.broadcasted_iota(jnp.int32, (sub, sub), 0)
    col = lax.broadcasted_iota(jnp.int32, (sub, sub), 1)
    cblk = lax.shift_right_logical(col, log2c)
    same = lax.shift_right_logical(row, log2c) == cblk
    tril = same & (row >= col)
    strict = same & (row > col)
    tril_f = tril.astype(F32)
    eye = (row == col).astype(F32)
    cblk_row = cblk[0:1, :]
    nblk = sub // c
    blk_per_seq = tq // c
    nsub = R // sub
    pairs = [(st, h) for st in range(nsub) for h in range(GDN_HEADS)]
    gl = lambda h: slice(GDN_HEADS + h, GDN_HEADS + h + 1)
    gc_alls, glasts, grems, gc_ts = [], [], [], []
    for st in range(nsub):
        gc_all = _dot_exact(tril_f, g_all[st * sub:(st + 1) * sub])
        glast = [gc_all[(j + 1) * c - 1:(j + 1) * c, :] for j in range(nblk)]
        gc_alls.append(gc_all)
        glasts.append(glast)
        grems.append(cat([jnp.broadcast_to(gl, (c, LANES)) for gl in glast], 0) - gc_all)
        gc_ts.append(gc_all.T)
    qs, ks, vs, betas, egs, decays, qkks = {}, {}, {}, {}, {}, {}, {}
    for p in pairs:
        st, h = p
        r0, lo = st * sub, h * GDN_DK
        q = qkv[r0:r0 + sub, lo:lo + GDN_DK]
        k = qkv[r0:r0 + sub, D_A + lo:D_A + lo + GDN_DK]
        qs[p] = q * (lax.rsqrt(jnp.sum(q * q, axis=-1, keepdims=True) + EPS) * (GDN_DK ** -0.5))
        ks[p] = k * lax.rsqrt(jnp.sum(k * k, axis=-1, keepdims=True) + EPS)
        vs[p] = qkv[r0:r0 + sub, 2 * D_A + lo:2 * D_A + lo + GDN_DK]
        betas[p] = beta_all[r0:r0 + sub, h:h + 1]
        gc = gc_alls[st][:, gl(h)]
        egs[p] = jnp.exp(gc)
        decays[p] = jnp.where(tril, jnp.exp(jnp.where(tril, gc - gc_ts[st][gl(h), :], 0.0)), 0.0)
    for p in pairs:
        kb = ks[p].astype(BF16)
        qkks[p] = _dot_nt(jnp.concatenate([qs[p].astype(BF16), kb], axis=0), kb)
    hook()
    xks, tinvs, qkds = {}, {}, {}
    for p in pairs:
        qkds[p] = (qkks[p][:sub] * decays[p]).astype(BF16)
        xks[p] = jnp.where(strict, -(betas[p] * qkks[p][sub:] * decays[p]), 0.0)
        tinvs[p] = eye + xks[p]
    if log2c > 1:
        for p in pairs:
            xb = xks[p].astype(BF16)
            xks[p] = _dot(xb, xb)
        hook()
    for lev in range(1, log2c):
        res = {}
        for p in pairs:
            xb = xks[p].astype(BF16)
            if lev < log2c - 1:
                res[p] = _dot(jnp.concatenate([xb, tinvs[p].astype(BF16)], axis=0), xb)
            else:
                res[p] = _dot(tinvs[p].astype(BF16), xb)
        hook()
        for p in pairs:
            if lev < log2c - 1:
                xks[p] = res[p][:sub]
                tinvs[p] = tinvs[p] + res[p][sub:]
            else:
                tinvs[p] = tinvs[p] + res[p]
    uws, q_heads, kts = {}, {}, {}
    for p in pairs:
        rhs = jnp.concatenate([(vs[p] * betas[p]).astype(BF16),
                               (ks[p] * (betas[p] * egs[p])).astype(BF16)], axis=1)
        uws[p] = _dot(tinvs[p].astype(BF16), rhs)
        q_heads[p] = qs[p] * egs[p]
        st, h = p
        kts[p] = (ks[p] * jnp.exp(grems[st][:, gl(h)])).T
    hook()
    u_lists = {p: [] for p in pairs}
    qs_lists = {p: [] for p in pairs}
    for kstep in range(blk_per_seq):
        items = []
        for seq in range(nseq):
            st, j = divmod((seq * tq + kstep * c) // c, nblk)
            items += [(seq, st, j, h) for h in range(GDN_HEADS)]
        s_olds, tmps = {}, {}
        for it in items:
            seq, st, j, h = it
            p = (st, h)
            sl = slice(j * c, (j + 1) * c)
            s_olds[it] = delta_ref[seq, h]
            both = jnp.concatenate([uws[p][sl, GDN_DK:], q_heads[p][sl]], axis=0).astype(BF16)
            tmps[it] = _dot(both, s_olds[it].astype(BF16))
        hook()
        for it in items:
            seq, st, j, h = it
            p = (st, h)
            sl = slice(j * c, (j + 1) * c)
            u_base = uws[p][:, :GDN_DK]
            u_lists[p].append(u_base[sl] - tmps[it][:c])
            qs_lists[p].append(tmps[it][c:])
            u_full = cat(u_lists[p] + ([u_base[(j + 1) * c:]] if j + 1 < nblk else []), 0)
            kt_j = jnp.where(cblk_row == j, kts[p], 0.0).astype(BF16)
            delta_ref[seq, h] = (s_olds[it] * jnp.exp(glasts[st][j][:, gl(h)])
                                 + _dot(kt_j, u_full.astype(BF16)))
        hook()
    o_raw = {}
    for p in pairs:
        o_raw[p] = cat(qs_lists[p], 0) + _dot(qkds[p], cat(u_lists[p], 0).astype(BF16))
    hook()
    o_tiles = []
    for st in range(nsub):
        o_heads = []
        for h in range(GDN_HEADS):
            zg = z[st * sub:(st + 1) * sub, D_QKV + h * GDN_DK:D_QKV + (h + 1) * GDN_DK]
            o_heads.append(_rms(o_raw[(st, h)], normo_ref[...]) * _silu(zg))
        o_tiles.append(jnp.concatenate(o_heads, axis=1))
    o_a = cat(o_tiles, 0)

    zu = z[:, D_QKV + D_A:D_QKV + D_A + D_B]
    zv = z[:, D_QKV + D_A + D_B:]
    u_act = _gelu_tanh(zu)
    gv = _gelu_tanh(zv)
    vc = gv - jnp.mean(gv, axis=-1, keepdims=True)
    var = jnp.mean(vc * vc, axis=-1, keepdims=True)
    v_ln = vc * lax.rsqrt(var + EPS) * lng_ref[...] + lnb_ref[...]
    rv = vrows_ref.shape[1]
    vrows_ref[...] = v_ln.reshape(nseq, tq, D_B)[:, tq - rv:, :]
    log2cs = _log2(cs)
    srow = lax.broadcasted_iota(jnp.int32, (SGU_CHUNK, SGU_CHUNK), 0)
    scol = lax.broadcasted_iota(jnp.int32, (SGU_CHUNK, SGU_CHUNK), 1)
    smask = (lax.shift_right_logical(srow, log2cs) == lax.shift_right_logical(scol, log2cs)) & (srow >= scol)
    v_bf = v_ln.astype(BF16)
    mixed_groups = []
    for g in range(SGU_GROUPS):
        wg = jnp.where(smask, wsp_ref[g], 0.0).astype(BF16)
        bg = bsp_ref[g]
        parts = []
        for r0 in range(0, R, SGU_CHUNK):
            parts.append(_dot(wg, v_bf[r0:r0 + SGU_CHUNK, g * LANES:(g + 1) * LANES]) + bg)
        mixed_groups.append(parts[0] if len(parts) == 1 else jnp.concatenate(parts, axis=0))
    o_b = u_act * jnp.concatenate(mixed_groups, axis=1)
    hook()

    y_in = jnp.concatenate([o_a, o_b], axis=1).astype(BF16)
    for i in range(R // pr):
        seqs = slice(i * spp, (i + 1) * spp)
        proj = _dot(y_in[i * pr:(i + 1) * pr], wout_ref[...])
        store(seqs, h_ref[seqs] + proj.reshape(spp, tq, D_MODEL))


def _cast_slabs(casts, nsteps, step_of):
    in_specs, out_specs, out_shapes = [], [], []
    for entry in casts:
        a, layer = entry if isinstance(entry, tuple) else (entry, None)
        rows = a.shape[-2] // nsteps
        assert rows * nsteps == a.shape[-2] and rows % (2 * SUBLANES) == 0, a.shape
        cols = a.shape[-1]
        if layer is None:
            lead = a.ndim - 2
            spec = pl.BlockSpec(a.shape[:lead] + (rows, cols), lambda *g, lead=lead: (0,) * lead + (step_of(*g), 0))
            in_specs.append(spec)
            out_specs.append(spec)
            out_shapes.append(jax.ShapeDtypeStruct(a.shape, BF16))
        else:
            in_specs.append(pl.BlockSpec((None, rows, cols), lambda *g, layer=layer: (layer, step_of(*g), 0)))
            out_specs.append(pl.BlockSpec((rows, cols), lambda *g: (step_of(*g), 0)))
            out_shapes.append(jax.ShapeDtypeStruct(a.shape[1:], BF16))
    arrays = [e[0] if isinstance(e, tuple) else e for e in casts]
    return arrays, in_specs, out_specs, out_shapes


def _ab_layer(h, states, w, *, nseq, tq, c, cs, casts=()):
    nb, t, _ = h.shape
    has_state = states is not None
    grid = (nb // nseq, t // tq)
    cast_arrays, cast_in, cast_out, cast_shapes = _cast_slabs(
        casts, grid[0] * grid[1], lambda b, s: b * grid[1] + s)
    rv = min(SGU_CHUNK, tq)
    seq_blk = lambda shape: pl.BlockSpec(shape, lambda b, s: (b,) + (0,) * (len(shape) - 1))
    full = lambda a: pl.BlockSpec(a.shape, lambda b, s: (0,) * a.ndim)
    params = [w['gmix0'], w['win_qkvg'], w['win_uv'], w['wba'], w['convw'], w['gpar'], w['normo'], w['lng'], w['lnb'],
              w['wsp'], w['bsp'], w['wout_ab']]
    out_shapes = (
        jax.ShapeDtypeStruct(h.shape, F32),
        jax.ShapeDtypeStruct((nb, SUBLANES, D_QKV), F32),
        jax.ShapeDtypeStruct((nb, GDN_HEADS, GDN_DK, GDN_DK), F32),
        jax.ShapeDtypeStruct((nb, rv, D_B), F32),
    ) + tuple(cast_shapes)
    outs = pl.pallas_call(
        functools.partial(_ab_body, nseq=nseq, tq=tq, c=c, cs=cs, has_state=has_state, n_cast=len(casts)),
        grid=grid,
        in_specs=[pl.BlockSpec((nseq, tq, D_MODEL), lambda b, s: (b, s, 0))]
        + ([seq_blk((nseq, GDN_CONV - 1, D_QKV)), seq_blk((nseq, GDN_HEADS, GDN_DK, GDN_DK))]
           if has_state else []) + [full(a) for a in params] + cast_in,
        out_specs=(pl.BlockSpec((nseq, tq, D_MODEL), lambda b, s: (b, s, 0)),
                   seq_blk((nseq, SUBLANES, D_QKV)),
                   seq_blk((nseq, GDN_HEADS, GDN_DK, GDN_DK)),
                   seq_blk((nseq, rv, D_B))) + tuple(cast_out),
        out_shape=out_shapes,
        compiler_params=pltpu.CompilerParams(
            dimension_semantics=("arbitrary", "arbitrary"), vmem_limit_bytes=VMEM_LIMIT_BYTES),
        name="ab_mixer",
    )(h, *(states if has_state else ()), *params, *cast_arrays)
    return outs[:4], outs[4:]


def _abf_body(*refs, nseq, tq, c, cs, n_cast, n_tiles, tiles_per_group):
    h_ref, p_ref, refs = refs[0], refs[1], refs[2:]
    prm, refs = refs[:N_AB_PARAMS], refs[N_AB_PARAMS:]
    ffn_prm, refs = refs[:6], refs[6:]
    cast_in, refs = refs[:n_cast], refs[n_cast:]
    (hout_ref, conv_ref, delta_ref, vrows_ref), refs = refs[:4], refs[4:]
    cast_out, h1_ref = refs[:n_cast], refs[n_cast]
    i = pl.program_id(0)
    R = nseq * tq

    @pl.when(i == 0)
    def _first():
        h1_ref[...] = jnp.zeros(h1_ref.shape, F32)

    def store_ffn(h):
        hout_ref[...] = h.reshape(nseq, tq, D_MODEL)

    def ffn_steps():
        return _ffn_pe_steps(lambda: h1_ref[...], lambda: p_ref[...].reshape(R, D_PLE), store_ffn, *ffn_prm)

    @pl.when(i < n_tiles)
    def _mix_and_ffn():
        for src_ref, dst_ref in zip(cast_in, cast_out):
            dst_ref[...] = src_ref[...].astype(BF16)

        @pl.when(lax.rem(i, tiles_per_group) == 0)
        def _init():
            conv_ref[...] = jnp.zeros(conv_ref.shape, F32)
            delta_ref[...] = jnp.zeros(delta_ref.shape, F32)

        steps = ffn_steps()
        mixed = []
        _ab_core(h_ref, prm, conv_ref, delta_ref, vrows_ref, lambda seqs, v: mixed.append((seqs, v)),
                 nseq=nseq, tq=tq, c=c, cs=cs, hook=lambda: next(steps, None))
        for _ in steps:
            pass
        for seqs, v in mixed:
            h1_ref[seqs.start * tq:seqs.stop * tq, :] = v.reshape((seqs.stop - seqs.start) * tq, D_MODEL)

    @pl.when(i == n_tiles)
    def _flush():
        for _ in ffn_steps():
            pass


def _abf_layer(h, p4, layer, w, *, nseq, tq, c, cs, casts=()):
    nb, t, _ = h.shape
    tiles_per_group = t // tq
    n_tiles = (nb // nseq) * tiles_per_group
    rv = min(SGU_CHUNK, tq)
    cur = lambda i: jnp.minimum(i, n_tiles - 1)
    prev = lambda i: jnp.maximum(i - 1, 0)
    tile_of = lambda j: (j // tiles_per_group, lax.rem(j, tiles_per_group))
    grp_blk = lambda shape: pl.BlockSpec(shape, lambda i: (cur(i) // tiles_per_group,) + (0,) * (len(shape) - 1))
    full = lambda a: pl.BlockSpec(a.shape, lambda i: (0,) * a.ndim)
    of_layer = lambda a: pl.BlockSpec((None,) + a.shape[1:], lambda i: (layer,) + (0,) * (a.ndim - 1))

    cast_arrays, cast_in, cast_out, cast_shapes = _cast_slabs(casts, n_tiles, cur)
    params = [w['gmix0'], w['win_qkvg'], w['win_uv'], w['wba'], w['convw'], w['gpar'], w['normo'], w['lng'],
              w['lnb'], w['wsp'], w['bsp'], w['wout_ab']]
    ffn_params = [w['gffn'], w['wup0'], w['wdn0'], w['gpe'], w['wgate0'], w['wproj']]
    ffn_specs = [of_layer(w['gffn']), full(w['wup0']), full(w['wdn0']), of_layer(w['gpe']), full(w['wgate0']),
                 of_layer(w['wproj'])]
    out_shapes = (
        jax.ShapeDtypeStruct(h.shape, F32),
        jax.ShapeDtypeStruct((nb, SUBLANES, D_QKV), F32),
        jax.ShapeDtypeStruct((nb, GDN_HEADS, GDN_DK, GDN_DK), F32),
        jax.ShapeDtypeStruct((nb, rv, D_B), F32),
    ) + tuple(cast_shapes)
    outs = pl.pallas_call(
        functools.partial(_abf_body, nseq=nseq, tq=tq, c=c, cs=cs, n_cast=len(casts), n_tiles=n_tiles,
                          tiles_per_group=tiles_per_group),
        grid=(n_tiles + 1,),
        in_specs=[pl.BlockSpec((nseq, tq, D_MODEL), lambda i: tile_of(cur(i)) + (0,)),
                  pl.BlockSpec((None, nseq, tq, D_PLE), lambda i: (layer,) + tile_of(prev(i)) + (0,))]
        + [full(a) for a in params] + ffn_specs + cast_in,
        out_specs=(pl.BlockSpec((nseq, tq, D_MODEL), lambda i: tile_of(prev(i)) + (0,)),
                   grp_blk((nseq, SUBLANES, D_QKV)),
                   grp_blk((nseq, GDN_HEADS, GDN_DK, GDN_DK)),
                   grp_blk((nseq, rv, D_B))) + tuple(cast_out),
        out_shape=out_shapes,
        scratch_shapes=[pltpu.VMEM((nseq * tq, D_MODEL), F32)],
        compiler_params=pltpu.CompilerParams(
            dimension_semantics=("arbitrary",), vmem_limit_bytes=VMEM_LIMIT_BYTES),
        name="ab_mixer_ffn",
    )(h, p4, *params, *ffn_params, *cast_arrays)
    return outs[:4], outs[4:]


def _cd_body(*refs, nseq, tq, pos0, has_state):
    h_ref, refs = refs[0], refs[1:]
    if has_state:
        (pool0_ref, s5re0_ref, s5im0_ref), refs = refs[:3], refs[3:]
    (gmix_ref, win_ref, wpool_ref, pscale_ref, wb_ref, lam_re_ref, lam_im_ref, wcre_ref, wcim_ref,
     dskip_ref, wglu_ref, bglu_ref, wout_ref, hout_ref, pool_ref, s5re_ref, s5im_ref, bu_ref) = refs
    R = nseq * tq
    t_idx = pl.program_id(1)

    @pl.when(t_idx == 0)
    def _init():
        if has_state:
            pool_ref[...] = pool0_ref[...]
            s5re_ref[...] = s5re0_ref[...]
            s5im_ref[...] = s5im0_ref[...]
        else:
            pool_ref[...] = jnp.zeros(pool_ref.shape, F32)
            s5re_ref[...] = jnp.zeros(s5re_ref.shape, F32)
            s5im_ref[...] = jnp.zeros(s5im_ref.shape, F32)

    x3 = h_ref[...]
    xn3 = _rms(x3, gmix_ref[...]).astype(BF16)
    nper = R // CD_PERM_ROWS
    tsub = tq // nper
    log2n = _log2(nseq)
    ri = lax.broadcasted_iota(jnp.int32, (CD_PERM_ROWS, CD_PERM_ROWS), 0)
    ci = lax.broadcasted_iota(jnp.int32, (CD_PERM_ROWS, CD_PERM_ROWS), 1)
    to_tm = (ci == (ri & (nseq - 1)) * tsub + lax.shift_right_logical(ri, log2n)).astype(BF16)
    to_sm = (ri == (ci & (nseq - 1)) * tsub + lax.shift_right_logical(ci, log2n)).astype(BF16)
    xt = jnp.concatenate(
        [_dot(to_tm, xn3[:, i * tsub:(i + 1) * tsub, :].reshape(CD_PERM_ROWS, D_MODEL)).astype(BF16)
         for i in range(nper)], axis=0)
    z = _dot(xt, win_ref[...])
    xc = z[:, :D_C].reshape(tq, nseq, D_C)
    xd = z[:, D_C:]

    xext = jnp.concatenate([pool_ref[...], xc], axis=0)
    pool_ref[...] = xext[tq:]
    pos = pos0 + t_idx * tq + lax.broadcasted_iota(jnp.int32, (tq, 1, 1), 0)
    sums = xext
    outs = []
    for gi, win in enumerate(POOL_WINDOWS):
        sums = sums[win // 2:] + sums[:-(win // 2)]
        inv_cnt = 1.0 / jnp.minimum(win, pos + 1).astype(F32)
        sl = slice(gi * LANES, (gi + 1) * LANES)
        m = sums[sums.shape[0] - tq:, :, sl] * inv_cnt - xc[:, :, sl]
        outs.append(_dot(m.reshape(R, LANES).astype(BF16), wpool_ref[gi]))
    y_c = jnp.concatenate(outs, axis=1) * pscale_ref[...]

    xd_bf = xd.astype(BF16)
    for half in range(2):
        res = _dot(xd_bf[:, half * D_D // 2:(half + 1) * D_D // 2], wb_ref[half])
        bu_ref[:, half * S5_HALF:(half + 1) * S5_HALF] = res[:, :S5_HALF]
        bu_ref[:, S5_MODES + half * S5_HALF:S5_MODES + (half + 1) * S5_HALF] = res[:, S5_HALF:]
    piece = S5_SCAN_VREGS * SUBLANES * LANES // (2 * nseq)
    pieces = range(0, S5_MODES, piece)
    lam = {p0: (jnp.broadcast_to(lam_re_ref[:, p0:p0 + piece], (nseq, piece)),
                jnp.broadcast_to(lam_im_ref[:, p0:p0 + piece], (nseq, piece))) for p0 in pieces}
    carry = {p0: (s5re_ref[:, p0:p0 + piece], s5im_ref[:, p0:p0 + piece]) for p0 in pieces}
    kn = wcre_ref.shape[1]
    for i in range(nper):
        rows = slice(i * CD_PERM_ROWS, (i + 1) * CD_PERM_ROWS)
        for p0 in pieces:
            lr, li = lam[p0]
            re, im = carry[p0]
            for t in range(i * tsub, (i + 1) * tsub):
                trows = slice(t * nseq, (t + 1) * nseq)
                re, im = (lr * re - li * im + bu_ref[trows, p0:p0 + piece],
                          lr * im + li * re + bu_ref[trows, S5_MODES + p0:S5_MODES + p0 + piece])
                bu_ref[trows, p0:p0 + piece] = re
                bu_ref[trows, S5_MODES + p0:S5_MODES + p0 + piece] = im
            carry[p0] = (re, im)
        ys = []
        for n in range(wcre_ref.shape[0]):
            s_re = bu_ref[rows, n * kn:(n + 1) * kn].astype(BF16)
            s_im = bu_ref[rows, S5_MODES + n * kn:S5_MODES + (n + 1) * kn].astype(BF16)
            ys.append(_dot(s_re, wcre_ref[n]) - _dot(s_im, wcim_ref[n]))
        y = jnp.concatenate(ys, axis=1) + dskip_ref[...] * xd[rows]
        y = _gelu_tanh(y)
        y_d = y * _sigmoid(_dot(y.astype(BF16), wglu_ref[...]) + bglu_ref[...])
        y_in = jnp.concatenate([y_c[rows], y_d], axis=1).astype(BF16)
        proj = _dot(_dot(to_sm, y_in).astype(BF16), wout_ref[...])
        ts = slice(i * tsub, (i + 1) * tsub)
        hout_ref[:, ts, :] = x3[:, ts, :] + proj.reshape(nseq, tsub, D_MODEL)
    for p0 in pieces:
        s5re_ref[:, p0:p0 + piece], s5im_ref[:, p0:p0 + piece] = carry[p0]


def _cd_layer(h, states, w, *, nseq, tq, pos0):
    nb, t, _ = h.shape
    grid = (nb // nseq, t // tq)
    full = lambda a: pl.BlockSpec(a.shape, lambda b, s: (0,) * a.ndim)
    params = [w['gmix1'], w['win_cd'], w['wpool'], w['pscale'], w['wb'], w['lam_re'], w['lam_im'],
              w['wcre'], w['wcim'], w['dskip'], w['wglu'], w['bglu'], w['wout_cd']]
    h_spec = pl.BlockSpec((nseq, tq, D_MODEL), lambda b, s: (b, s, 0))
    pool_spec = pl.BlockSpec((2 * SUBLANES, nseq, D_C), lambda b, s: (0, b, 0))
    s5_spec = pl.BlockSpec((nseq, S5_MODES), lambda b, s: (b, 0))
    state_specs = [pool_spec, s5_spec, s5_spec]
    has_state = states is not None
    out_shapes = (
        jax.ShapeDtypeStruct(h.shape, F32),
        jax.ShapeDtypeStruct((2 * SUBLANES, nb, D_C), F32),
        jax.ShapeDtypeStruct((nb, S5_MODES), F32),
        jax.ShapeDtypeStruct((nb, S5_MODES), F32),
    )
    return pl.pallas_call(
        functools.partial(_cd_body, nseq=nseq, tq=tq, pos0=pos0, has_state=has_state),
        grid=grid,
        in_specs=[h_spec] + (state_specs if has_state else []) + [full(a) for a in params],
        out_specs=tuple([h_spec] + state_specs),
        out_shape=out_shapes,
        scratch_shapes=[pltpu.VMEM((nseq * tq, 2 * S5_MODES), F32)],
        compiler_params=pltpu.CompilerParams(
            dimension_semantics=("arbitrary", "arbitrary"), vmem_limit_bytes=VMEM_LIMIT_BYTES),
        name="cd_mixer",
    )(h, *(states if has_state else ()), *params)


def _ffn_pe_steps(load_x, load_p, store, gffn_ref, wup_ref, wdn_ref, gpe_ref, wgate_ref, wproj_ref,
                  gfin_ref=None):
    x = load_x()
    xn = _rms(x, gffn_ref[...]).astype(BF16)
    h = x
    ffc = D_FF // 4
    for ci in range(4):
        a = jnp.maximum(_dot(xn, wup_ref[:, ci * ffc:(ci + 1) * ffc]), 0.0)
        yield
        h = h + _dot((a * a).astype(BF16), wdn_ref[ci * ffc:(ci + 1) * ffc, :])
        yield
    hn = _rms(h, gpe_ref[...]).astype(BF16)
    gate = _sigmoid(_dot(hn, wgate_ref[...]))
    yield
    h = h + _dot(load_p().astype(BF16), wproj_ref[...]) * gate
    if gfin_ref is not None:
        h = _rms(h, gfin_ref[...])
    store(h)


def _ffn_pe_body(h_ref, p_ref, gffn_ref, wup_ref, wdn_ref, gpe_ref, wgate_ref, wproj_ref, gfin_ref,
                 out_ref, *, final):
    def store(h):
        out_ref[...] = h

    for _ in _ffn_pe_steps(lambda: h_ref[...], lambda: p_ref[...], store, gffn_ref, wup_ref, wdn_ref, gpe_ref,
                           wgate_ref, wproj_ref, gfin_ref if final else None):
        pass


def _ffn_pe(h2, p3, w, layer, *, tm, final):
    n = h2.shape[0]
    of_layer = lambda a: pl.BlockSpec((None,) + a.shape[1:], lambda i: (layer,) + (0,) * (a.ndim - 1),
                                      pipeline_mode=pl.Buffered(1))
    whole = lambda a: pl.BlockSpec(a.shape, lambda i: (0,) * a.ndim, pipeline_mode=pl.Buffered(1))
    params = [w['gffn'], w[f'wup{layer}'], w[f'wdn{layer}'], w['gpe'], w[f'wgate{layer}'], w['wproj']]
    specs = [of_layer(params[0]), whole(params[1]), whole(params[2]), of_layer(params[3]), whole(params[4]),
             of_layer(params[5])]
    return pl.pallas_call(
        functools.partial(_ffn_pe_body, final=final),
        grid=(n // tm,),
        in_specs=[pl.BlockSpec((tm, D_MODEL), lambda i: (i, 0)),
                  pl.BlockSpec((None, tm, D_PLE), lambda i: (layer, i, 0))]
        + specs + [pl.BlockSpec(w['gfin'].shape, lambda i: (0, 0))],
        out_specs=pl.BlockSpec((tm, D_MODEL), lambda i: (i, 0)),
        out_shape=jax.ShapeDtypeStruct(h2.shape, F32),
        compiler_params=pltpu.CompilerParams(
            dimension_semantics=("arbitrary",), vmem_limit_bytes=VMEM_LIMIT_BYTES),
        name="ffn_pe",
    )(h2, p3, *params, w['gfin'])


def _s5_prep_body(lre_ref, lim_ref, ldt_ref, bre_ref, bim_ref, lbre_ref, lbim_ref, bbre_ref, bbim_ref):
    lr = lre_ref[...]
    li = lim_ref[...]
    dt = jnp.exp(ldt_ref[...])
    mag = jnp.exp(lr * dt)
    ang = li * dt
    lb_re = mag * jnp.cos(ang)
    lb_im = mag * jnp.sin(ang)
    lbre_ref[...] = lb_re
    lbim_ref[...] = lb_im
    a = lb_re - 1.0
    b = lb_im
    den = lr * lr + li * li
    cr = (a * lr + b * li) / den
    ci = (b * lr - a * li) / den
    bre = bre_ref[...]
    bim = bim_ref[...]
    bbre_ref[...] = cr * bre - ci * bim
    bbim_ref[...] = cr * bim + ci * bre


def _s5_prep(lam_re, lam_im, log_dt, b_re, b_im):
    ldt = jnp.broadcast_to(log_dt[:, None], (S5_GROUPS, S5_STATE))
    rows = [a.reshape(1, S5_MODES) for a in (lam_re, lam_im, ldt)]
    chan_major = lambda a: a.reshape(S5_MODES, S5_GW).T
    shp = jax.ShapeDtypeStruct((S5_GW, S5_MODES), F32)
    row_shp = jax.ShapeDtypeStruct((1, S5_MODES), F32)
    lb_re, lb_im, bb_re, bb_im = pl.pallas_call(
        _s5_prep_body, out_shape=(row_shp, row_shp, shp, shp), name="s5_prep",
    )(*rows, chan_major(b_re), chan_major(b_im))
    to_gcn = lambda a: jnp.swapaxes(a.reshape(S5_GW, S5_GROUPS, S5_STATE), 0, 1)
    return lb_re, lb_im, to_gcn(bb_re), to_gcn(bb_im)


def _block_diag(a, ngrp):
    p, q = a.shape[1], a.shape[2]
    eye = jnp.eye(ngrp, dtype=a.dtype)
    a = a.reshape(S5_GROUPS // ngrp, ngrp, p, q)
    return jnp.einsum('hgpq,gk->hgpkq', a, eye).reshape(S5_GROUPS // ngrp, ngrp * p, ngrp * q)


def _prepare_weights(norm_mix, norm_ffn, norm_pe, norm_final, w_in_ab, conv_qkv, a_log, dt_bias, norm_o,
                     ln_v_gain, ln_v_bias, w_spatial, b_spatial, w_out_ab, w_in_cd, w_pool, pool_scale,
                     lam_re, lam_im, log_dt, b_re, b_im, c_re, c_im, d_skip, w_glu, b_glu, w_out_cd,
                     w_ffn_up, w_ffn_down, w_pe_proj, w_pe_gate):
    row = lambda a: a.reshape(1, -1)
    w = {}
    w['gmix0'] = row(norm_mix[0])
    w['gmix1'] = row(norm_mix[1])
    wi = w_in_ab[0]
    cut0 = D_QKV + D_A
    cut1 = cut0 + 2 * GDN_HEADS
    w['win_qkvg'] = wi[:, :cut0].astype(BF16)
    w['win_uv'] = wi[:, cut1:].astype(BF16)
    w['wba'] = jnp.pad(wi[:, cut0:cut1], ((0, 0), (0, LANES - 2 * GDN_HEADS))).astype(BF16)
    w['convw'] = jnp.pad(conv_qkv[0], ((0, SUBLANES - 4), (0, 0)))
    w['gpar'] = jnp.pad(jnp.concatenate([a_log, dt_bias], axis=0),
                        ((0, 0), (GDN_HEADS, LANES - 2 * GDN_HEADS)))
    w['normo'] = row(norm_o[0])
    w['lng'] = row(ln_v_gain[0])
    w['lnb'] = row(ln_v_bias[0])
    w['wout_ab'] = w_out_ab[0].astype(BF16)
    w['w_spatial'] = w_spatial[0]
    w['b_spatial'] = b_spatial[0]
    w['wpool'] = w_pool[0].astype(BF16)
    w['pscale'] = row(pool_scale[0])
    lb_re, lb_im, bb_re, bb_im = _s5_prep(lam_re[0], lam_im[0], log_dt[0], b_re[0], b_im[0])
    w['lam_re'] = lb_re
    w['lam_im'] = lb_im
    half = S5_GROUPS // 2
    w['wb'] = jnp.concatenate([_block_diag(bb_re, half), _block_diag(bb_im, half)],
                              axis=2).astype(BF16)
    per_tile = LANES // S5_GW
    w['wcre'] = _block_diag(jnp.swapaxes(c_re[0], 1, 2), per_tile).astype(BF16)
    w['wcim'] = _block_diag(jnp.swapaxes(c_im[0], 1, 2), per_tile).astype(BF16)
    w['dskip'] = row(d_skip[0])
    w['bglu'] = row(b_glu[0])
    w['gffn'] = norm_ffn.reshape(2, 1, D_MODEL)
    w['gpe'] = norm_pe.reshape(2, 1, D_MODEL)
    w['gfin'] = row(norm_final)
    w['wproj'] = w_pe_proj.astype(BF16)
    return w


def _mix_ab(x, states, w, *, tile, casts=(), p4=None):
    nseq, tq = tile
    assert tq % SGU_CHUNK == 0 or (tq == x.shape[1] and tq < SGU_CHUNK), (tq, x.shape)
    c = min(64, tq)
    cs = min(SGU_CHUNK, tq)
    reps = SGU_CHUNK // cs
    wl = dict(w)
    wl['wsp'] = jnp.tile(w['w_spatial'][:, :cs, :cs], (1, reps, reps))
    wl['bsp'] = jnp.tile(w['b_spatial'][:, :cs], (1, reps))[:, :, None]
    if p4 is None:
        outs, cast_out = _ab_layer(x, states, wl, nseq=nseq, tq=tq, c=c, cs=cs, casts=casts)
    else:
        assert states is None
        outs, cast_out = _abf_layer(x, p4, 0, wl, nseq=nseq, tq=tq, c=c, cs=cs, casts=casts)
    h, conv_o, delta_o, vrows = outs
    return h, (conv_o[None, :, SUBLANES - (GDN_CONV - 1):, :], delta_o[None], vrows[None]), cast_out


def _mix_cd(h, states, pos0, w, *, tile):
    nb = h.shape[0]
    if states is not None:
        pool0, s5re0, s5im0 = states
        pool_in = jnp.swapaxes(jnp.pad(pool0, ((0, 0), (2 * SUBLANES - POOL_BUF, 0), (0, 0))), 0, 1)
        states = (pool_in, s5re0.reshape(nb, S5_MODES), s5im0.reshape(nb, S5_MODES))
    h, pool_o, s5re, s5im = _cd_layer(h, states, w, nseq=tile[0], tq=tile[1], pos0=pos0)
    pool_o = jnp.swapaxes(pool_o, 0, 1)[:, 2 * SUBLANES - POOL_BUF:, :]
    return h, (pool_o[None], s5re.reshape(1, nb, S5_GROUPS, S5_STATE), s5im.reshape(1, nb, S5_GROUPS, S5_STATE))


def kernel(x_prompt, x_sample, state_conv, state_delta, state_pool, state_s5_re, state_s5_im, p_prompt, p_sample, norm_mix, norm_ffn, norm_pe, norm_final, w_in_ab, conv_qkv, a_log, dt_bias, norm_o, ln_v_gain, ln_v_bias, w_spatial, b_spatial, w_out_ab, w_in_cd, w_pool, pool_scale, lam_re, lam_im, log_dt, b_re, b_im, c_re, c_im, d_skip, w_glu, b_glu, w_out_cd, w_ffn_up, w_ffn_down, w_pe_proj, w_pe_gate):
    w = _prepare_weights(norm_mix, norm_ffn, norm_pe, norm_final, w_in_ab, conv_qkv, a_log, dt_bias, norm_o,
                         ln_v_gain, ln_v_bias, w_spatial, b_spatial, w_out_ab, w_in_cd, w_pool, pool_scale,
                         lam_re, lam_im, log_dt, b_re, b_im, c_re, c_im, d_skip, w_glu, b_glu, w_out_cd,
                         w_ffn_up, w_ffn_down, w_pe_proj, w_pe_gate)
    bp, tp, _ = x_prompt.shape
    bs, ts, _ = x_sample.shape
    flat = lambda a: a.reshape(-1, a.shape[-1])
    pp3 = p_prompt.reshape(p_prompt.shape[0], bp * tp, D_PLE)
    ps3 = p_sample.reshape(p_sample.shape[0], bs * ts, D_PLE)
    first = ((w_ffn_up, 0), (w_ffn_down, 0), (w_pe_gate, 0))
    hs, ab_s, first_bf16 = _mix_ab(x_sample, (state_conv[0], state_delta[0]), w, tile=(16, 8), casts=first)
    w['wup0'], w['wdn0'], w['wgate0'] = first_bf16
    later = ((w_ffn_up, 1), (w_ffn_down, 1), (w_pe_gate, 1), w_in_cd[0], w_out_cd[0])
    hp, ab_p, later_bf16 = _mix_ab(x_prompt, None, w, tile=(2, 128), casts=later, p4=p_prompt)
    w['wup1'], w['wdn1'], w['wgate1'], w['win_cd'], w['wout_cd'] = later_bf16
    w['wglu'] = w_glu[0].astype(BF16)
    hs = _ffn_pe(flat(hs), ps3, w, 0, tm=FFN_ROWS // 2, final=False)
    hp, cd_p = _mix_cd(hp.reshape(bp, tp, D_MODEL), None, 0, w, tile=(8, 128))
    hs, cd_s = _mix_cd(hs.reshape(bs, ts, D_MODEL), (state_pool[0], state_s5_re[0], state_s5_im[0]),
                       PAST_LEN, w, tile=(32, 8))
    yp = _ffn_pe(flat(hp), pp3, w, 1, tm=FFN_ROWS, final=True)
    ys = _ffn_pe(flat(hs), ps3, w, 1, tm=FFN_ROWS // 2, final=True)
    return (yp.reshape(bp, tp, D_MODEL), ys.reshape(bs, ts, D_MODEL)) + ab_p + cd_p + ab_s + cd_s
```

```python
import functools
import math

import jax
import jax.numpy as jnp
from jax import lax
from jax.experimental import pallas as pl
from jax.experimental.pallas import tpu as pltpu

F32 = jnp.float32
BF16 = jnp.bfloat16
HIGHEST = lax.Precision.HIGHEST

EPS = 1e-6
D_MODEL = 1024
PAST_LEN = 16384
GDN_HEADS = 4
GDN_DK = 128
GDN_CONV = 4
D_A = 512
D_QKV = 1536
D_B = 512
SGU_GROUPS = 4
SGU_CHUNK = 128
POOL_WINDOWS = (2, 4, 8, 16)
POOL_BUF = 15
D_C = 512
D_D = 512
S5_GROUPS = 32
S5_STATE = 64
S5_GW = 16
D_FF = 4096
D_PLE = 256
S5_MODES = S5_GROUPS * S5_STATE
S5_HALF = S5_MODES // 2
LANES = 128
SUBLANES = 8
GDN_SUB = 128
S5_SCAN_VREGS = 16
CD_PERM_ROWS = 256
AB_PROJ_ROWS = 256
FFN_ROWS = 1024
VMEM_LIMIT_BYTES = 56 * 1024 * 1024


def _dot(a, b):
    return jnp.dot(a, b, preferred_element_type=F32)


def _dot_nt(a, b):
    return lax.dot_general(a, b, (((1,), (1,)), ((), ())), preferred_element_type=F32)


def _dot_exact(a, b):
    return jnp.dot(a, b, precision=HIGHEST, preferred_element_type=F32)


def _rms(x, g):
    return x * lax.rsqrt(jnp.mean(x * x, axis=-1, keepdims=True) + EPS) * g


def _sigmoid(x):
    return 0.5 * jnp.tanh(0.5 * x) + 0.5


def _silu(x):
    return x * _sigmoid(x)


def _gelu_tanh(x):
    return 0.5 * x * (1.0 + jnp.tanh(math.sqrt(2.0 / math.pi) * (x + 0.044715 * (x * x * x))))


def _softplus(x):
    return jnp.maximum(x, 0.0) + jnp.log(1.0 + jnp.exp(-jnp.abs(x)))


def _log2(n):
    l = n.bit_length() - 1
    assert (1 << l) == n, n
    return l


def _ab_body(*refs, nseq, tq, c, cs, has_state, n_cast, out_major):
    h_ref, refs = refs[0], refs[1:]
    if has_state:
        (conv0_ref, delta0_ref), refs = refs[:2], refs[2:]
    (gmix_ref, win_qkvg_ref, win_uv_ref, wba_ref, convw_ref, gpar_ref, normo_ref, lng_ref, lnb_ref, wsp_ref,
     bsp_ref, wout_ref), refs = refs[:12], refs[12:]
    cast_in, refs = refs[:n_cast], refs[n_cast:]
    (hout_ref, conv_ref, delta_ref, vrows_ref), refs = refs[:4], refs[4:]
    cast_out, refs = refs[:n_cast], refs[n_cast:]
    for src_ref, dst_ref in zip(cast_in, cast_out):
        dst_ref[...] = src_ref[...].astype(BF16)
    win_dot = _dot_nt if out_major else _dot
    if out_major:
        qkvg_t_ref, uv_t_ref, ba_t_ref = refs
        step = pl.program_id(0) * pl.num_programs(1) + pl.program_id(1)
        n_a = win_qkvg_ref.shape[0] // LANES
        n_b = win_uv_ref.shape[0] // LANES
        flip = lambda blk: blk.astype(F32).T.astype(BF16)

        @pl.when(step < n_a)
        def _flip_qkvg():
            qkvg_t_ref[...] = flip(win_qkvg_ref[pl.ds(pl.multiple_of(step * LANES, LANES), LANES), :])

        @pl.when((step >= n_a) & (step < n_a + n_b))
        def _flip_uv():
            uv_t_ref[...] = flip(win_uv_ref[pl.ds(pl.multiple_of((step - n_a) * LANES, LANES), LANES), :])

        @pl.when(step == n_a + n_b)
        def _flip_ba():
            ba_t_ref[...] = flip(wba_ref[...])
    R = nseq * tq
    t_idx = pl.program_id(1)

    @pl.when(t_idx == 0)
    def _init():
        conv_ref[...] = jnp.zeros(conv_ref.shape, F32)
        if has_state:
            conv_ref[:, SUBLANES - (GDN_CONV - 1):, :] = conv0_ref[...]
            delta_ref[...] = delta0_ref[...]
        else:
            delta_ref[...] = jnp.zeros(delta_ref.shape, F32)

    cat = lambda parts, axis: parts[0] if len(parts) == 1 else jnp.concatenate(parts, axis=axis)
    pr = min(R, AB_PROJ_ROWS)
    spp = pr // tq
    cw = convw_ref[...]
    z_parts, zba_parts, qkv_parts = [], [], []
    for i in range(R // pr):
        xn = _rms(h_ref[i * spp:(i + 1) * spp].reshape(pr, D_MODEL), gmix_ref[...]).astype(BF16)
        zi = jnp.concatenate([win_dot(xn, win_qkvg_ref[...]), win_dot(xn, win_uv_ref[...])], axis=1)
        z_parts.append(zi)
        zba_parts.append(win_dot(xn, wba_ref[...]))
        pre = zi[:, :D_QKV].reshape(spp, tq, D_QKV)
        xext = jnp.concatenate([conv_ref[i * spp:(i + 1) * spp], pre], axis=1)
        acc = pre * cw[3:4, :]
        for k in range(1, GDN_CONV):
            acc = acc + pltpu.roll(xext, k, axis=1)[:, SUBLANES:, :] * cw[3 - k:4 - k, :]
        conv_ref[i * spp:(i + 1) * spp] = xext[:, tq:, :]
        qkv_parts.append(_silu(acc).reshape(pr, D_QKV))
    z = cat(z_parts, 0)
    zba = cat(zba_parts, 0)
    qkv = cat(qkv_parts, 0)

    gpar = gpar_ref[...]
    beta_all = _sigmoid(zba)
    g_all = -jnp.exp(gpar[0:1, :]) * _softplus(zba + gpar[1:2, :])

    sub = GDN_SUB
    log2c = _log2(c)
    row = lax.broadcasted_iota(jnp.int32, (sub, sub), 0)
    col = lax.broadcasted_iota(jnp.int32, (sub, sub), 1)
    cblk = lax.shift_right_logical(col, log2c)
    same = lax.shift_right_logical(row, log2c) == cblk
    tril = same & (row >= col)
    strict = same & (row > col)
    tril_f = tril.astype(F32)
    eye = (row == col).astype(F32)
    cblk_row = cblk[0:1, :]
    nblk = sub // c
    blk_per_seq = tq // c
    nsub = R // sub
    pairs = [(st, h) for st in range(nsub) for h in range(GDN_HEADS)]
    gl = lambda h: slice(GDN_HEADS + h, GDN_HEADS + h + 1)
    gc_alls, glasts, grems, gc_ts = [], [], [], []
    for st in range(nsub):
        gc_all = _dot_exact(tril_f, g_all[st * sub:(st + 1) * sub])
        glast = [gc_all[(j + 1) * c - 1:(j + 1) * c, :] for j in range(nblk)]
        gc_alls.append(gc_all)
        glasts.append(glast)
        grems.append(cat([jnp.broadcast_to(gl, (c, LANES)) for gl in glast], 0) - gc_all)
        gc_ts.append(gc_all.T)
    qs, ks, vs, betas, egs, decays, qkks = {}, {}, {}, {}, {}, {}, {}
    for p in pairs:
        st, h = p
        r0, lo = st * sub, h * GDN_DK
        q = qkv[r0:r0 + sub, lo:lo + GDN_DK]
        k = qkv[r0:r0 + sub, D_A + lo:D_A + lo + GDN_DK]
        qs[p] = q * lax.rsqrt(jnp.sum(q * q, axis=-1, keepdims=True) + EPS) * (GDN_DK ** -0.5)
        ks[p] = k * lax.rsqrt(jnp.sum(k * k, axis=-1, keepdims=True) + EPS)
        vs[p] = qkv[r0:r0 + sub, 2 * D_A + lo:2 * D_A + lo + GDN_DK]
        betas[p] = beta_all[r0:r0 + sub, h:h + 1]
        gc = gc_alls[st][:, gl(h)]
        egs[p] = jnp.exp(gc)
        decays[p] = jnp.where(tril, jnp.exp(jnp.where(tril, gc - gc_ts[st][gl(h), :], 0.0)), 0.0)
    for p in pairs:
        kb = ks[p].astype(BF16)
        qkks[p] = _dot_nt(jnp.concatenate([qs[p].astype(BF16), kb], axis=0), kb)
    xks, tinvs, qkds = {}, {}, {}
    for p in pairs:
        qkds[p] = (qkks[p][:sub] * decays[p]).astype(BF16)
        xks[p] = jnp.where(strict, -(betas[p] * qkks[p][sub:] * decays[p]), 0.0)
        tinvs[p] = eye + xks[p]
    if log2c > 1:
        for p in pairs:
            xb = xks[p].astype(BF16)
            xks[p] = _dot(xb, xb)
    for lev in range(1, log2c):
        res = {}
        for p in pairs:
            xb = xks[p].astype(BF16)
            if lev < log2c - 1:
                res[p] = _dot(jnp.concatenate([xb, tinvs[p].astype(BF16)], axis=0), xb)
            else:
                res[p] = _dot(tinvs[p].astype(BF16), xb)
        for p in pairs:
            if lev < log2c - 1:
                xks[p] = res[p][:sub]
                tinvs[p] = tinvs[p] + res[p][sub:]
            else:
                tinvs[p] = tinvs[p] + res[p]
    uws, q_heads, kts = {}, {}, {}
    for p in pairs:
        rhs = jnp.concatenate([(vs[p] * betas[p]).astype(BF16),
                               (ks[p] * (betas[p] * egs[p])).astype(BF16)], axis=1)
        uws[p] = _dot(tinvs[p].astype(BF16), rhs)
        q_heads[p] = qs[p] * egs[p]
        st, h = p
        kts[p] = (ks[p] * jnp.exp(grems[st][:, gl(h)])).T
    u_lists = {p: [] for p in pairs}
    qs_lists = {p: [] for p in pairs}
    for kstep in range(blk_per_seq):
        items = []
        for seq in range(nseq):
            st, j = divmod((seq * tq + kstep * c) // c, nblk)
            items += [(seq, st, j, h) for h in range(GDN_HEADS)]
        s_olds, tmps = {}, {}
        for it in items:
            seq, st, j, h = it
            p = (st, h)
            sl = slice(j * c, (j + 1) * c)
            s_olds[it] = delta_ref[seq, h]
            both = jnp.concatenate([uws[p][sl, GDN_DK:], q_heads[p][sl]], axis=0).astype(BF16)
            tmps[it] = _dot(both, s_olds[it].astype(BF16))
        for it in items:
            seq, st, j, h = it
            p = (st, h)
            sl = slice(j * c, (j + 1) * c)
            u_base = uws[p][:, :GDN_DK]
            u_lists[p].append(u_base[sl] - tmps[it][:c])
            qs_lists[p].append(tmps[it][c:])
            u_full = cat(u_lists[p] + ([u_base[(j + 1) * c:]] if j + 1 < nblk else []), 0)
            kt_j = jnp.where(cblk_row == j, kts[p], 0.0).astype(BF16)
            delta_ref[seq, h] = (s_olds[it] * jnp.exp(glasts[st][j][:, gl(h)])
                                 + _dot(kt_j, u_full.astype(BF16)))
    o_raw = {}
    for p in pairs:
        o_raw[p] = cat(qs_lists[p], 0) + _dot(qkds[p], cat(u_lists[p], 0).astype(BF16))
    o_tiles = []
    for st in range(nsub):
        o_heads = []
        for h in range(GDN_HEADS):
            zg = z[st * sub:(st + 1) * sub, D_QKV + h * GDN_DK:D_QKV + (h + 1) * GDN_DK]
            o_heads.append(_rms(o_raw[(st, h)], normo_ref[...]) * _silu(zg))
        o_tiles.append(jnp.concatenate(o_heads, axis=1))
    o_a = cat(o_tiles, 0)

    zu = z[:, D_QKV + D_A:D_QKV + D_A + D_B]
    zv = z[:, D_QKV + D_A + D_B:]
    u_act = _gelu_tanh(zu)
    gv = _gelu_tanh(zv)
    vc = gv - jnp.mean(gv, axis=-1, keepdims=True)
    var = jnp.mean(vc * vc, axis=-1, keepdims=True)
    v_ln = vc * lax.rsqrt(var + EPS) * lng_ref[...] + lnb_ref[...]
    rv = vrows_ref.shape[1]
    vrows_ref[...] = v_ln.reshape(nseq, tq, D_B)[:, tq - rv:, :]
    log2cs = _log2(cs)
    srow = lax.broadcasted_iota(jnp.int32, (SGU_CHUNK, SGU_CHUNK), 0)
    scol = lax.broadcasted_iota(jnp.int32, (SGU_CHUNK, SGU_CHUNK), 1)
    smask = (lax.shift_right_logical(srow, log2cs) == lax.shift_right_logical(scol, log2cs)) & (srow >= scol)
    v_bf = v_ln.astype(BF16)
    mixed_groups = []
    for g in range(SGU_GROUPS):
        wg = jnp.where(smask, wsp_ref[g], 0.0).astype(BF16)
        bg = bsp_ref[g]
        parts = []
        for r0 in range(0, R, SGU_CHUNK):
            parts.append(_dot(wg, v_bf[r0:r0 + SGU_CHUNK, g * LANES:(g + 1) * LANES]) + bg)
        mixed_groups.append(parts[0] if len(parts) == 1 else jnp.concatenate(parts, axis=0))
    o_b = u_act * jnp.concatenate(mixed_groups, axis=1)

    y_in = jnp.concatenate([o_a, o_b], axis=1).astype(BF16)
    for i in range(R // pr):
        seqs = slice(i * spp, (i + 1) * spp)
        proj = _dot(y_in[i * pr:(i + 1) * pr], wout_ref[...])
        hout_ref[seqs] = h_ref[seqs] + proj.reshape(spp, tq, D_MODEL)


def _ab_layer(h, states, w, *, nseq, tq, c, cs, casts=(), out_major=False):
    nb, t, _ = h.shape
    has_state = states is not None
    grid = (nb // nseq, t // tq)
    nsteps = grid[0] * grid[1]
    flips, flip_specs = (), ()
    if out_major:
        n_a, n_b = w['win_qkvg'].shape[0] // LANES, w['win_uv'].shape[0] // LANES
        assert n_a + n_b + 1 <= nsteps
        step = lambda b, s: b * grid[1] + s
        flips = tuple(jax.ShapeDtypeStruct(w[k].shape[::-1], BF16) for k in ('win_qkvg', 'win_uv', 'wba'))
        flip_specs = (
            pl.BlockSpec((D_MODEL, LANES), lambda b, s: (0, jnp.minimum(step(b, s), n_a - 1))),
            pl.BlockSpec((D_MODEL, LANES), lambda b, s: (0, jnp.clip(step(b, s) - n_a, 0, n_b - 1))),
            pl.BlockSpec((D_MODEL, LANES), lambda b, s: (0, 0)))

    def slab(a):
        rows = a.shape[-2] // nsteps
        assert rows * nsteps == a.shape[-2] and rows % (2 * SUBLANES) == 0, a.shape
        lead = a.ndim - 2
        return pl.BlockSpec(a.shape[:lead] + (rows, a.shape[-1]),
                            lambda b, s: (0,) * lead + (b * grid[1] + s, 0))
    rv = min(SGU_CHUNK, tq)
    seq_blk = lambda shape: pl.BlockSpec(shape, lambda b, s: (b,) + (0,) * (len(shape) - 1))
    full = lambda a: pl.BlockSpec(a.shape, lambda b, s: (0,) * a.ndim)
    params = [w['gmix0'], w['win_qkvg'], w['win_uv'], w['wba'], w['convw'], w['gpar'], w['normo'], w['lng'], w['lnb'],
              w['wsp'], w['bsp'], w['wout_ab']]
    out_shapes = (
        jax.ShapeDtypeStruct(h.shape, F32),
        jax.ShapeDtypeStruct((nb, SUBLANES, D_QKV), F32),
        jax.ShapeDtypeStruct((nb, GDN_HEADS, GDN_DK, GDN_DK), F32),
        jax.ShapeDtypeStruct((nb, rv, D_B), F32),
    ) + tuple(jax.ShapeDtypeStruct(a.shape, BF16) for a in casts) + flips
    outs = pl.pallas_call(
        functools.partial(_ab_body, nseq=nseq, tq=tq, c=c, cs=cs, has_state=has_state, n_cast=len(casts),
                          out_major=out_major),
        grid=grid,
        in_specs=[pl.BlockSpec((nseq, tq, D_MODEL), lambda b, s: (b, s, 0))]
        + ([seq_blk((nseq, GDN_CONV - 1, D_QKV)), seq_blk((nseq, GDN_HEADS, GDN_DK, GDN_DK))]
           if has_state else []) + [full(a) for a in params] + [slab(a) for a in casts],
        out_specs=(pl.BlockSpec((nseq, tq, D_MODEL), lambda b, s: (b, s, 0)),
                   seq_blk((nseq, SUBLANES, D_QKV)),
                   seq_blk((nseq, GDN_HEADS, GDN_DK, GDN_DK)),
                   seq_blk((nseq, rv, D_B))) + tuple(slab(a) for a in casts) + flip_specs,
        out_shape=out_shapes,
        compiler_params=pltpu.CompilerParams(
            dimension_semantics=("arbitrary", "arbitrary"), vmem_limit_bytes=VMEM_LIMIT_BYTES),
        name="ab_mixer",
    )(h, *(states if has_state else ()), *params, *casts)
    n = 4 + len(casts)
    return outs[:4], outs[4:n], outs[n:]


def _cd_body(*refs, nseq, tq, pos0, has_state):
    h_ref, refs = refs[0], refs[1:]
    if has_state:
        (pool0_ref, s5re0_ref, s5im0_ref), refs = refs[:3], refs[3:]
    (gmix_ref, win_ref, wpool_ref, pscale_ref, wb_ref, lam_re_ref, lam_im_ref, wcre_ref, wcim_ref,
     dskip_ref, wglu_ref, bglu_ref, wout_ref, hout_ref, pool_ref, s5re_ref, s5im_ref, bu_ref) = refs
    R = nseq * tq
    t_idx = pl.program_id(1)

    @pl.when(t_idx == 0)
    def _init():
        if has_state:
            pool_ref[...] = pool0_ref[...]
            s5re_ref[...] = s5re0_ref[...]
            s5im_ref[...] = s5im0_ref[...]
        else:
            pool_ref[...] = jnp.zeros(pool_ref.shape, F32)
            s5re_ref[...] = jnp.zeros(s5re_ref.shape, F32)
            s5im_ref[...] = jnp.zeros(s5im_ref.shape, F32)

    x3 = h_ref[...]
    xn3 = _rms(x3, gmix_ref[...]).astype(BF16)
    nper = R // CD_PERM_ROWS
    tsub = tq // nper
    log2n = _log2(nseq)
    ri = lax.broadcasted_iota(jnp.int32, (CD_PERM_ROWS, CD_PERM_ROWS), 0)
    ci = lax.broadcasted_iota(jnp.int32, (CD_PERM_ROWS, CD_PERM_ROWS), 1)
    to_tm = (ci == (ri & (nseq - 1)) * tsub + lax.shift_right_logical(ri, log2n)).astype(BF16)
    to_sm = (ri == (ci & (nseq - 1)) * tsub + lax.shift_right_logical(ci, log2n)).astype(BF16)
    xt = jnp.concatenate(
        [_dot(to_tm, xn3[:, i * tsub:(i + 1) * tsub, :].reshape(CD_PERM_ROWS, D_MODEL)).astype(BF16)
         for i in range(nper)], axis=0)
    z = _dot(xt, win_ref[...])
    xc = z[:, :D_C].reshape(tq, nseq, D_C)
    xd = z[:, D_C:]

    xext = jnp.concatenate([pool_ref[...], xc], axis=0)
    pool_ref[...] = xext[tq:]
    pos = pos0 + t_idx * tq + lax.broadcasted_iota(jnp.int32, (tq, 1, 1), 0)
    sums = xext
    outs = []
    for gi, win in enumerate(POOL_WINDOWS):
        sums = sums[win // 2:] + sums[:-(win // 2)]
        cnt = jnp.minimum(win, pos + 1).astype(F32)
        sl = slice(gi * LANES, (gi + 1) * LANES)
        m = sums[sums.shape[0] - tq:, :, sl] / cnt - xc[:, :, sl]
        outs.append(_dot(m.reshape(R, LANES).astype(BF16), wpool_ref[gi]))
    y_c = jnp.concatenate(outs, axis=1) * pscale_ref[...]

    xd_bf = xd.astype(BF16)
    for half in range(2):
        res = _dot(xd_bf[:, half * D_D // 2:(half + 1) * D_D // 2], wb_ref[half])
        bu_ref[:, half * S5_HALF:(half + 1) * S5_HALF] = res[:, :S5_HALF]
        bu_ref[:, S5_MODES + half * S5_HALF:S5_MODES + (half + 1) * S5_HALF] = res[:, S5_HALF:]
    piece = S5_SCAN_VREGS * SUBLANES * LANES // (2 * nseq)
    pieces = range(0, S5_MODES, piece)
    lam = {p0: (jnp.broadcast_to(lam_re_ref[:, p0:p0 + piece], (nseq, piece)),
                jnp.broadcast_to(lam_im_ref[:, p0:p0 + piece], (nseq, piece))) for p0 in pieces}
    carry = {p0: (s5re_ref[:, p0:p0 + piece], s5im_ref[:, p0:p0 + piece]) for p0 in pieces}
    kn = wcre_ref.shape[1]
    for i in range(nper):
        rows = slice(i * CD_PERM_ROWS, (i + 1) * CD_PERM_ROWS)
        for p0 in pieces:
            lr, li = lam[p0]
            re, im = carry[p0]
            for t in range(i * tsub, (i + 1) * tsub):
                trows = slice(t * nseq, (t + 1) * nseq)
                re, im = (lr * re - li * im + bu_ref[trows, p0:p0 + piece],
                          lr * im + li * re + bu_ref[trows, S5_MODES + p0:S5_MODES + p0 + piece])
                bu_ref[trows, p0:p0 + piece] = re
                bu_ref[trows, S5_MODES + p0:S5_MODES + p0 + piece] = im
            carry[p0] = (re, im)
        ys = []
        for n in range(wcre_ref.shape[0]):
            s_re = bu_ref[rows, n * kn:(n + 1) * kn].astype(BF16)
            s_im = bu_ref[rows, S5_MODES + n * kn:S5_MODES + (n + 1) * kn].astype(BF16)
            ys.append(_dot(s_re, wcre_ref[n]) - _dot(s_im, wcim_ref[n]))
        y = jnp.concatenate(ys, axis=1) + dskip_ref[...] * xd[rows]
        y = _gelu_tanh(y)
        y_d = y * _sigmoid(_dot(y.astype(BF16), wglu_ref[...]) + bglu_ref[...])
        y_in = jnp.concatenate([y_c[rows], y_d], axis=1).astype(BF16)
        proj = _dot(_dot(to_sm, y_in).astype(BF16), wout_ref[...])
        ts = slice(i * tsub, (i + 1) * tsub)
        hout_ref[:, ts, :] = x3[:, ts, :] + proj.reshape(nseq, tsub, D_MODEL)
    for p0 in pieces:
        s5re_ref[:, p0:p0 + piece], s5im_ref[:, p0:p0 + piece] = carry[p0]


def _cd_layer(h, states, w, *, nseq, tq, pos0):
    nb, t, _ = h.shape
    grid = (nb // nseq, t // tq)
    full = lambda a: pl.BlockSpec(a.shape, lambda b, s: (0,) * a.ndim)
    params = [w['gmix1'], w['win_cd'], w['wpool'], w['pscale'], w['wb'], w['lam_re'], w['lam_im'],
              w['wcre'], w['wcim'], w['dskip'], w['wglu'], w['bglu'], w['wout_cd']]
    h_spec = pl.BlockSpec((nseq, tq, D_MODEL), lambda b, s: (b, s, 0))
    pool_spec = pl.BlockSpec((2 * SUBLANES, nseq, D_C), lambda b, s: (0, b, 0))
    s5_spec = pl.BlockSpec((nseq, S5_MODES), lambda b, s: (b, 0))
    state_specs = [pool_spec, s5_spec, s5_spec]
    has_state = states is not None
    out_shapes = (
        jax.ShapeDtypeStruct(h.shape, F32),
        jax.ShapeDtypeStruct((2 * SUBLANES, nb, D_C), F32),
        jax.ShapeDtypeStruct((nb, S5_MODES), F32),
        jax.ShapeDtypeStruct((nb, S5_MODES), F32),
    )
    return pl.pallas_call(
        functools.partial(_cd_body, nseq=nseq, tq=tq, pos0=pos0, has_state=has_state),
        grid=grid,
        in_specs=[h_spec] + (state_specs if has_state else []) + [full(a) for a in params],
        out_specs=tuple([h_spec] + state_specs),
        out_shape=out_shapes,
        scratch_shapes=[pltpu.VMEM((nseq * tq, 2 * S5_MODES), F32)],
        compiler_params=pltpu.CompilerParams(
            dimension_semantics=("arbitrary", "arbitrary"), vmem_limit_bytes=VMEM_LIMIT_BYTES),
        name="cd_mixer",
    )(h, *(states if has_state else ()), *params)


def _ffn_pe_body(h_ref, p_ref, gffn_ref, wup_ref, wdn_ref, gpe_ref, wgate_ref, wproj_ref, gfin_ref,
                 out_ref, *, final):
    x = h_ref[...]
    xn = _rms(x, gffn_ref[...]).astype(BF16)
    h = x
    ffc = D_FF // 4
    for ci in range(4):
        a = jnp.maximum(_dot(xn, wup_ref[:, ci * ffc:(ci + 1) * ffc]), 0.0)
        h = h + _dot((a * a).astype(BF16), wdn_ref[ci * ffc:(ci + 1) * ffc, :])
    hn = _rms(h, gpe_ref[...]).astype(BF16)
    gate = _sigmoid(_dot(hn, wgate_ref[...]))
    h = h + _dot(p_ref[...].astype(BF16), wproj_ref[...]) * gate
    if final:
        h = _rms(h, gfin_ref[...])
    out_ref[...] = h


def _ffn_pe(h2, p3, w, layer, *, tm, final):
    n = h2.shape[0]
    of_layer = lambda a: pl.BlockSpec((None,) + a.shape[1:], lambda i: (layer,) + (0,) * (a.ndim - 1),
                                      pipeline_mode=pl.Buffered(1))
    params = [w['gffn'], w['wup'], w['wdn'], w['gpe'], w['wgate'], w['wproj']]
    return pl.pallas_call(
        functools.partial(_ffn_pe_body, final=final),
        grid=(n // tm,),
        in_specs=[pl.BlockSpec((tm, D_MODEL), lambda i: (i, 0)),
                  pl.BlockSpec((None, tm, D_PLE), lambda i: (layer, i, 0))]
        + [of_layer(a) for a in params] + [pl.BlockSpec(w['gfin'].shape, lambda i: (0, 0))],
        out_specs=pl.BlockSpec((tm, D_MODEL), lambda i: (i, 0)),
        out_shape=jax.ShapeDtypeStruct(h2.shape, F32),
        compiler_params=pltpu.CompilerParams(
            dimension_semantics=("arbitrary",), vmem_limit_bytes=VMEM_LIMIT_BYTES),
        name="ffn_pe",
    )(h2, p3, *params, w['gfin'])


def _s5_prep_body(lre_ref, lim_ref, ldt_ref, bre_ref, bim_ref, lbre_ref, lbim_ref, bbre_ref, bbim_ref):
    lr = lre_ref[...]
    li = lim_ref[...]
    dt = jnp.exp(ldt_ref[...])
    mag = jnp.exp(lr * dt)
    ang = li * dt
    lb_re = mag * jnp.cos(ang)
    lb_im = mag * jnp.sin(ang)
    lbre_ref[...] = lb_re
    lbim_ref[...] = lb_im
    a = lb_re - 1.0
    b = lb_im
    den = lr * lr + li * li
    cr = (a * lr + b * li) / den
    ci = (b * lr - a * li) / den
    bre = bre_ref[...]
    bim = bim_ref[...]
    bbre_ref[...] = cr * bre - ci * bim
    bbim_ref[...] = cr * bim + ci * bre


def _s5_prep(lam_re, lam_im, log_dt, b_re, b_im):
    ldt = jnp.broadcast_to(log_dt[:, None], (S5_GROUPS, S5_STATE))
    rows = [a.reshape(1, S5_MODES) for a in (lam_re, lam_im, ldt)]
    chan_major = lambda a: a.reshape(S5_MODES, S5_GW).T
    shp = jax.ShapeDtypeStruct((S5_GW, S5_MODES), F32)
    row_shp = jax.ShapeDtypeStruct((1, S5_MODES), F32)
    lb_re, lb_im, bb_re, bb_im = pl.pallas_call(
        _s5_prep_body, out_shape=(row_shp, row_shp, shp, shp), name="s5_prep",
    )(*rows, chan_major(b_re), chan_major(b_im))
    to_gcn = lambda a: jnp.swapaxes(a.reshape(S5_GW, S5_GROUPS, S5_STATE), 0, 1)
    return lb_re, lb_im, to_gcn(bb_re), to_gcn(bb_im)


def _block_diag(a, ngrp):
    p, q = a.shape[1], a.shape[2]
    eye = jnp.eye(ngrp, dtype=a.dtype)
    a = a.reshape(S5_GROUPS // ngrp, ngrp, p, q)
    return jnp.einsum('hgpq,gk->hgpkq', a, eye).reshape(S5_GROUPS // ngrp, ngrp * p, ngrp * q)


def _prepare_weights(norm_mix, norm_ffn, norm_pe, norm_final, w_in_ab, conv_qkv, a_log, dt_bias, norm_o,
                     ln_v_gain, ln_v_bias, w_spatial, b_spatial, w_out_ab, w_in_cd, w_pool, pool_scale,
                     lam_re, lam_im, log_dt, b_re, b_im, c_re, c_im, d_skip, w_glu, b_glu, w_out_cd,
                     w_ffn_up, w_ffn_down, w_pe_proj, w_pe_gate):
    row = lambda a: a.reshape(1, -1)
    w = {}
    w['gmix0'] = row(norm_mix[0])
    w['gmix1'] = row(norm_mix[1])
    wt = jnp.swapaxes(w_in_ab[0], 0, 1)
    cut0 = D_QKV + D_A
    cut1 = cut0 + 2 * GDN_HEADS
    w['win_qkvg'] = wt[:cut0].astype(BF16)
    w['win_uv'] = wt[cut1:].astype(BF16)
    w['wba'] = jnp.pad(wt[cut0:cut1], ((0, LANES - 2 * GDN_HEADS), (0, 0))).astype(BF16)
    w['convw'] = jnp.pad(conv_qkv[0], ((0, SUBLANES - 4), (0, 0)))
    w['gpar'] = jnp.pad(jnp.concatenate([a_log, dt_bias], axis=0),
                        ((0, 0), (GDN_HEADS, LANES - 2 * GDN_HEADS)))
    w['normo'] = row(norm_o[0])
    w['lng'] = row(ln_v_gain[0])
    w['lnb'] = row(ln_v_bias[0])
    w['wout_ab'] = w_out_ab[0].astype(BF16)
    w['w_spatial'] = w_spatial[0]
    w['b_spatial'] = b_spatial[0]
    w['wpool'] = w_pool[0].astype(BF16)
    w['pscale'] = row(pool_scale[0])
    lb_re, lb_im, bb_re, bb_im = _s5_prep(lam_re[0], lam_im[0], log_dt[0], b_re[0], b_im[0])
    w['lam_re'] = lb_re
    w['lam_im'] = lb_im
    half = S5_GROUPS // 2
    w['wb'] = jnp.concatenate([_block_diag(bb_re, half), _block_diag(bb_im, half)],
                              axis=2).astype(BF16)
    per_tile = LANES // S5_GW
    w['wcre'] = _block_diag(jnp.swapaxes(c_re[0], 1, 2), per_tile).astype(BF16)
    w['wcim'] = _block_diag(jnp.swapaxes(c_im[0], 1, 2), per_tile).astype(BF16)
    w['dskip'] = row(d_skip[0])
    w['bglu'] = row(b_glu[0])
    w['gffn'] = norm_ffn.reshape(2, 1, D_MODEL)
    w['gpe'] = norm_pe.reshape(2, 1, D_MODEL)
    w['gfin'] = row(norm_final)
    w['wproj'] = w_pe_proj.astype(BF16)
    return w


def _mix_ab(x, states, w, *, tile, casts=(), out_major=False):
    nseq, tq = tile
    assert tq % SGU_CHUNK == 0 or (tq == x.shape[1] and tq < SGU_CHUNK), (tq, x.shape)
    c = min(64, tq)
    cs = min(SGU_CHUNK, tq)
    reps = SGU_CHUNK // cs
    wl = dict(w)
    wl['wsp'] = jnp.tile(w['w_spatial'][:, :cs, :cs], (1, reps, reps))
    wl['bsp'] = jnp.tile(w['b_spatial'][:, :cs], (1, reps))[:, :, None]
    (h, conv_o, delta_o, vrows), cast_out, flipped = _ab_layer(
        x, states, wl, nseq=nseq, tq=tq, c=c, cs=cs, casts=casts, out_major=out_major)
    return h, (conv_o[None, :, SUBLANES - (GDN_CONV - 1):, :], delta_o[None], vrows[None]), cast_out, flipped


def _mix_cd(h, states, pos0, w, *, tile):
    nb = h.shape[0]
    if states is not None:
        pool0, s5re0, s5im0 = states
        pool_in = jnp.swapaxes(jnp.pad(pool0, ((0, 0), (2 * SUBLANES - POOL_BUF, 0), (0, 0))), 0, 1)
        states = (pool_in, s5re0.reshape(nb, S5_MODES), s5im0.reshape(nb, S5_MODES))
    h, pool_o, s5re, s5im = _cd_layer(h, states, w, nseq=tile[0], tq=tile[1], pos0=pos0)
    pool_o = jnp.swapaxes(pool_o, 0, 1)[:, 2 * SUBLANES - POOL_BUF:, :]
    return h, (pool_o[None], s5re.reshape(1, nb, S5_GROUPS, S5_STATE), s5im.reshape(1, nb, S5_GROUPS, S5_STATE))


def kernel(x_prompt, x_sample, state_conv, state_delta, state_pool, state_s5_re, state_s5_im, p_prompt, p_sample, norm_mix, norm_ffn, norm_pe, norm_final, w_in_ab, conv_qkv, a_log, dt_bias, norm_o, ln_v_gain, ln_v_bias, w_spatial, b_spatial, w_out_ab, w_in_cd, w_pool, pool_scale, lam_re, lam_im, log_dt, b_re, b_im, c_re, c_im, d_skip, w_glu, b_glu, w_out_cd, w_ffn_up, w_ffn_down, w_pe_proj, w_pe_gate):
    w = _prepare_weights(norm_mix, norm_ffn, norm_pe, norm_final, w_in_ab, conv_qkv, a_log, dt_bias, norm_o,
                         ln_v_gain, ln_v_bias, w_spatial, b_spatial, w_out_ab, w_in_cd, w_pool, pool_scale,
                         lam_re, lam_im, log_dt, b_re, b_im, c_re, c_im, d_skip, w_glu, b_glu, w_out_cd,
                         w_ffn_up, w_ffn_down, w_pe_proj, w_pe_gate)
    bp, tp, _ = x_prompt.shape
    bs, ts, _ = x_sample.shape
    flat = lambda a: a.reshape(-1, a.shape[-1])
    pp3 = p_prompt.reshape(p_prompt.shape[0], bp * tp, D_PLE)
    ps3 = p_sample.reshape(p_sample.shape[0], bs * ts, D_PLE)
    later = (w_ffn_up, w_ffn_down, w_pe_gate, w_in_cd[0], w_out_cd[0], w_glu[0])
    hp, ab_p, later_bf16, in_major = _mix_ab(x_prompt, None, w, tile=(4, 128), casts=later, out_major=True)
    w['wup'], w['wdn'], w['wgate'], w['win_cd'], w['wout_cd'], w['wglu'] = later_bf16
    ws = dict(w)
    ws['win_qkvg'], ws['win_uv'], ws['wba'] = in_major
    hs, ab_s, _, _ = _mix_ab(x_sample, (state_conv[0], state_delta[0]), ws, tile=(16, 8))
    hp = _ffn_pe(flat(hp), pp3, w, 0, tm=FFN_ROWS, final=False)
    hs = _ffn_pe(flat(hs), ps3, w, 0, tm=FFN_ROWS // 2, final=False)
    hp, cd_p = _mix_cd(hp.reshape(bp, tp, D_MODEL), None, 0, w, tile=(8, 128))
    hs, cd_s = _mix_cd(hs.reshape(bs, ts, D_MODEL), (state_pool[0], state_s5_re[0], state_s5_im[0]),
                       PAST_LEN, w, tile=(32, 8))
    yp = _ffn_pe(flat(hp), pp3, w, 1, tm=FFN_ROWS, final=True)
    ys = _ffn_pe(flat(hs), ps3, w, 1, tm=FFN_ROWS // 2, final=True)
    return (yp.reshape(bp, tp, D_MODEL), ys.reshape(bs, ts, D_MODEL)) + ab_p + cd_p + ab_s + cd_s
```
